```python
import jax, jax.numpy as jnp
from jax import lax
import numpy as np

D_MODEL = 2048
BATCH = 2
SEQ = 4096
DEPTH = 4
DEC_BATCH = 32
DEC_SEQ = 1
PAST_LEN = 16384
PAGE_SIZE = 128

N_A_LAYERS = DEPTH // 2
N_B_LAYERS = DEPTH - N_A_LAYERS
POOL_WINDOWS = (2, 4, 8, 16)
N_POOL_GROUPS = len(POOL_WINDOWS)
POOL_WIDTH = D_MODEL
POOL_GROUP = POOL_WIDTH // N_POOL_GROUPS
POOL_BUF = max(POOL_WINDOWS) - 1
HEAD_DIM = 64
N_HEADS = D_MODEL // HEAD_DIM
N_KV_HEADS = N_HEADS // 8
GQA_GROUP = N_HEADS // N_KV_HEADS
ATTN_WIDTH = N_HEADS * HEAD_DIM
WINDOW = 128
BLOCK = WINDOW
RMS_EPS = 1e-6
NEG_INF = -1e30

kernel_name = "yoco_pool_swa_sink_decoder_step"


def rms_norm(x, g):
    xf = x.astype(jnp.float32)
    y = xf * lax.rsqrt(jnp.mean(xf * xf, axis=-1, keepdims=True) + RMS_EPS)
    return (y * g.astype(jnp.float32)).astype(x.dtype)


def alibi_slopes():
    return jnp.exp2(-8.0 * jnp.arange(1, N_HEADS + 1, dtype=jnp.float32) / N_HEADS)


def multiscale_pool(u_ext, first_pos):
    B, L, E = u_ext.shape
    S = L - POOL_BUF
    uf = u_ext.astype(jnp.float32)
    cs = jnp.concatenate([jnp.zeros((B, 1, E), jnp.float32), jnp.cumsum(uf, axis=1)], axis=1)
    t_abs = first_pos + POOL_BUF + jnp.arange(S)
    end = cs[:, POOL_BUF + 1:]
    outs = []
    for g, w in enumerate(POOL_WINDOWS):
        sl = slice(g * POOL_GROUP, (g + 1) * POOL_GROUP)
        start = cs[:, POOL_BUF + 1 - w: POOL_BUF + 1 - w + S, sl]
        cnt = jnp.minimum(t_abs + 1, w).astype(jnp.float32)[None, :, None]
        outs.append((end[..., sl] - start) / cnt)
    mean = jnp.concatenate(outs, axis=-1)
    return (mean - uf[:, POOL_BUF:]).astype(u_ext.dtype)


def pool_layer(x, buf, first_pos, norm_g, w_in, w_grp, scale, w_out):
    h = rms_norm(x, norm_g)
    u, z = jnp.split(h @ w_in, 2, axis=-1)
    u_ext = jnp.concatenate([buf.astype(u.dtype), u], axis=1)
    p = multiscale_pool(u_ext, first_pos)
    B, S, _ = p.shape
    p = jnp.einsum('bsgc,gcd->bsgd', p.reshape(B, S, N_POOL_GROUPS, POOL_GROUP), w_grp)
    y = p.reshape(B, S, POOL_WIDTH) * scale * jax.nn.silu(z)
    return x + y @ w_out, u_ext[:, -POOL_BUF:]


def shared_kv(x, norm_g, w_kv, k_norm_g):
    h = rms_norm(x, norm_g)
    B, S, _ = h.shape
    kv = (h @ w_kv).reshape(B, S, 2, N_KV_HEADS, HEAD_DIM)
    k = rms_norm(kv[:, :, 0], k_norm_g)
    return k, kv[:, :, 1]


def sink_attention(q, k, v, q_pos, k_pos, sinks):
    qg = q.reshape(q.shape[:-2] + (N_KV_HEADS, GQA_GROUP, HEAD_DIM))
    s = jnp.einsum('...qkgd,...skd->...kgqs', qg, k,
                   preferred_element_type=jnp.float32) * (HEAD_DIM ** -0.5)
    dist = (q_pos[..., :, None] - k_pos[..., None, :])[..., None, None, :, :]
    slopes = alibi_slopes().reshape(N_KV_HEADS, GQA_GROUP, 1, 1)
    valid = (dist >= 0) & (dist < WINDOW) & (k_pos[..., None, None, None, :] >= 0)
    s = jnp.where(valid, s - slopes * dist.astype(jnp.float32), NEG_INF)
    sink = sinks.astype(jnp.float32).reshape(N_KV_HEADS, GQA_GROUP, 1, 1)
    m = jnp.maximum(jnp.max(s, axis=-1, keepdims=True), sink)
    p = jnp.exp(s - m)
    p = p / (jnp.sum(p, axis=-1, keepdims=True) + jnp.exp(sink - m))
    o = jnp.einsum('...kgqs,...skd->...qkgd', p.astype(v.dtype), v)
    return o.reshape(o.shape[:-3] + (ATTN_WIDTH,))


def band_blocks(t):
    B, S = t.shape[:2]
    tb = t.reshape(B, S // BLOCK, BLOCK, N_KV_HEADS, HEAD_DIM)
    prev = jnp.concatenate([jnp.zeros_like(tb[:, :1]), tb[:, :-1]], axis=1)
    return jnp.concatenate([prev, tb], axis=2)


def attn_layer(x, attend, norm_g, w_in, q_norm_g, w_out):
    h = rms_norm(x, norm_g)
    q, z = jnp.split(h @ w_in, 2, axis=-1)
    B, S, _ = q.shape
    q = rms_norm(q.reshape(B, S, N_HEADS, HEAD_DIM), q_norm_g)
    o = attend(q).reshape(B, S, ATTN_WIDTH)
    return x + (o * jax.nn.silu(z)) @ w_out


def setup_inputs(seed: int = 0) -> dict:
    key = jax.random.key(seed)
    ks = jax.random.split(key, 20)
    f32 = jnp.float32

    def nrm(k, shape, s):
        return jax.random.normal(k, shape, f32) * s

    win_buf = min(WINDOW, PAST_LEN)
    return {
        "x_prompt": nrm(ks[0], (BATCH, SEQ, D_MODEL), 1.0),
        "x_sample": nrm(ks[1], (DEC_BATCH, DEC_SEQ, D_MODEL), 1.0),
        "state_pool": nrm(ks[2], (N_A_LAYERS, DEC_BATCH, POOL_BUF, POOL_WIDTH), 1.0),
        "cache_k_win": nrm(ks[3], (DEC_BATCH, win_buf, N_KV_HEADS, HEAD_DIM), 1.0),
        "cache_v_win": nrm(ks[4], (DEC_BATCH, win_buf, N_KV_HEADS, HEAD_DIM), 1.0),
        "norm_a": 1.0 + nrm(ks[5], (N_A_LAYERS, D_MODEL), 0.1),
        "w_in_a": nrm(ks[6], (N_A_LAYERS, D_MODEL, 2 * POOL_WIDTH), D_MODEL ** -0.5),
        "w_grp_a": nrm(ks[7], (N_A_LAYERS, N_POOL_GROUPS, POOL_GROUP, POOL_GROUP), POOL_GROUP ** -0.5),
        "scale_a": 1.0 + nrm(ks[8], (N_A_LAYERS, POOL_WIDTH), 0.1),
        "w_out_a": nrm(ks[9], (N_A_LAYERS, POOL_WIDTH, D_MODEL), POOL_WIDTH ** -0.5),
        "norm_kv": 1.0 + nrm(ks[10], (D_MODEL,), 0.1),
        "w_kv": nrm(ks[11], (D_MODEL, 2 * N_KV_HEADS * HEAD_DIM), D_MODEL ** -0.5),
        "k_norm": 1.0 + nrm(ks[12], (HEAD_DIM,), 0.1),
        "norm_b": 1.0 + nrm(ks[13], (N_B_LAYERS, D_MODEL), 0.1),
        "w_in_b": nrm(ks[14], (N_B_LAYERS, D_MODEL, 2 * ATTN_WIDTH), D_MODEL ** -0.5),
        "q_norm": 1.0 + nrm(ks[15], (N_B_LAYERS, HEAD_DIM), 0.1),
        "sinks": nrm(ks[16], (N_B_LAYERS, N_HEADS), 1.0),
        "w_out_b": nrm(ks[17], (N_B_LAYERS, ATTN_WIDTH, D_MODEL), ATTN_WIDTH ** -0.5),
    }


def reference(x_prompt, x_sample, state_pool, cache_k_win, cache_v_win,
              norm_a, w_in_a, w_grp_a, scale_a, w_out_a,
              norm_kv, w_kv, k_norm,
              norm_b, w_in_b, q_norm, sinks, w_out_b):
    hp, hs = x_prompt, x_sample
    Bp, Sp = x_prompt.shape[:2]
    Ss = x_sample.shape[1]
    buf_len = cache_k_win.shape[1]
    pool_zero = jnp.zeros((Bp, POOL_BUF, POOL_WIDTH), x_prompt.dtype)
    pool_p, pool_s = [], []
    for l in range(DEPTH):
        if l < N_A_LAYERS:
            hp, bp = pool_layer(hp, pool_zero, -POOL_BUF,
                                norm_a[l], w_in_a[l], w_grp_a[l], scale_a[l], w_out_a[l])
            hs, bs = pool_layer(hs, state_pool[l], PAST_LEN - POOL_BUF,
                                norm_a[l], w_in_a[l], w_grp_a[l], scale_a[l], w_out_a[l])
            pool_p.append(bp)
            pool_s.append(bs)
            continue
        if l == N_A_LAYERS:
            kp, vp = shared_kv(hp, norm_kv, w_kv, k_norm)
            kn, vn = shared_kv(hs, norm_kv, w_kv, k_norm)
            kp_band, vp_band = band_blocks(kp), band_blocks(vp)
            blk0 = jnp.arange(Sp // BLOCK)[:, None] * BLOCK
            qpos_p = blk0 + jnp.arange(BLOCK)[None]
            kpos_p = blk0 - BLOCK + jnp.arange(2 * BLOCK)[None]
            ks_ext = jnp.concatenate([cache_k_win.astype(kn.dtype), kn], axis=1)
            vs_ext = jnp.concatenate([cache_v_win.astype(vn.dtype), vn], axis=1)
            qpos_s = PAST_LEN + jnp.arange(Ss)
            kpos_s = PAST_LEN - buf_len + jnp.arange(buf_len + Ss)
        j = l - N_A_LAYERS
        snk = sinks[j]
        attend_p = lambda q, snk=snk: sink_attention(
            q.reshape(Bp, Sp // BLOCK, BLOCK, N_HEADS, HEAD_DIM), kp_band, vp_band, qpos_p, kpos_p, snk)
        attend_s = lambda q, snk=snk: sink_attention(q, ks_ext, vs_ext, qpos_s, kpos_s, snk)
        hp = attn_layer(hp, attend_p, norm_b[j], w_in_b[j], q_norm[j], w_out_b[j])
        hs = attn_layer(hs, attend_s, norm_b[j], w_in_b[j], q_norm[j], w_out_b[j])
    pool_state_prompt = jnp.stack(pool_p, axis=0)
    pool_state_sample = jnp.stack(pool_s, axis=0)
    k_win_prompt = kp[:, -WINDOW:]
    v_win_prompt = vp[:, -WINDOW:]
    k_win_sample = ks_ext[:, -buf_len:]
    v_win_sample = vs_ext[:, -buf_len:]
    return (hp, hs, pool_state_prompt, pool_state_sample, k_win_prompt, v_win_prompt, k_win_sample, v_win_sample)
```

```python
import functools

import jax
import jax.numpy as jnp
from jax import lax
from jax.experimental import pallas as pl
from jax.experimental.pallas import tpu as pltpu

D_MODEL = 2048
N_A_LAYERS = 2
N_B_LAYERS = 2
POOL_WINDOWS = (2, 4, 8, 16)
N_POOL_GROUPS = len(POOL_WINDOWS)
POOL_WIDTH = D_MODEL
POOL_GROUP = POOL_WIDTH // N_POOL_GROUPS
POOL_BUF = max(POOL_WINDOWS) - 1
POOL_HIST = POOL_BUF + 1
HEAD_DIM = 64
N_HEADS = D_MODEL // HEAD_DIM
N_KV_HEADS = N_HEADS // 8
GQA_GROUP = N_HEADS // N_KV_HEADS
ATTN_WIDTH = N_HEADS * HEAD_DIM
KV_WIDTH = N_KV_HEADS * HEAD_DIM
WINDOW = 128
BLOCK = WINDOW
PAST_LEN = 16384
RMS_EPS = 1e-6
NEG_INF = -1e30
LANES = 128
PAIR = 2 * HEAD_DIM
VMEM_LIMIT = 56 * 1024 * 1024

BF16 = jnp.bfloat16
F32 = jnp.float32


def _alibi_slope(h):
    return 2.0 ** (-8.0 * (h + 1) / N_HEADS)


def _rms(x, g):
    return x * lax.rsqrt(jnp.mean(x * x, axis=-1, keepdims=True) + RMS_EPS) * g


def _silu(z):
    return z * jax.nn.sigmoid(z)


def _dot(a, b):
    return jnp.dot(a, b, preferred_element_type=F32)


def _resident(shape):
    return pl.BlockSpec(shape, lambda *_: (0,) * len(shape), pipeline_mode=pl.Buffered(1))


def _pool_prompt_kernel(x_ref, g_ref, win_ref, wgrp_ref, scale_ref, wout_ref,
                        out_ref, hist_ref, uext_ref, *, tm):
    i = pl.program_id(1)

    @pl.when(i == 0)
    def _():
        uext_ref[0:POOL_HIST, :] = jnp.zeros((POOL_HIST, POOL_WIDTH), F32)

    x = x_ref[0]
    h = _rms(x, g_ref[...]).astype(BF16)
    u = _dot(h, win_ref[:, :POOL_WIDTH])
    z = _dot(h, win_ref[:, POOL_WIDTH:])
    uext_ref[POOL_HIST:POOL_HIST + tm, :] = u

    pos = i * tm + lax.broadcasted_iota(jnp.int32, (tm, 1), 0)
    mixed = []
    for g, w in enumerate(POOL_WINDOWS):
        cols = slice(g * POOL_GROUP, (g + 1) * POOL_GROUP)
        ug = u[:, cols]
        acc = ug
        for k in range(1, w):
            acc = acc + uext_ref[POOL_HIST - k:POOL_HIST - k + tm, cols]
        inv_cnt = 1.0 / jnp.minimum(pos + 1, w).astype(F32)
        p = acc * inv_cnt - ug
        mixed.append(_dot(p.astype(BF16), wgrp_ref[g]))
    y = jnp.concatenate(mixed, axis=-1) * scale_ref[...] * _silu(z)
    out_ref[0] = x + _dot(y.astype(BF16), wout_ref[...])

    tail = uext_ref[tm:tm + POOL_HIST, :]
    hist_ref[0] = tail
    uext_ref[0:POOL_HIST, :] = tail


def _pool_prompt_layer(x, g, w_in, w_grp, scale, w_out, tm):
    bsz, seq, _ = x.shape
    return pl.pallas_call(
        functools.partial(_pool_prompt_kernel, tm=tm),
        grid=(bsz, seq // tm),
        in_specs=[
            pl.BlockSpec((1, tm, D_MODEL), lambda b, i: (b, i, 0)),
            _resident((1, D_MODEL)),
            _resident((D_MODEL, 2 * POOL_WIDTH)),
            _resident((N_POOL_GROUPS, POOL_GROUP, POOL_GROUP)),
            _resident((1, POOL_WIDTH)),
            _resident((POOL_WIDTH, D_MODEL)),
        ],
        out_specs=[
            pl.BlockSpec((1, tm, D_MODEL), lambda b, i: (b, i, 0)),
            pl.BlockSpec((1, POOL_HIST, POOL_WIDTH), lambda b, i: (b, 0, 0)),
        ],
        out_shape=[
            jax.ShapeDtypeStruct((bsz, seq, D_MODEL), F32),
            jax.ShapeDtypeStruct((bsz, POOL_HIST, POOL_WIDTH), F32),
        ],
        scratch_shapes=[pltpu.VMEM((POOL_HIST + tm, POOL_WIDTH), F32)],
        compiler_params=pltpu.CompilerParams(
            dimension_semantics=("arbitrary", "arbitrary"),
            vmem_limit_bytes=VMEM_LIMIT),
        name="pool_prompt_layer",
    )(x, g, w_in, w_grp, scale, w_out)


def _kv_kernel(x_ref, g_ref, wkv_ref, kg_ref, k_ref, v_ref):
    h = _rms(x_ref[...], g_ref[...]).astype(BF16)
    kv = _dot(h, wkv_ref[...])
    lane = lax.broadcasted_iota(jnp.int32, (1, PAIR), 1)
    for p in range(KV_WIDTH // PAIR):
        kp = kv[:, p * PAIR:(p + 1) * PAIR]
        sq = kp * kp
        lo = jnp.sum(jnp.where(lane < HEAD_DIM, sq, 0.0), axis=-1, keepdims=True)
        hi = jnp.sum(jnp.where(lane >= HEAD_DIM, sq, 0.0), axis=-1, keepdims=True)
        ms = jnp.where(lane < HEAD_DIM, lo, hi) * (1.0 / HEAD_DIM)
        k_ref[:, p * PAIR:(p + 1) * PAIR] = kp * lax.rsqrt(ms + RMS_EPS) * kg_ref[...]
    v_ref[...] = kv[:, KV_WIDTH:]


def _shared_kv(x2d, g, w_kv, k_gain_pair, tm):
    rows = x2d.shape[0]
    return pl.pallas_call(
        _kv_kernel,
        grid=(rows // tm,),
        in_specs=[
            pl.BlockSpec((tm, D_MODEL), lambda i: (i, 0)),
            _resident((1, D_MODEL)),
            _resident((D_MODEL, 2 * KV_WIDTH)),
            _resident((1, PAIR)),
        ],
        out_specs=[
            pl.BlockSpec((tm, KV_WIDTH), lambda i: (i, 0)),
            pl.BlockSpec((tm, KV_WIDTH), lambda i: (i, 0)),
        ],
        out_shape=[
            jax.ShapeDtypeStruct((rows, KV_WIDTH), F32),
            jax.ShapeDtypeStruct((rows, KV_WIDTH), F32),
        ],
        compiler_params=pltpu.CompilerParams(
            dimension_semantics=("arbitrary",), vmem_limit_bytes=VMEM_LIMIT),
        name="shared_kv",
    )(x2d, g, w_kv, k_gain_pair)


def _head_inv_rms(q_sq_pair, lane_lo):
    lo = jnp.sum(jnp.where(lane_lo, q_sq_pair, 0.0), axis=-1, keepdims=True)
    hi = jnp.sum(jnp.where(lane_lo, 0.0, q_sq_pair), axis=-1, keepdims=True)
    c = HEAD_DIM ** -0.5
    return (lax.rsqrt(lo * (1.0 / HEAD_DIM) + RMS_EPS) * c,
            lax.rsqrt(hi * (1.0 / HEAD_DIM) + RMS_EPS) * c)


def _attn_prompt_kernel(sink_ref, x_ref, g_ref, win_ref, qg_ref, kt_ref, vw_ref, wout_ref,
                        out_ref, o_ref, *, tm):
    i = pl.program_id(1)
    nblk = tm // BLOCK

    x = x_ref[0]
    h = _rms(x, g_ref[...]).astype(BF16)
    q = _dot(h, win_ref[:, :ATTN_WIDTH])
    z = _dot(h, win_ref[:, ATTN_WIDTH:])

    lane_lo = (lax.broadcasted_iota(jnp.int32, (1, ATTN_WIDTH), 1) % PAIR) < HEAD_DIM
    pair_lo = lax.broadcasted_iota(jnp.int32, (1, PAIR), 1) < HEAD_DIM
    q_sq = q * q
    qg = q * qg_ref[...]
    q_lo = jnp.where(lane_lo, qg, 0.0).astype(BF16)
    q_hi = jnp.where(lane_lo, 0.0, qg).astype(BF16)

    r_idx = lax.broadcasted_iota(jnp.int32, (BLOCK, 2 * BLOCK), 0)
    c_idx = lax.broadcasted_iota(jnp.int32, (BLOCK, 2 * BLOCK), 1)
    dist = r_idx + BLOCK - c_idx
    dist_f = dist.astype(F32)
    in_window = (dist >= 0) & (dist < WINDOW)

    for blk in range(nblk):
        rows = slice(blk * BLOCK, (blk + 1) * BLOCK)
        first_key = (i * nblk + blk - 1) * BLOCK
        valid = in_window & (first_key + c_idx >= 0)
        for kh in range(N_KV_HEADS):
            kt = kt_ref[0, blk, kh]
            vdup = vw_ref[0, blk, :, kh * PAIR:(kh + 1) * PAIR]
            v_lo = jnp.where(pair_lo, vdup, jnp.zeros_like(vdup))
            v_hi = jnp.where(pair_lo, jnp.zeros_like(vdup), vdup)
            pieces = []
            for pr in range(GQA_GROUP // 2):
                cols = slice((kh * (GQA_GROUP // 2) + pr) * PAIR,
                             (kh * (GQA_GROUP // 2) + pr + 1) * PAIR)
                pieces.append(q_lo[rows, cols])
                pieces.append(q_hi[rows, cols])
            s_all = _dot(jnp.concatenate(pieces, axis=0), kt)
            for pr in range(GQA_GROUP // 2):
                cols = slice((kh * (GQA_GROUP // 2) + pr) * PAIR,
                             (kh * (GQA_GROUP // 2) + pr + 1) * PAIR)
                inv_rms = _head_inv_rms(q_sq[rows, cols], pair_lo)
                probs, inv_den = [], []
                for half in range(2):
                    g_idx = 2 * pr + half
                    head = kh * GQA_GROUP + g_idx
                    s = s_all[g_idx * BLOCK:(g_idx + 1) * BLOCK] * inv_rms[half]
                    s = jnp.where(valid, s - _alibi_slope(head) * dist_f, NEG_INF)
                    sink = sink_ref[head]
                    m = jnp.maximum(jnp.max(s, axis=-1, keepdims=True), sink)
                    p = jnp.exp(s - m)
                    den = jnp.sum(p, axis=-1, keepdims=True) + jnp.exp(sink - m)
                    probs.append(p.astype(BF16))
                    inv_den.append(1.0 / den)
                o_pair = _dot(probs[0], v_lo) + _dot(probs[1], v_hi)
                o_ref[rows, cols] = o_pair * jnp.where(pair_lo, inv_den[0], inv_den[1])

    y = o_ref[...] * _silu(z)
    out_ref[0] = x + _dot(y.astype(BF16), wout_ref[...])


def _attn_prompt_layer(x, sink, g, w_in, q_gain, kt_band, v_band, w_out, tm):
    bsz, seq, _ = x.shape
    nblk = tm // BLOCK
    return pl.pallas_call(
        functools.partial(_attn_prompt_kernel, tm=tm),
        grid=(bsz, seq // tm),
        in_specs=[
            pl.BlockSpec(memory_space=pltpu.SMEM),
            pl.BlockSpec((1, tm, D_MODEL), lambda b, i: (b, i, 0)),
            _resident((1, D_MODEL)),
            _resident((D_MODEL, 2 * ATTN_WIDTH)),
            _resident((1, ATTN_WIDTH)),
            pl.BlockSpec((1, nblk, N_KV_HEADS, PAIR, 2 * BLOCK), lambda b, i: (b, i, 0, 0, 0)),
            pl.BlockSpec((1, nblk, 2 * BLOCK, N_KV_HEADS * PAIR), lambda b, i: (b, i, 0, 0)),
            _resident((ATTN_WIDTH, D_MODEL)),
        ],
        out_specs=pl.BlockSpec((1, tm, D_MODEL), lambda b, i: (b, i, 0)),
        out_shape=jax.ShapeDtypeStruct((bsz, seq, D_MODEL), F32),
        scratch_shapes=[pltpu.VMEM((tm, ATTN_WIDTH), F32)],
        compiler_params=pltpu.CompilerParams(
            dimension_semantics=("arbitrary", "arbitrary"),
            vmem_limit_bytes=VMEM_LIMIT),
        name="attn_prompt_layer",
    )(sink, x, g, w_in, q_gain, kt_band, v_band, w_out)


def _pool_decode_kernel(x_ref, st_ref, g_ref, win_ref, wgrp_ref, scale_ref, wout_ref,
                        out_ref, u_ref):
    x = x_ref[...]
    h = _rms(x, g_ref[...]).astype(BF16)
    u = _dot(h, win_ref[:, :POOL_WIDTH])
    z = _dot(h, win_ref[:, POOL_WIDTH:])
    u_ref[...] = u
    mixed = []
    for g, w in enumerate(POOL_WINDOWS):
        cols = slice(g * POOL_GROUP, (g + 1) * POOL_GROUP)
        ug = u[:, cols]
        acc = ug
        for k in range(1, w):
            acc = acc + st_ref[POOL_BUF - k, :, cols]
        p = acc / float(min(PAST_LEN + 1, w)) - ug
        mixed.append(_dot(p.astype(BF16), wgrp_ref[g]))
    y = jnp.concatenate(mixed, axis=-1) * scale_ref[...] * _silu(z)
    out_ref[...] = x + _dot(y.astype(BF16), wout_ref[...])


def _pool_decode_layer(x, state_t, g, w_in, w_grp, scale, w_out):
    n = x.shape[0]
    return pl.pallas_call(
        _pool_decode_kernel,
        out_shape=[jax.ShapeDtypeStruct((n, D_MODEL), F32),
                   jax.ShapeDtypeStruct((n, POOL_WIDTH), F32)],
        compiler_params=pltpu.CompilerParams(vmem_limit_bytes=VMEM_LIMIT),
        name="pool_decode_layer",
    )(x, state_t, g, w_in, w_grp, scale, w_out)


def _q_decode_kernel(x_ref, g_ref, win_ref, qg_ref, q_ref, z_ref):
    h = _rms(x_ref[...], g_ref[...]).astype(BF16)
    q = _dot(h, win_ref[:, :ATTN_WIDTH])
    z_ref[...] = _dot(h, win_ref[:, ATTN_WIDTH:])
    pair_lo = lax.broadcasted_iota(jnp.int32, (1, PAIR), 1) < HEAD_DIM
    for pr in range(N_HEADS // 2):
        cols = slice(pr * PAIR, (pr + 1) * PAIR)
        qp = q[:, cols]
        inv_lo, inv_hi = _head_inv_rms(qp * qp, pair_lo)
        q_ref[:, cols] = qp * jnp.where(pair_lo, inv_lo, inv_hi) * qg_ref[:, cols]


def _q_decode(x, g, w_in, q_gain):
    n = x.shape[0]
    return pl.pallas_call(
        _q_decode_kernel,
        out_shape=[jax.ShapeDtypeStruct((n, ATTN_WIDTH), F32),
                   jax.ShapeDtypeStruct((n, ATTN_WIDTH), F32)],
        compiler_params=pltpu.CompilerParams(vmem_limit_bytes=VMEM_LIMIT),
        name="q_decode",
    )(x, g, w_in, q_gain)


def _attn_decode_kernel(sink_ref, q_ref, k_ref, v_ref, o_ref):
    n_keys = k_ref.shape[2]
    dist = (n_keys - 1 - lax.broadcasted_iota(jnp.int32, (1, 1, n_keys), 2)).astype(F32)
    g_idx = lax.broadcasted_iota(jnp.int32, (1, GQA_GROUP, 1), 1)
    for kh in range(N_KV_HEADS):
        slope = jnp.zeros((1, GQA_GROUP, 1), F32)
        sink = jnp.zeros((1, GQA_GROUP, 1), F32)
        for g in range(GQA_GROUP):
            head = kh * GQA_GROUP + g
            slope = jnp.where(g_idx == g, _alibi_slope(head), slope)
            sink = jnp.where(g_idx == g, sink_ref[head], sink)
        s = jnp.einsum("bgd,bkd->bgk", q_ref[kh].astype(BF16), k_ref[kh].astype(BF16),
                       preferred_element_type=F32)
        s = s - slope * dist
        m = jnp.maximum(jnp.max(s, axis=-1, keepdims=True), sink)
        p = jnp.exp(s - m)
        den = jnp.sum(p, axis=-1, keepdims=True) + jnp.exp(sink - m)
        o = jnp.einsum("bgk,bkd->bgd", p.astype(BF16), v_ref[kh].astype(BF16),
                       preferred_element_type=F32)
        o_ref[kh] = o / den


def _attn_decode(sink, q4, k4, v4):
    return pl.pallas_call(
        _attn_decode_kernel,
        in_specs=[pl.BlockSpec(memory_space=pltpu.SMEM),
                  pl.BlockSpec(memory_space=pltpu.VMEM),
                  pl.BlockSpec(memory_space=pltpu.VMEM),
                  pl.BlockSpec(memory_space=pltpu.VMEM)],
        out_shape=jax.ShapeDtypeStruct(q4.shape, F32),
        compiler_params=pltpu.CompilerParams(vmem_limit_bytes=VMEM_LIMIT),
        name="attn_decode",
    )(sink, q4, k4, v4)


def _out_decode_kernel(o_ref, z_ref, x_ref, wout_ref, out_ref):
    y = o_ref[...] * _silu(z_ref[...])
    out_ref[...] = x_ref[...] + _dot(y.astype(BF16), wout_ref[...])


def _out_decode(o, z, x, w_out):
    return pl.pallas_call(
        _out_decode_kernel,
        out_shape=jax.ShapeDtypeStruct(x.shape, F32),
        compiler_params=pltpu.CompilerParams(vmem_limit_bytes=VMEM_LIMIT),
        name="out_decode",
    )(o, z, x, w_out)


def _band(t_bf16):
    bsz, seq = t_bf16.shape[:2]
    tb = t_bf16.reshape(bsz, seq // BLOCK, BLOCK, N_KV_HEADS, HEAD_DIM)
    prev = jnp.concatenate([jnp.zeros_like(tb[:, :1]), tb[:, :-1]], axis=1)
    return jnp.concatenate([prev, tb], axis=2)


def kernel(x_prompt, x_sample, state_pool, cache_k_win, cache_v_win, norm_a, w_in_a, w_grp_a, scale_a, w_out_a, norm_kv, w_kv, k_norm, norm_b, w_in_b, q_norm, sinks, w_out_b):
    bsz, seq, _ = x_prompt.shape
    n_dec = x_sample.shape[0]
    assert x_sample.shape[1] == 1 and cache_k_win.shape[1] == WINDOW
    tm_a, tm_b, tm_kv = 256, 256, 512

    w_in_a16, w_grp_a16, w_out_a16 = (w.astype(BF16) for w in (w_in_a, w_grp_a, w_out_a))
    w_in_b16, w_out_b16, w_kv16 = (w.astype(BF16) for w in (w_in_b, w_out_b, w_kv))
    k_gain_pair = jnp.tile(k_norm, 2)[None]
    norm_kv2 = norm_kv[None]

    hp = x_prompt
    hs = x_sample[:, 0]
    state_t = state_pool.transpose(0, 2, 1, 3)
    pool_p, pool_s = [], []
    for l in range(N_A_LAYERS):
        args = (norm_a[l][None], w_in_a16[l], w_grp_a16[l], scale_a[l][None], w_out_a16[l])
        hp, hist = _pool_prompt_layer(hp, *args, tm=tm_a)
        pool_p.append(hist[:, POOL_HIST - POOL_BUF:])
        hs, u_new = _pool_decode_layer(hs, state_t[l], *args)
        pool_s.append(jnp.concatenate([state_pool[l][:, 1:], u_new[:, None]], axis=1))

    kp, vp = _shared_kv(hp.reshape(bsz * seq, D_MODEL), norm_kv2, w_kv16, k_gain_pair, tm_kv)
    kp = kp.reshape(bsz, seq, N_KV_HEADS, HEAD_DIM)
    vp = vp.reshape(bsz, seq, N_KV_HEADS, HEAD_DIM)
    kn, vn = _shared_kv(hs, norm_kv2, w_kv16, k_gain_pair, n_dec)
    kn = kn.reshape(n_dec, 1, N_KV_HEADS, HEAD_DIM)
    vn = vn.reshape(n_dec, 1, N_KV_HEADS, HEAD_DIM)

    kband = _band(kp.astype(BF16)).transpose(0, 1, 3, 4, 2)
    kt_band = jnp.concatenate([kband, kband], axis=3)
    vband = _band(vp.astype(BF16))
    v_band = jnp.concatenate([vband, vband], axis=-1).reshape(
        bsz, seq // BLOCK, 2 * BLOCK, N_KV_HEADS * PAIR)

    k_win_s = jnp.concatenate([cache_k_win[:, 1:], kn], axis=1)
    v_win_s = jnp.concatenate([cache_v_win[:, 1:], vn], axis=1)
    k4 = k_win_s.transpose(2, 0, 1, 3)
    v4 = v_win_s.transpose(2, 0, 1, 3)

    for j in range(N_B_LAYERS):
        q_gain = jnp.tile(q_norm[j], N_HEADS)[None]
        hp = _attn_prompt_layer(hp, sinks[j], norm_b[j][None], w_in_b16[j], q_gain,
                                kt_band, v_band, w_out_b16[j], tm=tm_b)
        qs, zs = _q_decode(hs, norm_b[j][None], w_in_b16[j], q_gain)
        q4 = qs.reshape(n_dec, N_KV_HEADS, GQA_GROUP, HEAD_DIM).transpose(1, 0, 2, 3)
        o4 = _attn_decode(sinks[j], q4, k4, v4)
        os_ = o4.transpose(1, 0, 2, 3).reshape(n_dec, ATTN_WIDTH)
        hs = _out_decode(os_, zs, hs, w_out_b16[j])

    return (hp, hs[:, None], jnp.stack(pool_p, axis=0), jnp.stack(pool_s, axis=0),
            kp[:, -WINDOW:], vp[:, -WINDOW:], k_win_s, v_win_s)
```

```python
import functools
import math

import jax
import jax.numpy as jnp
from jax import lax
from jax.experimental import pallas as pl
from jax.experimental.pallas import tpu as pltpu

D_MODEL = 2048
N_A_LAYERS = 2
N_B_LAYERS = 2
POOL_WINDOWS = (2, 4, 8, 16)
N_POOL_GROUPS = len(POOL_WINDOWS)
POOL_WIDTH = D_MODEL
POOL_GROUP = POOL_WIDTH // N_POOL_GROUPS
POOL_BUF = max(POOL_WINDOWS) - 1
POOL_HIST = POOL_BUF + 1
HEAD_DIM = 64
N_HEADS = D_MODEL // HEAD_DIM
N_KV_HEADS = N_HEADS // 8
GQA_GROUP = N_HEADS // N_KV_HEADS
ATTN_WIDTH = N_HEADS * HEAD_DIM
KV_WIDTH = N_KV_HEADS * HEAD_DIM
WINDOW = 128
BLOCK = WINDOW
PAST_LEN = 16384
RMS_EPS = 1e-6
NEG_INF = -1e30
LOG2E = math.log2(math.e)
MIN_SLOPE = 2.0 ** -8
PAIR = 2 * HEAD_DIM
N_PAIRS = N_HEADS // 2
POOL_W_CHUNK = 128
ATTN_W_CHUNK = 64
DEC_CHUNK = 8
VMEM_LIMIT = 60 * 1024 * 1024

BF16 = jnp.bfloat16
F32 = jnp.float32


def _rms(x, g):
    return x * lax.rsqrt(jnp.mean(x * x, axis=-1, keepdims=True) + RMS_EPS) * g


def _silu(z):
    return z * jax.nn.sigmoid(z)


def _dot(a, b):
    return jnp.dot(a, b, preferred_element_type=F32)


def _resident(shape):
    return pl.BlockSpec(shape, lambda *_: (0,) * len(shape), pipeline_mode=pl.Buffered(1))


def _w_step(s, n_w):
    return jnp.minimum(s, n_w - 1)


def _tile_of(s, n_w, n_tiles):
    return jnp.clip(s - n_w, 0, n_tiles - 1)


def _pair_inv_rms(pair, pair_lo, scale):
    sq = pair * pair
    lo = jnp.sum(jnp.where(pair_lo, sq, 0.0), axis=-1, keepdims=True)
    hi = jnp.sum(jnp.where(pair_lo, 0.0, sq), axis=-1, keepdims=True)
    ms = jnp.where(pair_lo, lo, hi) * (1.0 / HEAD_DIM)
    return lax.rsqrt(ms + RMS_EPS) * scale


def _pool_tail(x_ref, u, z, window_sums, inv_cnt, wgrp_ref, scale_ref, wout_ref):
    mixed = []
    for g in range(N_POOL_GROUPS):
        cols = slice(g * POOL_GROUP, (g + 1) * POOL_GROUP)
        p = window_sums[g] * inv_cnt[g] - u[:, cols]
        mixed.append(_dot(p.astype(BF16), wgrp_ref[g]))
    y = jnp.concatenate(mixed, axis=-1) * scale_ref[...] * _silu(z)
    return x_ref[...] + _dot(y.astype(BF16), wout_ref[...])


def _pool_kernel(x_ref, xs_ref, st_ref, g_ref, win_ref, wgrp_ref, scale_ref, wout_ref,
                 out_ref, hist_ref, outs_ref, unew_ref,
                 win16_ref, wgrp16_ref, wout16_ref, uext_ref, *, tm, tiles_per_seq, n_tiles):
    s = pl.program_id(0)
    wc = POOL_W_CHUNK
    n_w = D_MODEL // wc

    @pl.when(s < n_w)
    def _():
        r0 = pl.multiple_of(s * wc, wc)
        win16_ref[pl.ds(r0, wc), :] = win_ref[...].astype(BF16)
        wout16_ref[pl.ds(r0, wc), :] = wout_ref[...].astype(BF16)
        chunks_per_group = POOL_GROUP // wc
        g0 = pl.multiple_of((s % chunks_per_group) * wc, wc)
        wgrp16_ref[s // chunks_per_group, pl.ds(g0, wc), :] = wgrp_ref[0].astype(BF16)

    @pl.when((s >= n_w) & (s < n_w + n_tiles))
    def _():
        i = (s - n_w) % tiles_per_seq

        @pl.when(i == 0)
        def _():
            uext_ref[0:POOL_HIST, :] = jnp.zeros((POOL_HIST, POOL_WIDTH), F32)

        h = _rms(x_ref[...], g_ref[...]).astype(BF16)
        u = _dot(h, win16_ref[:, :POOL_WIDTH])
        z = _dot(h, win16_ref[:, POOL_WIDTH:])
        uext_ref[POOL_HIST:POOL_HIST + tm, :] = u

        pos = i * tm + lax.broadcasted_iota(jnp.int32, (tm, 1), 0)
        sums, inv_cnt = [], []
        for g, w in enumerate(POOL_WINDOWS):
            cols = slice(g * POOL_GROUP, (g + 1) * POOL_GROUP)
            acc = u[:, cols]
            for k in range(1, w):
                acc = acc + uext_ref[POOL_HIST - k:POOL_HIST - k + tm, cols]
            sums.append(acc)
            inv_cnt.append(1.0 / jnp.minimum(pos + 1, w).astype(F32))
        out_ref[...] = _pool_tail(x_ref, u, z, sums, inv_cnt, wgrp16_ref, scale_ref, wout16_ref)

        tail = uext_ref[tm:tm + POOL_HIST, :]
        hist_ref[0] = tail
        uext_ref[0:POOL_HIST, :] = tail

    @pl.when(s == n_w + n_tiles)
    def _():
        h = _rms(xs_ref[...], g_ref[...]).astype(BF16)
        u = _dot(h, win16_ref[:, :POOL_WIDTH])
        z = _dot(h, win16_ref[:, POOL_WIDTH:])
        unew_ref[...] = u
        sums, inv_cnt = [], []
        for g, w in enumerate(POOL_WINDOWS):
            cols = slice(g * POOL_GROUP, (g + 1) * POOL_GROUP)
            acc = u[:, cols]
            for k in range(1, w):
                acc = acc + st_ref[POOL_BUF - k, :, cols]
            sums.append(acc)
            inv_cnt.append(1.0 / float(min(PAST_LEN + 1, w)))
        outs_ref[...] = _pool_tail(xs_ref, u, z, sums, inv_cnt, wgrp16_ref, scale_ref, wout16_ref)


def _pool_layer(x2d, xs, state_t, g, w_in, w_grp, scale, w_out, *, seq, tm):
    rows = x2d.shape[0]
    n_dec = xs.shape[0]
    n_tiles = rows // tm
    tiles_per_seq = seq // tm
    wc = POOL_W_CHUNK
    n_w = D_MODEL // wc
    chunks_per_group = POOL_GROUP // wc
    tile = lambda s: (_tile_of(s, n_w, n_tiles), 0)
    w_row = lambda s: (_w_step(s, n_w), 0)
    return pl.pallas_call(
        functools.partial(_pool_kernel, tm=tm, tiles_per_seq=tiles_per_seq, n_tiles=n_tiles),
        grid=(n_w + n_tiles + 1,),
        in_specs=[
            pl.BlockSpec((tm, D_MODEL), tile),
            _resident((n_dec, D_MODEL)),
            _resident((POOL_BUF, n_dec, POOL_WIDTH)),
            _resident((1, D_MODEL)),
            pl.BlockSpec((wc, 2 * POOL_WIDTH), w_row),
            pl.BlockSpec((1, wc, POOL_GROUP),
                         lambda s: (_w_step(s, n_w) // chunks_per_group,
                                    _w_step(s, n_w) % chunks_per_group, 0)),
            _resident((1, POOL_WIDTH)),
            pl.BlockSpec((wc, D_MODEL), w_row),
        ],
        out_specs=[
            pl.BlockSpec((tm, D_MODEL), tile),
            pl.BlockSpec((1, POOL_HIST, POOL_WIDTH),
                         lambda s: (_tile_of(s, n_w, n_tiles) // tiles_per_seq, 0, 0)),
            _resident((n_dec, D_MODEL)),
            _resident((n_dec, POOL_WIDTH)),
        ],
        out_shape=[
            jax.ShapeDtypeStruct((rows, D_MODEL), F32),
            jax.ShapeDtypeStruct((rows // seq, POOL_HIST, POOL_WIDTH), F32),
            jax.ShapeDtypeStruct((n_dec, D_MODEL), F32),
            jax.ShapeDtypeStruct((n_dec, POOL_WIDTH), F32),
        ],
        scratch_shapes=[
            pltpu.VMEM((D_MODEL, 2 * POOL_WIDTH), BF16),
            pltpu.VMEM((N_POOL_GROUPS, POOL_GROUP, POOL_GROUP), BF16),
            pltpu.VMEM((POOL_WIDTH, D_MODEL), BF16),
            pltpu.VMEM((POOL_HIST + tm, POOL_WIDTH), F32),
        ],
        compiler_params=pltpu.CompilerParams(
            dimension_semantics=("arbitrary",), vmem_limit_bytes=VMEM_LIMIT),
        name="pool_layer",
    )(x2d, xs, state_t, g, w_in, w_grp, scale, w_out)


def _kv_project(x, g_ref, w16_ref, kg_ref):
    h = _rms(x, g_ref[...]).astype(BF16)
    kv = _dot(h, w16_ref[...])
    pair_lo = lax.broadcasted_iota(jnp.int32, (1, PAIR), 1) < HEAD_DIM
    k_pairs = []
    for p in range(KV_WIDTH // PAIR):
        kp = kv[:, p * PAIR:(p + 1) * PAIR]
        k_pairs.append(kp * _pair_inv_rms(kp, pair_lo, 1.0) * kg_ref[...])
    return k_pairs, kv[:, KV_WIDTH:]


def _dup_heads(col, pair_lo):
    rolled = pltpu.roll(col, HEAD_DIM, 1)
    return jnp.where(pair_lo, col, rolled), jnp.where(pair_lo, rolled, col)


def _kv_kernel(x_ref, xs_ref, g_ref, wkv_ref, kg_ref,
               kt_ref, vb_ref, kwin_ref, vwin_ref, kn_ref, vn_ref,
               w16_ref, pk_ref, pv_ref, *, tm, tiles_per_seq, n_tiles):
    s = pl.program_id(0)
    nblk = tm // BLOCK
    pair_lo = lax.broadcasted_iota(jnp.int32, (1, PAIR), 1) < HEAD_DIM

    @pl.when(s == 0)
    def _():
        w16_ref[...] = wkv_ref[...].astype(BF16)

    @pl.when(s < n_tiles)
    def _():
        @pl.when(s % tiles_per_seq == 0)
        def _():
            pk_ref[...] = jnp.zeros(pk_ref.shape, BF16)
            pv_ref[...] = jnp.zeros(pv_ref.shape, BF16)

        k_pairs, v = _kv_project(x_ref[...], g_ref, w16_ref, kg_ref)
        for p in range(KV_WIDTH // PAIR):
            kwin_ref[0, :, p * PAIR:(p + 1) * PAIR] = k_pairs[p][tm - BLOCK:]
        vwin_ref[0] = v[tm - BLOCK:]
        for jb in range(nblk):
            rows = slice(jb * BLOCK, (jb + 1) * BLOCK)
            for p in range(KV_WIDTH // PAIR):
                k_dup = _dup_heads(k_pairs[p][rows], pair_lo)
                v_dup = _dup_heads(v[rows, p * PAIR:(p + 1) * PAIR], pair_lo)
                for half in range(2):
                    kh = 2 * p + half
                    cols = slice(kh * PAIR, (kh + 1) * PAIR)
                    kt_cur = k_dup[half].T.astype(BF16)
                    v_cur = v_dup[half].astype(BF16)
                    kt_ref[0, jb, kh, :, 0:BLOCK] = pk_ref[kh]
                    kt_ref[0, jb, kh, :, BLOCK:2 * BLOCK] = kt_cur
                    vb_ref[0, jb, 0:BLOCK, cols] = pv_ref[:, cols]
                    vb_ref[0, jb, BLOCK:2 * BLOCK, cols] = v_cur
                    pk_ref[kh] = kt_cur
                    pv_ref[:, cols] = v_cur

    @pl.when(s == n_tiles)
    def _():
        k_pairs, v = _kv_project(xs_ref[...], g_ref, w16_ref, kg_ref)
        for p in range(KV_WIDTH // PAIR):
            kn_ref[:, p * PAIR:(p + 1) * PAIR] = k_pairs[p]
        vn_ref[...] = v


def _shared_kv(x2d, xs, g, w_kv, k_gain_pair, *, seq, tm):
    rows = x2d.shape[0]
    bsz = rows // seq
    n_dec = xs.shape[0]
    n_tiles = rows // tm
    tiles_per_seq = seq // tm
    nblk = tm // BLOCK
    t = lambda s: jnp.minimum(s, n_tiles - 1)
    return pl.pallas_call(
        functools.partial(_kv_kernel, tm=tm, tiles_per_seq=tiles_per_seq, n_tiles=n_tiles),
        grid=(n_tiles + 1,),
        in_specs=[
            pl.BlockSpec((tm, D_MODEL), lambda s: (t(s), 0)),
            _resident((n_dec, D_MODEL)),
            _resident((1, D_MODEL)),
            _resident((D_MODEL, 2 * KV_WIDTH)),
            _resident((1, PAIR)),
        ],
        out_specs=[
            pl.BlockSpec((1, nblk, N_KV_HEADS, PAIR, 2 * BLOCK),
                         lambda s: (t(s) // tiles_per_seq, t(s) % tiles_per_seq, 0, 0, 0)),
            pl.BlockSpec((1, nblk, 2 * BLOCK, N_KV_HEADS * PAIR),
                         lambda s: (t(s) // tiles_per_seq, t(s) % tiles_per_seq, 0, 0)),
            pl.BlockSpec((1, BLOCK, KV_WIDTH), lambda s: (t(s) // tiles_per_seq, 0, 0)),
            pl.BlockSpec((1, BLOCK, KV_WIDTH), lambda s: (t(s) // tiles_per_seq, 0, 0)),
            _resident((n_dec, KV_WIDTH)),
            _resident((n_dec, KV_WIDTH)),
        ],
        out_shape=[
            jax.ShapeDtypeStruct((bsz, seq // BLOCK, N_KV_HEADS, PAIR, 2 * BLOCK), BF16),
            jax.ShapeDtypeStruct((bsz, seq // BLOCK, 2 * BLOCK, N_KV_HEADS * PAIR), BF16),
            jax.ShapeDtypeStruct((bsz, BLOCK, KV_WIDTH), F32),
            jax.ShapeDtypeStruct((bsz, BLOCK, KV_WIDTH), F32),
            jax.ShapeDtypeStruct((n_dec, KV_WIDTH), F32),
            jax.ShapeDtypeStruct((n_dec, KV_WIDTH), F32),
        ],
        scratch_shapes=[
            pltpu.VMEM((D_MODEL, 2 * KV_WIDTH), BF16),
            pltpu.VMEM((N_KV_HEADS, PAIR, BLOCK), BF16),
            pltpu.VMEM((BLOCK, N_KV_HEADS * PAIR), BF16),
        ],
        compiler_params=pltpu.CompilerParams(
            dimension_semantics=("arbitrary",), vmem_limit_bytes=VMEM_LIMIT),
        name="shared_kv",
    )(x2d, xs, g, w_kv, k_gain_pair)


def _normed_q_pair(q, pr, qg_ref, pair_lo, scale):
    cols = slice(pr * PAIR, (pr + 1) * PAIR)
    qp = q[:, cols]
    return qp * _pair_inv_rms(qp, pair_lo, scale) * qg_ref[:, cols]


def _decode_attention(q, k3, v3, sink_col):
    n, n_keys = k3.shape[0], k3.shape[1]
    rows = n * N_HEADS
    ri = lax.broadcasted_iota(jnp.int32, (rows, n), 0)
    ci = lax.broadcasted_iota(jnp.int32, (rows, n), 1)
    rep = jnp.where(ri // N_HEADS == ci, 1.0, 0.0).astype(BF16)
    row_h = lax.broadcasted_iota(jnp.int32, (rows, ATTN_WIDTH), 0) % N_HEADS
    col_h = lax.broadcasted_iota(jnp.int32, (rows, ATTN_WIDTH), 1) // HEAD_DIM
    own = row_h == col_h
    fj = lax.broadcasted_iota(jnp.int32, (ATTN_WIDTH, KV_WIDTH), 0)
    fc = lax.broadcasted_iota(jnp.int32, (ATTN_WIDTH, KV_WIDTH), 1)
    to_kv = jnp.where((fj % HEAD_DIM == fc % HEAD_DIM) & ((fj // HEAD_DIM) // GQA_GROUP == fc // HEAD_DIM),
                      1.0, 0.0).astype(BF16)
    tc = lax.broadcasted_iota(jnp.int32, (KV_WIDTH, ATTN_WIDTH), 0)
    tj = lax.broadcasted_iota(jnp.int32, (KV_WIDTH, ATTN_WIDTH), 1)
    from_kv = jnp.where((tj % HEAD_DIM == tc % HEAD_DIM) & ((tj // HEAD_DIM) // GQA_GROUP == tc // HEAD_DIM),
                        1.0, 0.0).astype(BF16)

    m1 = jnp.where(own, _dot(rep, q.astype(BF16)), 0.0).astype(BF16)
    qm3 = _dot(m1, to_kv).astype(BF16).reshape(n, N_HEADS, KV_WIDTH)
    s = jnp.einsum("bhc,bkc->bhk", qm3, k3, preferred_element_type=F32)
    h_col = lax.broadcasted_iota(jnp.int32, (N_HEADS, 1), 0).astype(F32)
    slope = jnp.exp2(-8.0 * (h_col + 1.0) / N_HEADS)[None]
    dist = (n_keys - 1 - lax.broadcasted_iota(jnp.int32, (1, n_keys), 1)).astype(F32)[None]
    s = s - slope * dist
    sink = sink_col[None]
    m = jnp.maximum(jnp.max(s, axis=-1, keepdims=True), sink)
    p = jnp.exp(s - m)
    den = jnp.sum(p, axis=-1, keepdims=True) + jnp.exp(sink - m)
    o3 = jnp.einsum("bhk,bkc->bhc", p.astype(BF16), v3, preferred_element_type=F32) / den
    o2 = o3.reshape(rows, KV_WIDTH)
    hi = o2.astype(BF16)
    lo = (o2 - hi.astype(F32)).astype(BF16)
    spread = jnp.where(own, _dot(hi, from_kv) + _dot(lo, from_kv), 0.0)
    return jnp.sum(spread.reshape(n, N_HEADS, ATTN_WIDTH), axis=1)


def _attn_kernel(sink_ref, x_ref, xs_ref, k3_ref, v3_ref, sinkc_ref, g_ref, win_ref, qg_ref,
                 kt_ref, vb_ref, wout_ref,
                 out_ref, outs_ref,
                 win16_ref, wout16_ref, h_ref, qlo_ref, qhi_ref, o_ref, *, tm, tiles_per_seq, n_tiles):
    s = pl.program_id(0)
    nblk = tm // BLOCK
    pair_lo = lax.broadcasted_iota(jnp.int32, (1, PAIR), 1) < HEAD_DIM
    wc = ATTN_W_CHUNK
    n_w = D_MODEL // wc

    @pl.when(s < n_w)
    def _():
        r0 = pl.multiple_of(s * wc, wc)
        win16_ref[pl.ds(r0, wc), :] = win_ref[...].astype(BF16)
        wout16_ref[pl.ds(r0, wc), :] = wout_ref[...].astype(BF16)

    @pl.when((s >= n_w) & (s < n_w + n_tiles))
    def _():
        i = (s - n_w) % tiles_per_seq
        h_ref[...] = _rms(x_ref[...], g_ref[...]).astype(BF16)
        for kh in range(N_KV_HEADS):
            gcols = slice(kh * GQA_GROUP * HEAD_DIM, (kh + 1) * GQA_GROUP * HEAD_DIM)
            q = _dot(h_ref[...], win16_ref[:, gcols])
            for pr in range(GQA_GROUP // 2):
                cols = slice(pr * PAIR, (pr + 1) * PAIR)
                ocols = slice((kh * (GQA_GROUP // 2) + pr) * PAIR, (kh * (GQA_GROUP // 2) + pr + 1) * PAIR)
                qp = q[:, cols]
                qn = qp * _pair_inv_rms(qp, pair_lo, HEAD_DIM ** -0.5 * LOG2E) * qg_ref[:, ocols]
                qlo_ref[:, ocols] = jnp.where(pair_lo, qn, 0.0).astype(BF16)
                qhi_ref[:, ocols] = jnp.where(pair_lo, 0.0, qn).astype(BF16)

        r_idx = lax.broadcasted_iota(jnp.int32, (BLOCK, 2 * BLOCK), 0)
        c_idx = lax.broadcasted_iota(jnp.int32, (BLOCK, 2 * BLOCK), 1)
        dist = r_idx + BLOCK - c_idx
        in_window = (dist >= 0) & (dist < WINDOW)
        neg_dist = -dist.astype(F32)

        for blk in range(nblk):
            rows = slice(blk * BLOCK, (blk + 1) * BLOCK)
            first_key = (i * nblk + blk - 1) * BLOCK
            valid = in_window & (first_key + c_idx >= 0)
            nd = jnp.where(valid, neg_dist, NEG_INF / MIN_SLOPE)
            for kh in range(N_KV_HEADS):
                kt = kt_ref[0, blk, kh]
                vdup = vb_ref[0, blk, :, kh * PAIR:(kh + 1) * PAIR]
                v_lo = jnp.where(pair_lo, vdup, jnp.zeros_like(vdup))
                v_hi = jnp.where(pair_lo, jnp.zeros_like(vdup), vdup)
                pieces = []
                for pr in range(GQA_GROUP // 2):
                    cols = slice((kh * (GQA_GROUP // 2) + pr) * PAIR,
                                 (kh * (GQA_GROUP // 2) + pr + 1) * PAIR)
                    pieces.append(qlo_ref[rows, cols])
                    pieces.append(qhi_ref[rows, cols])
                s_all = _dot(jnp.concatenate(pieces, axis=0), kt)
                for pr in range(GQA_GROUP // 2):
                    cols = slice((kh * (GQA_GROUP // 2) + pr) * PAIR,
                                 (kh * (GQA_GROUP // 2) + pr + 1) * PAIR)
                    probs, inv_den = [], []
                    for half in range(2):
                        g_idx = 2 * pr + half
                        head = kh * GQA_GROUP + g_idx
                        slope2 = 2.0 ** (-8.0 * (head + 1) / N_HEADS) * LOG2E
                        sc = s_all[g_idx * BLOCK:(g_idx + 1) * BLOCK] + slope2 * nd
                        sink2 = sink_ref[head] * LOG2E
                        m = jnp.maximum(jnp.max(sc, axis=-1, keepdims=True), sink2)
                        p = jnp.exp2(sc - m)
                        den = jnp.sum(p, axis=-1, keepdims=True) + jnp.exp2(sink2 - m)
                        probs.append(p.astype(BF16))
                        inv_den.append(1.0 / den)
                    o_pair = _dot(probs[0], v_lo) + _dot(probs[1], v_hi)
                    o_ref[rows, cols] = o_pair * jnp.where(pair_lo, inv_den[0], inv_den[1])

        z = _dot(h_ref[...], win16_ref[:, ATTN_WIDTH:])
        y = o_ref[...] * _silu(z)
        out_ref[...] = x_ref[...] + _dot(y.astype(BF16), wout16_ref[...])

    @pl.when(s == n_w + n_tiles)
    def _():
        x = xs_ref[...]
        h = _rms(x, g_ref[...]).astype(BF16)
        q = _dot(h, win16_ref[:, :ATTN_WIDTH])
        z = _dot(h, win16_ref[:, ATTN_WIDTH:])
        qn = jnp.concatenate(
            [_normed_q_pair(q, pr, qg_ref, pair_lo, HEAD_DIM ** -0.5) for pr in range(N_PAIRS)], axis=-1)
        n_dec = x.shape[0]
        o_rows = []
        for c in range(n_dec // DEC_CHUNK):
            rows = slice(c * DEC_CHUNK, (c + 1) * DEC_CHUNK)
            o_rows.append(_decode_attention(qn[rows], k3_ref[rows], v3_ref[rows], sinkc_ref[...]))
        y = jnp.concatenate(o_rows, axis=0) * _silu(z)
        outs_ref[...] = x + _dot(y.astype(BF16), wout16_ref[...])


def _attn_layer(x2d, xs, k3, v3, sink, g, w_in, q_gain, kt_band, v_band, w_out, *, seq, tm):
    rows = x2d.shape[0]
    n_dec = xs.shape[0]
    n_tiles = rows // tm
    tiles_per_seq = seq // tm
    nblk = tm // BLOCK
    wc = ATTN_W_CHUNK
    n_w = D_MODEL // wc
    tile = lambda s: (_tile_of(s, n_w, n_tiles), 0)
    band = lambda s: (_tile_of(s, n_w, n_tiles) // tiles_per_seq,
                      _tile_of(s, n_w, n_tiles) % tiles_per_seq)
    w_row = lambda s: (_w_step(s, n_w), 0)
    return pl.pallas_call(
        functools.partial(_attn_kernel, tm=tm, tiles_per_seq=tiles_per_seq, n_tiles=n_tiles),
        grid=(n_w + n_tiles + 1,),
        in_specs=[
            pl.BlockSpec(memory_space=pltpu.SMEM),
            pl.BlockSpec((tm, D_MODEL), tile),
            _resident((n_dec, D_MODEL)),
            _resident(k3.shape),
            _resident(v3.shape),
            _resident((N_HEADS, 1)),
            _resident((1, D_MODEL)),
            pl.BlockSpec((wc, 2 * ATTN_WIDTH), w_row),
            _resident((1, ATTN_WIDTH)),
            pl.BlockSpec((1, nblk, N_KV_HEADS, PAIR, 2 * BLOCK), lambda s: band(s) + (0, 0, 0)),
            pl.BlockSpec((1, nblk, 2 * BLOCK, N_KV_HEADS * PAIR), lambda s: band(s) + (0, 0)),
            pl.BlockSpec((wc, D_MODEL), w_row),
        ],
        out_specs=[
            pl.BlockSpec((tm, D_MODEL), tile),
            _resident((n_dec, D_MODEL)),
        ],
        out_shape=[
            jax.ShapeDtypeStruct((rows, D_MODEL), F32),
            jax.ShapeDtypeStruct((n_dec, D_MODEL), F32),
        ],
        scratch_shapes=[
            pltpu.VMEM((D_MODEL, 2 * ATTN_WIDTH), BF16),
            pltpu.VMEM((ATTN_WIDTH, D_MODEL), BF16),
            pltpu.VMEM((tm, D_MODEL), BF16),
            pltpu.VMEM((tm, ATTN_WIDTH), BF16),
            pltpu.VMEM((tm, ATTN_WIDTH), BF16),
            pltpu.VMEM((tm, ATTN_WIDTH), F32),
        ],
        compiler_params=pltpu.CompilerParams(
            dimension_semantics=("arbitrary",), vmem_limit_bytes=VMEM_LIMIT),
        name="attn_layer",
    )(sink, x2d, xs, k3, v3, sink[:, None], g, w_in, q_gain, kt_band, v_band, w_out)


def kernel(x_prompt, x_sample, state_pool, cache_k_win, cache_v_win, norm_a, w_in_a, w_grp_a, scale_a, w_out_a, norm_kv, w_kv, k_norm, norm_b, w_in_b, q_norm, sinks, w_out_b):
    bsz, seq, _ = x_prompt.shape
    n_dec = x_sample.shape[0]
    assert x_sample.shape[1] == 1 and cache_k_win.shape[1] == WINDOW
    tm_a, tm_b, tm_kv = 256, 256, 512

    hp = x_prompt.reshape(bsz * seq, D_MODEL)
    hs = x_sample[:, 0]
    state_t = state_pool.transpose(0, 2, 1, 3)
    pool_p, pool_s = [], []
    for l in range(N_A_LAYERS):
        hp, hist, hs, u_new = _pool_layer(
            hp, hs, state_t[l], norm_a[l][None], w_in_a[l], w_grp_a[l], scale_a[l][None], w_out_a[l],
            seq=seq, tm=tm_a)
        pool_p.append(hist[:, POOL_HIST - POOL_BUF:])
        pool_s.append(jnp.concatenate([state_pool[l][:, 1:], u_new[:, None]], axis=1))

    kt_band, v_band, k_win_p, v_win_p, kn, vn = _shared_kv(
        hp, hs, norm_kv[None], w_kv, jnp.tile(k_norm, 2)[None], seq=seq, tm=tm_kv)

    kv_shape = (n_dec, 1, N_KV_HEADS, HEAD_DIM)
    k_win_s = jnp.concatenate([cache_k_win[:, 1:], kn.reshape(kv_shape)], axis=1)
    v_win_s = jnp.concatenate([cache_v_win[:, 1:], vn.reshape(kv_shape)], axis=1)
    k3 = k_win_s.reshape(n_dec, WINDOW, KV_WIDTH).astype(BF16)
    v3 = v_win_s.reshape(n_dec, WINDOW, KV_WIDTH).astype(BF16)

    for j in range(N_B_LAYERS):
        hp, hs = _attn_layer(hp, hs, k3, v3, sinks[j], norm_b[j][None], w_in_b[j],
                             jnp.tile(q_norm[j], N_HEADS)[None], kt_band, v_band, w_out_b[j],
                             seq=seq, tm=tm_b)

    win_shape = (bsz, WINDOW, N_KV_HEADS, HEAD_DIM)
    return (hp.reshape(bsz, seq, D_MODEL), hs[:, None],
            jnp.stack(pool_p, axis=0), jnp.stack(pool_s, axis=0),
            k_win_p.reshape(win_shape), v_win_p.reshape(win_shape), k_win_s, v_win_s)
```

```python
import functools
import math

import jax
import jax.numpy as jnp
from jax import lax
from jax.experimental import pallas as pl
from jax.experimental.pallas import tpu as pltpu

D_MODEL = 2048
N_A_LAYERS = 2
N_B_LAYERS = 2
POOL_WINDOWS = (2, 4, 8, 16)
N_POOL_GROUPS = len(POOL_WINDOWS)
POOL_WIDTH = D_MODEL
POOL_GROUP = POOL_WIDTH // N_POOL_GROUPS
POOL_BUF = max(POOL_WINDOWS) - 1
POOL_HIST = POOL_BUF + 1
HEAD_DIM = 64
N_HEADS = D_MODEL // HEAD_DIM
N_KV_HEADS = N_HEADS // 8
GQA_GROUP = N_HEADS // N_KV_HEADS
ATTN_WIDTH = N_HEADS * HEAD_DIM
KV_WIDTH = N_KV_HEADS * HEAD_DIM
WINDOW = 128
BLOCK = WINDOW
PAST_LEN = 16384
RMS_EPS = 1e-6
NEG_INF = -1e30
LOG2E = math.log2(math.e)
MIN_SLOPE = 2.0 ** -8
PAIR = 2 * HEAD_DIM
N_PAIRS = N_HEADS // 2
POOL_W_CHUNK = 128
ATTN_W_CHUNK = 64
DEC_CHUNK = 8
VMEM_LIMIT = 60 * 1024 * 1024

BF16 = jnp.bfloat16
F32 = jnp.float32


def _rms(x, g):
    return x * lax.rsqrt(jnp.mean(x * x, axis=-1, keepdims=True) + RMS_EPS) * g


def _silu(z):
    return z * jax.nn.sigmoid(z)


def _dot(a, b):
    return jnp.dot(a, b, preferred_element_type=F32)


def _resident(shape):
    return pl.BlockSpec(shape, lambda *_: (0,) * len(shape), pipeline_mode=pl.Buffered(1))


def _w_step(s, n_w):
    return jnp.minimum(s, n_w - 1)


def _tile_of(s, n_w, n_tiles):
    return jnp.clip(s - n_w, 0, n_tiles - 1)


def _pair_inv_rms(pair, pair_lo, scale):
    sq = pair * pair
    lo = jnp.sum(jnp.where(pair_lo, sq, 0.0), axis=-1, keepdims=True)
    hi = jnp.sum(jnp.where(pair_lo, 0.0, sq), axis=-1, keepdims=True)
    ms = jnp.where(pair_lo, lo, hi) * (1.0 / HEAD_DIM)
    return lax.rsqrt(ms + RMS_EPS) * scale


def _pool_tail(x_ref, u, z, window_sums, inv_cnt, wgrp_ref, scale_ref, wout_ref):
    mixed = []
    for g in range(N_POOL_GROUPS):
        cols = slice(g * POOL_GROUP, (g + 1) * POOL_GROUP)
        p = window_sums[g] * inv_cnt[g] - u[:, cols]
        mixed.append(_dot(p.astype(BF16), wgrp_ref[g]))
    y = jnp.concatenate(mixed, axis=-1) * scale_ref[...] * _silu(z)
    return x_ref[...] + _dot(y.astype(BF16), wout_ref[...])


def _pool_kernel(x_ref, xs_ref, st_ref, g_ref, win_ref, wgrp_ref, scale_ref, wout_ref,
                 out_ref, hist_ref, outs_ref, unew_ref,
                 win16_ref, wgrp16_ref, wout16_ref, uext_ref, *, tm, tiles_per_seq, n_tiles):
    s = pl.program_id(0)
    wc = POOL_W_CHUNK
    n_w = D_MODEL // wc

    @pl.when(s < n_w)
    def _():
        r0 = pl.multiple_of(s * wc, wc)
        win16_ref[pl.ds(r0, wc), :] = win_ref[0].astype(BF16)
        wout16_ref[pl.ds(r0, wc), :] = wout_ref[0].astype(BF16)
        chunks_per_group = POOL_GROUP // wc
        g0 = pl.multiple_of((s % chunks_per_group) * wc, wc)
        wgrp16_ref[s // chunks_per_group, pl.ds(g0, wc), :] = wgrp_ref[0, 0].astype(BF16)

    @pl.when((s >= n_w) & (s < n_w + n_tiles))
    def _():
        i = (s - n_w) % tiles_per_seq

        @pl.when(i == 0)
        def _():
            uext_ref[0:POOL_HIST, :] = jnp.zeros((POOL_HIST, POOL_WIDTH), F32)

        h = _rms(x_ref[...], g_ref[...]).astype(BF16)
        u = _dot(h, win16_ref[:, :POOL_WIDTH])
        z = _dot(h, win16_ref[:, POOL_WIDTH:])
        uext_ref[POOL_HIST:POOL_HIST + tm, :] = u

        pos = i * tm + lax.broadcasted_iota(jnp.int32, (tm, 1), 0)
        sums, inv_cnt = [], []
        for g, w in enumerate(POOL_WINDOWS):
            cols = slice(g * POOL_GROUP, (g + 1) * POOL_GROUP)
            acc = u[:, cols]
            for k in range(1, w):
                acc = acc + uext_ref[POOL_HIST - k:POOL_HIST - k + tm, cols]
            sums.append(acc)
            inv_cnt.append(1.0 / jnp.minimum(pos + 1, w).astype(F32))
        out_ref[...] = _pool_tail(x_ref, u, z, sums, inv_cnt, wgrp16_ref, scale_ref, wout16_ref)

        tail = uext_ref[tm:tm + POOL_HIST, :]
        hist_ref[0] = tail
        uext_ref[0:POOL_HIST, :] = tail

    @pl.when(s == n_w + n_tiles)
    def _():
        h = _rms(xs_ref[...], g_ref[...]).astype(BF16)
        u = _dot(h, win16_ref[:, :POOL_WIDTH])
        z = _dot(h, win16_ref[:, POOL_WIDTH:])
        unew_ref[...] = u
        hist_row = lax.broadcasted_iota(jnp.int32, (1, POOL_BUF, 1), 1)
        sums, inv_cnt = [], []
        for g, w in enumerate(POOL_WINDOWS):
            cols = slice(g * POOL_GROUP, (g + 1) * POOL_GROUP)
            hist = jnp.where(hist_row >= POOL_BUF - (w - 1), st_ref[0, :, :, cols], 0.0)
            sums.append(u[:, cols] + jnp.sum(hist, axis=1))
            inv_cnt.append(1.0 / float(min(PAST_LEN + 1, w)))
        outs_ref[...] = _pool_tail(xs_ref, u, z, sums, inv_cnt, wgrp16_ref, scale_ref, wout16_ref)


def _pool_layer(x2d, xs, state, g, w_in, w_grp, scale, w_out, *, layer, seq, tm):
    rows = x2d.shape[0]
    n_dec = xs.shape[0]
    n_tiles = rows // tm
    tiles_per_seq = seq // tm
    wc = POOL_W_CHUNK
    n_w = D_MODEL // wc
    chunks_per_group = POOL_GROUP // wc
    tile = lambda s: (_tile_of(s, n_w, n_tiles), 0)
    w_row = lambda s: (layer, _w_step(s, n_w), 0)
    one = pl.Buffered(1)
    return pl.pallas_call(
        functools.partial(_pool_kernel, tm=tm, tiles_per_seq=tiles_per_seq, n_tiles=n_tiles),
        grid=(n_w + n_tiles + 1,),
        in_specs=[
            pl.BlockSpec((tm, D_MODEL), tile),
            _resident((n_dec, D_MODEL)),
            pl.BlockSpec((1, n_dec, POOL_BUF, POOL_WIDTH), lambda s: (layer, 0, 0, 0), pipeline_mode=one),
            pl.BlockSpec((None, 1, D_MODEL), lambda s: (layer, 0, 0), pipeline_mode=one),
            pl.BlockSpec((1, wc, 2 * POOL_WIDTH), w_row),
            pl.BlockSpec((1, 1, wc, POOL_GROUP),
                         lambda s: (layer, _w_step(s, n_w) // chunks_per_group,
                                    _w_step(s, n_w) % chunks_per_group, 0)),
            pl.BlockSpec((None, 1, POOL_WIDTH), lambda s: (layer, 0, 0), pipeline_mode=one),
            pl.BlockSpec((1, wc, D_MODEL), w_row),
        ],
        out_specs=[
            pl.BlockSpec((tm, D_MODEL), tile),
            pl.BlockSpec((1, POOL_HIST, POOL_WIDTH),
                         lambda s: (_tile_of(s, n_w, n_tiles) // tiles_per_seq, 0, 0)),
            _resident((n_dec, D_MODEL)),
            _resident((n_dec, POOL_WIDTH)),
        ],
        out_shape=[
            jax.ShapeDtypeStruct((rows, D_MODEL), F32),
            jax.ShapeDtypeStruct((rows // seq, POOL_HIST, POOL_WIDTH), F32),
            jax.ShapeDtypeStruct((n_dec, D_MODEL), F32),
            jax.ShapeDtypeStruct((n_dec, POOL_WIDTH), F32),
        ],
        scratch_shapes=[
            pltpu.VMEM((D_MODEL, 2 * POOL_WIDTH), BF16),
            pltpu.VMEM((N_POOL_GROUPS, POOL_GROUP, POOL_GROUP), BF16),
            pltpu.VMEM((POOL_WIDTH, D_MODEL), BF16),
            pltpu.VMEM((POOL_HIST + tm, POOL_WIDTH), F32),
        ],
        compiler_params=pltpu.CompilerParams(
            dimension_semantics=("arbitrary",), vmem_limit_bytes=VMEM_LIMIT),
        name="pool_layer",
    )(x2d, xs, state, g, w_in, w_grp, scale, w_out)


def _kv_project(x, g_ref, w16_ref, kg_ref):
    h = _rms(x, g_ref[...]).astype(BF16)
    kv = _dot(h, w16_ref[...])
    pair_lo = lax.broadcasted_iota(jnp.int32, (1, PAIR), 1) < HEAD_DIM
    k_pairs = []
    for p in range(KV_WIDTH // PAIR):
        kp = kv[:, p * PAIR:(p + 1) * PAIR]
        k_pairs.append(kp * _pair_inv_rms(kp, pair_lo, 1.0) * kg_ref[...])
    return k_pairs, kv[:, KV_WIDTH:]


def _dup_heads(col, pair_lo):
    rolled = pltpu.roll(col, HEAD_DIM, 1)
    return jnp.where(pair_lo, col, rolled), jnp.where(pair_lo, rolled, col)


def _kv_kernel(x_ref, xs_ref, g_ref, wkv_ref, kg_ref,
               kt_ref, vb_ref, kwin_ref, vwin_ref, kn_ref, vn_ref,
               w16_ref, pk_ref, pv_ref, *, tm, tiles_per_seq, n_tiles):
    s = pl.program_id(0)
    nblk = tm // BLOCK
    pair_lo = lax.broadcasted_iota(jnp.int32, (1, PAIR), 1) < HEAD_DIM

    @pl.when(s == 0)
    def _():
        w16_ref[...] = wkv_ref[...].astype(BF16)

    @pl.when(s < n_tiles)
    def _():
        @pl.when(s % tiles_per_seq == 0)
        def _():
            pk_ref[...] = jnp.zeros(pk_ref.shape, BF16)
            pv_ref[...] = jnp.zeros(pv_ref.shape, BF16)

        k_pairs, v = _kv_project(x_ref[...], g_ref, w16_ref, kg_ref)
        for p in range(KV_WIDTH // PAIR):
            kwin_ref[0, :, p * PAIR:(p + 1) * PAIR] = k_pairs[p][tm - BLOCK:]
        vwin_ref[0] = v[tm - BLOCK:]
        for jb in range(nblk):
            rows = slice(jb * BLOCK, (jb + 1) * BLOCK)
            for p in range(KV_WIDTH // PAIR):
                k_dup = _dup_heads(k_pairs[p][rows], pair_lo)
                v_dup = _dup_heads(v[rows, p * PAIR:(p + 1) * PAIR], pair_lo)
                for half in range(2):
                    kh = 2 * p + half
                    cols = slice(kh * PAIR, (kh + 1) * PAIR)
                    kt_cur = k_dup[half].T.astype(BF16)
                    v_cur = v_dup[half].astype(BF16)
                    kt_ref[0, jb, kh, :, 0:BLOCK] = pk_ref[kh]
                    kt_ref[0, jb, kh, :, BLOCK:2 * BLOCK] = kt_cur
                    vb_ref[0, jb, 0:BLOCK, cols] = pv_ref[:, cols]
                    vb_ref[0, jb, BLOCK:2 * BLOCK, cols] = v_cur
                    pk_ref[kh] = kt_cur
                    pv_ref[:, cols] = v_cur

    @pl.when(s == n_tiles)
    def _():
        k_pairs, v = _kv_project(xs_ref[...], g_ref, w16_ref, kg_ref)
        for p in range(KV_WIDTH // PAIR):
            kn_ref[:, p * PAIR:(p + 1) * PAIR] = k_pairs[p]
        vn_ref[...] = v


def _shared_kv(x2d, xs, g, w_kv, k_gain_pair, *, seq, tm):
    rows = x2d.shape[0]
    bsz = rows // seq
    n_dec = xs.shape[0]
    n_tiles = rows // tm
    tiles_per_seq = seq // tm
    nblk = tm // BLOCK
    t = lambda s: jnp.minimum(s, n_tiles - 1)
    return pl.pallas_call(
        functools.partial(_kv_kernel, tm=tm, tiles_per_seq=tiles_per_seq, n_tiles=n_tiles),
        grid=(n_tiles + 1,),
        in_specs=[
            pl.BlockSpec((tm, D_MODEL), lambda s: (t(s), 0)),
            _resident((n_dec, D_MODEL)),
            _resident((1, D_MODEL)),
            _resident((D_MODEL, 2 * KV_WIDTH)),
            _resident((1, PAIR)),
        ],
        out_specs=[
            pl.BlockSpec((1, nblk, N_KV_HEADS, PAIR, 2 * BLOCK),
                         lambda s: (t(s) // tiles_per_seq, t(s) % tiles_per_seq, 0, 0, 0)),
            pl.BlockSpec((1, nblk, 2 * BLOCK, N_KV_HEADS * PAIR),
                         lambda s: (t(s) // tiles_per_seq, t(s) % tiles_per_seq, 0, 0)),
            pl.BlockSpec((1, BLOCK, KV_WIDTH), lambda s: (t(s) // tiles_per_seq, 0, 0)),
            pl.BlockSpec((1, BLOCK, KV_WIDTH), lambda s: (t(s) // tiles_per_seq, 0, 0)),
            _resident((n_dec, KV_WIDTH)),
            _resident((n_dec, KV_WIDTH)),
        ],
        out_shape=[
            jax.ShapeDtypeStruct((bsz, seq // BLOCK, N_KV_HEADS, PAIR, 2 * BLOCK), BF16),
            jax.ShapeDtypeStruct((bsz, seq // BLOCK, 2 * BLOCK, N_KV_HEADS * PAIR), BF16),
            jax.ShapeDtypeStruct((bsz, BLOCK, KV_WIDTH), F32),
            jax.ShapeDtypeStruct((bsz, BLOCK, KV_WIDTH), F32),
            jax.ShapeDtypeStruct((n_dec, KV_WIDTH), F32),
            jax.ShapeDtypeStruct((n_dec, KV_WIDTH), F32),
        ],
        scratch_shapes=[
            pltpu.VMEM((D_MODEL, 2 * KV_WIDTH), BF16),
            pltpu.VMEM((N_KV_HEADS, PAIR, BLOCK), BF16),
            pltpu.VMEM((BLOCK, N_KV_HEADS * PAIR), BF16),
        ],
        compiler_params=pltpu.CompilerParams(
            dimension_semantics=("arbitrary",), vmem_limit_bytes=VMEM_LIMIT),
        name="shared_kv",
    )(x2d, xs, g, w_kv, k_gain_pair)


def _normed_q_pair(q, pr, qg_ref, pair_lo, scale):
    cols = slice(pr * PAIR, (pr + 1) * PAIR)
    qp = q[:, cols]
    return qp * _pair_inv_rms(qp, pair_lo, scale) * qg_ref[:, cols]


def _decode_attention(q, k3, v3, sink_col):
    n, n_keys = k3.shape[0], k3.shape[1]
    rows = n * N_HEADS
    ri = lax.broadcasted_iota(jnp.int32, (rows, n), 0)
    ci = lax.broadcasted_iota(jnp.int32, (rows, n), 1)
    rep = jnp.where(ri // N_HEADS == ci, 1.0, 0.0).astype(BF16)
    row_h = lax.broadcasted_iota(jnp.int32, (rows, ATTN_WIDTH), 0) % N_HEADS
    col_h = lax.broadcasted_iota(jnp.int32, (rows, ATTN_WIDTH), 1) // HEAD_DIM
    own = row_h == col_h
    fj = lax.broadcasted_iota(jnp.int32, (ATTN_WIDTH, KV_WIDTH), 0)
    fc = lax.broadcasted_iota(jnp.int32, (ATTN_WIDTH, KV_WIDTH), 1)
    to_kv = jnp.where((fj % HEAD_DIM == fc % HEAD_DIM) & ((fj // HEAD_DIM) // GQA_GROUP == fc // HEAD_DIM),
                      1.0, 0.0).astype(BF16)
    tc = lax.broadcasted_iota(jnp.int32, (KV_WIDTH, ATTN_WIDTH), 0)
    tj = lax.broadcasted_iota(jnp.int32, (KV_WIDTH, ATTN_WIDTH), 1)
    from_kv = jnp.where((tj % HEAD_DIM == tc % HEAD_DIM) & ((tj // HEAD_DIM) // GQA_GROUP == tc // HEAD_DIM),
                        1.0, 0.0).astype(BF16)

    m1 = jnp.where(own, _dot(rep, q.astype(BF16)), 0.0).astype(BF16)
    qm3 = _dot(m1, to_kv).astype(BF16).reshape(n, N_HEADS, KV_WIDTH)
    s = jnp.einsum("bhc,bkc->bhk", qm3, k3, preferred_element_type=F32)
    h_col = lax.broadcasted_iota(jnp.int32, (N_HEADS, 1), 0).astype(F32)
    slope = jnp.exp2(-8.0 * (h_col + 1.0) / N_HEADS)[None]
    dist = (n_keys - 1 - lax.broadcasted_iota(jnp.int32, (1, n_keys), 1)).astype(F32)[None]
    s = s - slope * dist
    sink = sink_col[None]
    m = jnp.maximum(jnp.max(s, axis=-1, keepdims=True), sink)
    p = jnp.exp(s - m)
    den = jnp.sum(p, axis=-1, keepdims=True) + jnp.exp(sink - m)
    o3 = jnp.einsum("bhk,bkc->bhc", p.astype(BF16), v3, preferred_element_type=F32) / den
    o2 = o3.reshape(rows, KV_WIDTH)
    hi = o2.astype(BF16)
    lo = (o2 - hi.astype(F32)).astype(BF16)
    spread = jnp.where(own, _dot(hi, from_kv) + _dot(lo, from_kv), 0.0)
    return jnp.sum(spread.reshape(n, N_HEADS, ATTN_WIDTH), axis=1)


def _attn_kernel(sink_ref, x_ref, xs_ref, k3_ref, v3_ref, sinkc_ref, g_ref, win_ref, qg_ref,
                 kt_ref, vb_ref, wout_ref,
                 out_ref, outs_ref,
                 win16_ref, wout16_ref, h_ref, qlo_ref, qhi_ref, o_ref,
                 *, layer, tm, tiles_per_seq, n_tiles):
    s = pl.program_id(0)
    nblk = tm // BLOCK
    pair_lo = lax.broadcasted_iota(jnp.int32, (1, PAIR), 1) < HEAD_DIM
    wc = ATTN_W_CHUNK
    n_w = D_MODEL // wc

    @pl.when(s < n_w)
    def _():
        r0 = pl.multiple_of(s * wc, wc)
        win16_ref[pl.ds(r0, wc), :] = win_ref[0].astype(BF16)
        wout16_ref[pl.ds(r0, wc), :] = wout_ref[0].astype(BF16)

    @pl.when((s >= n_w) & (s < n_w + n_tiles))
    def _():
        i = (s - n_w) % tiles_per_seq
        h_ref[...] = _rms(x_ref[...], g_ref[...]).astype(BF16)
        for kh in range(N_KV_HEADS):
            gcols = slice(kh * GQA_GROUP * HEAD_DIM, (kh + 1) * GQA_GROUP * HEAD_DIM)
            q = _dot(h_ref[...], win16_ref[:, gcols])
            for pr in range(GQA_GROUP // 2):
                cols = slice(pr * PAIR, (pr + 1) * PAIR)
                ocols = slice((kh * (GQA_GROUP // 2) + pr) * PAIR, (kh * (GQA_GROUP // 2) + pr + 1) * PAIR)
                qp = q[:, cols]
                qn = qp * _pair_inv_rms(qp, pair_lo, HEAD_DIM ** -0.5 * LOG2E) * qg_ref[:, ocols]
                qlo_ref[:, ocols] = jnp.where(pair_lo, qn, 0.0).astype(BF16)
                qhi_ref[:, ocols] = jnp.where(pair_lo, 0.0, qn).astype(BF16)

        r_idx = lax.broadcasted_iota(jnp.int32, (BLOCK, 2 * BLOCK), 0)
        c_idx = lax.broadcasted_iota(jnp.int32, (BLOCK, 2 * BLOCK), 1)
        dist = r_idx + BLOCK - c_idx
        in_window = (dist >= 0) & (dist < WINDOW)
        neg_dist = -dist.astype(F32)

        for blk in range(nblk):
            rows = slice(blk * BLOCK, (blk + 1) * BLOCK)
            first_key = (i * nblk + blk - 1) * BLOCK
            valid = in_window & (first_key + c_idx >= 0)
            nd = jnp.where(valid, neg_dist, NEG_INF / MIN_SLOPE)
            for kh in range(N_KV_HEADS):
                kt = kt_ref[0, blk, kh]
                vdup = vb_ref[0, blk, :, kh * PAIR:(kh + 1) * PAIR]
                v_lo = jnp.where(pair_lo, vdup, jnp.zeros_like(vdup))
                v_hi = jnp.where(pair_lo, jnp.zeros_like(vdup), vdup)
                pieces = []
                for pr in range(GQA_GROUP // 2):
                    cols = slice((kh * (GQA_GROUP // 2) + pr) * PAIR,
                                 (kh * (GQA_GROUP // 2) + pr + 1) * PAIR)
                    pieces.append(qlo_ref[rows, cols])
                    pieces.append(qhi_ref[rows, cols])
                s_all = _dot(jnp.concatenate(pieces, axis=0), kt)
                for pr in range(GQA_GROUP // 2):
                    cols = slice((kh * (GQA_GROUP // 2) + pr) * PAIR,
                                 (kh * (GQA_GROUP // 2) + pr + 1) * PAIR)
                    probs, inv_den = [], []
                    for half in range(2):
                        g_idx = 2 * pr + half
                        head = kh * GQA_GROUP + g_idx
                        slope2 = 2.0 ** (-8.0 * (head + 1) / N_HEADS) * LOG2E
                        sc = s_all[g_idx * BLOCK:(g_idx + 1) * BLOCK] + slope2 * nd
                        sink2 = sink_ref[layer, head] * LOG2E
                        m = jnp.maximum(jnp.max(sc, axis=-1, keepdims=True), sink2)
                        p = jnp.exp2(sc - m)
                        den = jnp.sum(p, axis=-1, keepdims=True) + jnp.exp2(sink2 - m)
                        probs.append(p.astype(BF16))
                        inv_den.append(1.0 / den)
                    o_pair = _dot(probs[0], v_lo) + _dot(probs[1], v_hi)
                    o_ref[rows, cols] = o_pair * jnp.where(pair_lo, inv_den[0], inv_den[1])

        z = _dot(h_ref[...], win16_ref[:, ATTN_WIDTH:])
        y = o_ref[...] * _silu(z)
        out_ref[...] = x_ref[...] + _dot(y.astype(BF16), wout16_ref[...])

    @pl.when(s == n_w + n_tiles)
    def _():
        x = xs_ref[...]
        h = _rms(x, g_ref[...]).astype(BF16)
        q = _dot(h, win16_ref[:, :ATTN_WIDTH])
        z = _dot(h, win16_ref[:, ATTN_WIDTH:])
        qn = jnp.concatenate(
            [_normed_q_pair(q, pr, qg_ref, pair_lo, HEAD_DIM ** -0.5) for pr in range(N_PAIRS)], axis=-1)
        n_dec = x.shape[0]
        o_rows = []
        for c in range(n_dec // DEC_CHUNK):
            rows = slice(c * DEC_CHUNK, (c + 1) * DEC_CHUNK)
            o_rows.append(_decode_attention(qn[rows], k3_ref[rows], v3_ref[rows],
                                            sinkc_ref[:, layer:layer + 1]))
        y = jnp.concatenate(o_rows, axis=0) * _silu(z)
        outs_ref[...] = x + _dot(y.astype(BF16), wout16_ref[...])


def _attn_layer(x2d, xs, k3, v3, sinks, sinks_t, g, w_in, q_gain, kt_band, v_band, w_out,
                *, layer, seq, tm):
    rows = x2d.shape[0]
    n_dec = xs.shape[0]
    n_tiles = rows // tm
    tiles_per_seq = seq // tm
    nblk = tm // BLOCK
    wc = ATTN_W_CHUNK
    n_w = D_MODEL // wc
    tile = lambda s: (_tile_of(s, n_w, n_tiles), 0)
    band = lambda s: (_tile_of(s, n_w, n_tiles) // tiles_per_seq,
                      _tile_of(s, n_w, n_tiles) % tiles_per_seq)
    w_row = lambda s: (layer, _w_step(s, n_w), 0)
    one = pl.Buffered(1)
    return pl.pallas_call(
        functools.partial(_attn_kernel, layer=layer, tm=tm, tiles_per_seq=tiles_per_seq, n_tiles=n_tiles),
        grid=(n_w + n_tiles + 1,),
        in_specs=[
            pl.BlockSpec(memory_space=pltpu.SMEM),
            pl.BlockSpec((tm, D_MODEL), tile),
            _resident((n_dec, D_MODEL)),
            _resident(k3.shape),
            _resident(v3.shape),
            _resident(sinks_t.shape),
            pl.BlockSpec((None, 1, D_MODEL), lambda s: (layer, 0, 0), pipeline_mode=one),
            pl.BlockSpec((1, wc, 2 * ATTN_WIDTH), w_row),
            pl.BlockSpec((None, 1, ATTN_WIDTH), lambda s: (layer, 0, 0), pipeline_mode=one),
            pl.BlockSpec((1, nblk, N_KV_HEADS, PAIR, 2 * BLOCK), lambda s: band(s) + (0, 0, 0)),
            pl.BlockSpec((1, nblk, 2 * BLOCK, N_KV_HEADS * PAIR), lambda s: band(s) + (0, 0)),
            pl.BlockSpec((1, wc, D_MODEL), w_row),
        ],
        out_specs=[
            pl.BlockSpec((tm, D_MODEL), tile),
            _resident((n_dec, D_MODEL)),
        ],
        out_shape=[
            jax.ShapeDtypeStruct((rows, D_MODEL), F32),
            jax.ShapeDtypeStruct((n_dec, D_MODEL), F32),
        ],
        scratch_shapes=[
            pltpu.VMEM((D_MODEL, 2 * ATTN_WIDTH), BF16),
            pltpu.VMEM((ATTN_WIDTH, D_MODEL), BF16),
            pltpu.VMEM((tm, D_MODEL), BF16),
            pltpu.VMEM((tm, ATTN_WIDTH), BF16),
            pltpu.VMEM((tm, ATTN_WIDTH), BF16),
            pltpu.VMEM((tm, ATTN_WIDTH), F32),
        ],
        compiler_params=pltpu.CompilerParams(
            dimension_semantics=("arbitrary",), vmem_limit_bytes=VMEM_LIMIT),
        name="attn_layer",
    )(sinks, x2d, xs, k3, v3, sinks_t, g, w_in, q_gain, kt_band, v_band, w_out)


def kernel(x_prompt, x_sample, state_pool, cache_k_win, cache_v_win, norm_a, w_in_a, w_grp_a, scale_a, w_out_a, norm_kv, w_kv, k_norm, norm_b, w_in_b, q_norm, sinks, w_out_b):
    bsz, seq, _ = x_prompt.shape
    n_dec = x_sample.shape[0]
    assert x_sample.shape[1] == 1 and cache_k_win.shape[1] == WINDOW
    tm_a, tm_b, tm_kv = 256, 256, 512

    hp = x_prompt.reshape(bsz * seq, D_MODEL)
    hs = x_sample[:, 0]
    pool_p, pool_s = [], []
    for l in range(N_A_LAYERS):
        hp, hist, hs, u_new = _pool_layer(
            hp, hs, state_pool, norm_a[:, None], w_in_a, w_grp_a, scale_a[:, None], w_out_a,
            layer=l, seq=seq, tm=tm_a)
        pool_p.append(hist[:, POOL_HIST - POOL_BUF:])
        pool_s.append(jnp.concatenate([state_pool[l][:, 1:], u_new[:, None]], axis=1))

    kt_band, v_band, k_win_p, v_win_p, kn, vn = _shared_kv(
        hp, hs, norm_kv[None], w_kv, jnp.tile(k_norm, 2)[None], seq=seq, tm=tm_kv)

    kv_shape = (n_dec, 1, N_KV_HEADS, HEAD_DIM)
    k_win_s = jnp.concatenate([cache_k_win[:, 1:], kn.reshape(kv_shape)], axis=1)
    v_win_s = jnp.concatenate([cache_v_win[:, 1:], vn.reshape(kv_shape)], axis=1)
    k3 = k_win_s.reshape(n_dec, WINDOW, KV_WIDTH).astype(BF16)
    v3 = v_win_s.reshape(n_dec, WINDOW, KV_WIDTH).astype(BF16)

    q_gain = jnp.tile(q_norm, (1, N_HEADS))[:, None]
    for j in range(N_B_LAYERS):
        hp, hs = _attn_layer(hp, hs, k3, v3, sinks, sinks.T, norm_b[:, None], w_in_b, q_gain,
                             kt_band, v_band, w_out_b, layer=j, seq=seq, tm=tm_b)

    win_shape = (bsz, WINDOW, N_KV_HEADS, HEAD_DIM)
    return (hp.reshape(bsz, seq, D_MODEL), hs[:, None],
            jnp.stack(pool_p, axis=0), jnp.stack(pool_s, axis=0),
            k_win_p.reshape(win_shape), v_win_p.reshape(win_shape), k_win_s, v_win_s)
```

```python
import functools
import math

import jax
import jax.numpy as jnp
from jax import lax
from jax.experimental import pallas as pl
from jax.experimental.pallas import tpu as pltpu

D_MODEL = 2048
N_A_LAYERS = 2
N_B_LAYERS = 2
POOL_WINDOWS = (2, 4, 8, 16)
N_POOL_GROUPS = len(POOL_WINDOWS)
POOL_WIDTH = D_MODEL
POOL_GROUP = POOL_WIDTH // N_POOL_GROUPS
POOL_BUF = max(POOL_WINDOWS) - 1
POOL_HIST = POOL_BUF + 1
HEAD_DIM = 64
N_HEADS = D_MODEL // HEAD_DIM
N_KV_HEADS = N_HEADS // 8
GQA_GROUP = N_HEADS // N_KV_HEADS
ATTN_WIDTH = N_HEADS * HEAD_DIM
KV_WIDTH = N_KV_HEADS * HEAD_DIM
WINDOW = 128
BLOCK = WINDOW
PAST_LEN = 16384
RMS_EPS = 1e-6
NEG_INF = -1e30
LOG2E = math.log2(math.e)
MIN_SLOPE = 2.0 ** -8
PAIR = 2 * HEAD_DIM
N_PAIRS = N_HEADS // 2
POOL_W_CHUNK = 128
ATTN_W_CHUNK = 64
DEC_CHUNK = 8
VMEM_LIMIT = 62 * 1024 * 1024

BF16 = jnp.bfloat16
F32 = jnp.float32


def _rms(x, g):
    return x * lax.rsqrt(jnp.mean(x * x, axis=-1, keepdims=True) + RMS_EPS) * g


def _silu(z):
    return z * jax.nn.sigmoid(z)


def _dot(a, b):
    return jnp.dot(a, b, preferred_element_type=F32)


def _resident(shape):
    return pl.BlockSpec(shape, lambda *_: (0,) * len(shape), pipeline_mode=pl.Buffered(1))


def _w_step(s, n_w):
    return jnp.minimum(s, n_w - 1)


def _tile_of(s, n_w, n_tiles):
    return jnp.clip(s - n_w, 0, n_tiles - 1)


def _pair_inv_rms(pair, pair_lo, scale):
    sq = pair * pair
    lo = jnp.sum(jnp.where(pair_lo, sq, 0.0), axis=-1, keepdims=True)
    hi = jnp.sum(jnp.where(pair_lo, 0.0, sq), axis=-1, keepdims=True)
    ms = jnp.where(pair_lo, lo, hi) * (1.0 / HEAD_DIM)
    return lax.rsqrt(ms + RMS_EPS) * scale


def _pool_tail(x_ref, u, z, window_sums, inv_cnt, wgrp_ref, scale_ref, wout_ref):
    mixed = []
    for g in range(N_POOL_GROUPS):
        cols = slice(g * POOL_GROUP, (g + 1) * POOL_GROUP)
        p = window_sums[g] * inv_cnt[g] - u[:, cols]
        mixed.append(_dot(p.astype(BF16), wgrp_ref[g]))
    y = jnp.concatenate(mixed, axis=-1) * scale_ref[...] * _silu(z)
    return x_ref[...] + _dot(y.astype(BF16), wout_ref[...])


def _pool_kernel(x_ref, xs_ref, st_ref, g_ref, win_ref, wgrp_ref, scale_ref, wout_ref,
                 out_ref, hist_ref, outs_ref, unew_ref,
                 win16_ref, wgrp16_ref, wout16_ref, uext_ref, *, tm, tiles_per_seq, n_tiles):
    s = pl.program_id(0)
    wc = POOL_W_CHUNK
    n_w = D_MODEL // wc

    @pl.when(s < n_w)
    def _():
        r0 = pl.multiple_of(s * wc, wc)
        win16_ref[pl.ds(r0, wc), :] = win_ref[0].astype(BF16)
        wout16_ref[pl.ds(r0, wc), :] = wout_ref[0].astype(BF16)
        chunks_per_group = POOL_GROUP // wc
        g0 = pl.multiple_of((s % chunks_per_group) * wc, wc)
        wgrp16_ref[s // chunks_per_group, pl.ds(g0, wc), :] = wgrp_ref[0, 0].astype(BF16)

    @pl.when((s >= n_w) & (s < n_w + n_tiles))
    def _():
        i = (s - n_w) % tiles_per_seq

        @pl.when(i == 0)
        def _():
            uext_ref[0:POOL_HIST, :] = jnp.zeros((POOL_HIST, POOL_WIDTH), F32)

        h = _rms(x_ref[...], g_ref[...]).astype(BF16)
        u = _dot(h, win16_ref[:, :POOL_WIDTH])
        z = _dot(h, win16_ref[:, POOL_WIDTH:])
        uext_ref[POOL_HIST:POOL_HIST + tm, :] = u

        pos = i * tm + lax.broadcasted_iota(jnp.int32, (tm, 1), 0)
        sums, inv_cnt = [], []
        for g, w in enumerate(POOL_WINDOWS):
            cols = slice(g * POOL_GROUP, (g + 1) * POOL_GROUP)
            acc = u[:, cols]
            for k in range(1, w):
                acc = acc + uext_ref[POOL_HIST - k:POOL_HIST - k + tm, cols]
            sums.append(acc)
            inv_cnt.append(1.0 / jnp.minimum(pos + 1, w).astype(F32))
        out_ref[...] = _pool_tail(x_ref, u, z, sums, inv_cnt, wgrp16_ref, scale_ref, wout16_ref)

        tail = uext_ref[tm:tm + POOL_HIST, :]
        hist_ref[0] = tail
        uext_ref[0:POOL_HIST, :] = tail

    @pl.when(s == n_w + n_tiles)
    def _():
        h = _rms(xs_ref[...], g_ref[...]).astype(BF16)
        u = _dot(h, win16_ref[:, :POOL_WIDTH])
        z = _dot(h, win16_ref[:, POOL_WIDTH:])
        unew_ref[...] = u
        hist_row = lax.broadcasted_iota(jnp.int32, (1, POOL_BUF, 1), 1)
        sums, inv_cnt = [], []
        for g, w in enumerate(POOL_WINDOWS):
            cols = slice(g * POOL_GROUP, (g + 1) * POOL_GROUP)
            hist = jnp.where(hist_row >= POOL_BUF - (w - 1), st_ref[0, :, :, cols], 0.0)
            sums.append(u[:, cols] + jnp.sum(hist, axis=1))
            inv_cnt.append(1.0 / float(min(PAST_LEN + 1, w)))
        outs_ref[...] = _pool_tail(xs_ref, u, z, sums, inv_cnt, wgrp16_ref, scale_ref, wout16_ref)


def _pool_layer(x2d, xs, state, g, w_in, w_grp, scale, w_out, *, layer, seq, tm):
    rows = x2d.shape[0]
    n_dec = xs.shape[0]
    n_tiles = rows // tm
    tiles_per_seq = seq // tm
    wc = POOL_W_CHUNK
    n_w = D_MODEL // wc
    chunks_per_group = POOL_GROUP // wc
    tile = lambda s: (_tile_of(s, n_w, n_tiles), 0)
    w_row = lambda s: (layer, _w_step(s, n_w), 0)
    one = pl.Buffered(1)
    return pl.pallas_call(
        functools.partial(_pool_kernel, tm=tm, tiles_per_seq=tiles_per_seq, n_tiles=n_tiles),
        grid=(n_w + n_tiles + 1,),
        in_specs=[
            pl.BlockSpec((tm, D_MODEL), tile),
            _resident((n_dec, D_MODEL)),
            pl.BlockSpec((1, n_dec, POOL_BUF, POOL_WIDTH), lambda s: (layer, 0, 0, 0), pipeline_mode=one),
            pl.BlockSpec((None, 1, D_MODEL), lambda s: (layer, 0, 0), pipeline_mode=one),
            pl.BlockSpec((1, wc, 2 * POOL_WIDTH), w_row),
            pl.BlockSpec((1, 1, wc, POOL_GROUP),
                         lambda s: (layer, _w_step(s, n_w) // chunks_per_group,
                                    _w_step(s, n_w) % chunks_per_group, 0)),
            pl.BlockSpec((None, 1, POOL_WIDTH), lambda s: (layer, 0, 0), pipeline_mode=one),
            pl.BlockSpec((1, wc, D_MODEL), w_row),
        ],
        out_specs=[
            pl.BlockSpec((tm, D_MODEL), tile),
            pl.BlockSpec((1, POOL_HIST, POOL_WIDTH),
                         lambda s: (_tile_of(s, n_w, n_tiles) // tiles_per_seq, 0, 0)),
            _resident((n_dec, D_MODEL)),
            _resident((n_dec, POOL_WIDTH)),
        ],
        out_shape=[
            jax.ShapeDtypeStruct((rows, D_MODEL), F32),
            jax.ShapeDtypeStruct((rows // seq, POOL_HIST, POOL_WIDTH), F32),
            jax.ShapeDtypeStruct((n_dec, D_MODEL), F32),
            jax.ShapeDtypeStruct((n_dec, POOL_WIDTH), F32),
        ],
        scratch_shapes=[
            pltpu.VMEM((D_MODEL, 2 * POOL_WIDTH), BF16),
            pltpu.VMEM((N_POOL_GROUPS, POOL_GROUP, POOL_GROUP), BF16),
            pltpu.VMEM((POOL_WIDTH, D_MODEL), BF16),
            pltpu.VMEM((POOL_HIST + tm, POOL_WIDTH), F32),
        ],
        compiler_params=pltpu.CompilerParams(
            dimension_semantics=("arbitrary",), vmem_limit_bytes=VMEM_LIMIT),
        name="pool_layer",
    )(x2d, xs, state, g, w_in, w_grp, scale, w_out)


def _kv_project(x, g_ref, w16_ref, kg_ref):
    h = _rms(x, g_ref[...]).astype(BF16)
    kv = _dot(h, w16_ref[...])
    pair_lo = lax.broadcasted_iota(jnp.int32, (1, PAIR), 1) < HEAD_DIM
    k_pairs = []
    for p in range(KV_WIDTH // PAIR):
        kp = kv[:, p * PAIR:(p + 1) * PAIR]
        k_pairs.append(kp * _pair_inv_rms(kp, pair_lo, 1.0) * kg_ref[...])
    return k_pairs, kv[:, KV_WIDTH:]


def _dup_heads(col, pair_lo):
    rolled = pltpu.roll(col, HEAD_DIM, 1)
    return jnp.where(pair_lo, col, rolled), jnp.where(pair_lo, rolled, col)


def _kv_kernel(x_ref, xs_ref, g_ref, wkv_ref, kg_ref,
               kt_ref, vb_ref, kwin_ref, vwin_ref, kn_ref, vn_ref,
               w16_ref, pk_ref, pv_ref, *, tm, tiles_per_seq, n_tiles):
    s = pl.program_id(0)
    nblk = tm // BLOCK
    pair_lo = lax.broadcasted_iota(jnp.int32, (1, PAIR), 1) < HEAD_DIM

    @pl.when(s == 0)
    def _():
        w16_ref[...] = wkv_ref[...].astype(BF16)

    @pl.when(s < n_tiles)
    def _():
        @pl.when(s % tiles_per_seq == 0)
        def _():
            pk_ref[...] = jnp.zeros(pk_ref.shape, BF16)
            pv_ref[...] = jnp.zeros(pv_ref.shape, BF16)

        k_pairs, v = _kv_project(x_ref[...], g_ref, w16_ref, kg_ref)
        for p in range(KV_WIDTH // PAIR):
            kwin_ref[0, :, p * PAIR:(p + 1) * PAIR] = k_pairs[p][tm - BLOCK:]
        vwin_ref[0] = v[tm - BLOCK:]
        for jb in range(nblk):
            rows = slice(jb * BLOCK, (jb + 1) * BLOCK)
            for p in range(KV_WIDTH // PAIR):
                k_dup = _dup_heads(k_pairs[p][rows], pair_lo)
                v_dup = _dup_heads(v[rows, p * PAIR:(p + 1) * PAIR], pair_lo)
                for half in range(2):
                    kh = 2 * p + half
                    cols = slice(kh * PAIR, (kh + 1) * PAIR)
                    kt_cur = k_dup[half].T.astype(BF16)
                    v_cur = v_dup[half].astype(BF16)
                    kt_ref[0, jb, kh, :, 0:BLOCK] = pk_ref[kh]
                    kt_ref[0, jb, kh, :, BLOCK:2 * BLOCK] = kt_cur
                    vb_ref[0, jb, 0:BLOCK, cols] = pv_ref[:, cols]
                    vb_ref[0, jb, BLOCK:2 * BLOCK, cols] = v_cur
                    pk_ref[kh] = kt_cur
                    pv_ref[:, cols] = v_cur

    @pl.when(s == n_tiles)
    def _():
        k_pairs, v = _kv_project(xs_ref[...], g_ref, w16_ref, kg_ref)
        for p in range(KV_WIDTH // PAIR):
            kn_ref[:, p * PAIR:(p + 1) * PAIR] = k_pairs[p]
        vn_ref[...] = v


def _shared_kv(x2d, xs, g, w_kv, k_gain_pair, *, seq, tm):
    rows = x2d.shape[0]
    bsz = rows // seq
    n_dec = xs.shape[0]
    n_tiles = rows // tm
    tiles_per_seq = seq // tm
    nblk = tm // BLOCK
    t = lambda s: jnp.minimum(s, n_tiles - 1)
    return pl.pallas_call(
        functools.partial(_kv_kernel, tm=tm, tiles_per_seq=tiles_per_seq, n_tiles=n_tiles),
        grid=(n_tiles + 1,),
        in_specs=[
            pl.BlockSpec((tm, D_MODEL), lambda s: (t(s), 0)),
            _resident((n_dec, D_MODEL)),
            _resident((1, D_MODEL)),
            _resident((D_MODEL, 2 * KV_WIDTH)),
            _resident((1, PAIR)),
        ],
        out_specs=[
            pl.BlockSpec((1, nblk, N_KV_HEADS, PAIR, 2 * BLOCK),
                         lambda s: (t(s) // tiles_per_seq, t(s) % tiles_per_seq, 0, 0, 0)),
            pl.BlockSpec((1, nblk, 2 * BLOCK, N_KV_HEADS * PAIR),
                         lambda s: (t(s) // tiles_per_seq, t(s) % tiles_per_seq, 0, 0)),
            pl.BlockSpec((1, BLOCK, KV_WIDTH), lambda s: (t(s) // tiles_per_seq, 0, 0)),
            pl.BlockSpec((1, BLOCK, KV_WIDTH), lambda s: (t(s) // tiles_per_seq, 0, 0)),
            _resident((n_dec, KV_WIDTH)),
            _resident((n_dec, KV_WIDTH)),
        ],
        out_shape=[
            jax.ShapeDtypeStruct((bsz, seq // BLOCK, N_KV_HEADS, PAIR, 2 * BLOCK), BF16),
            jax.ShapeDtypeStruct((bsz, seq // BLOCK, 2 * BLOCK, N_KV_HEADS * PAIR), BF16),
            jax.ShapeDtypeStruct((bsz, BLOCK, KV_WIDTH), F32),
            jax.ShapeDtypeStruct((bsz, BLOCK, KV_WIDTH), F32),
            jax.ShapeDtypeStruct((n_dec, KV_WIDTH), F32),
            jax.ShapeDtypeStruct((n_dec, KV_WIDTH), F32),
        ],
        scratch_shapes=[
            pltpu.VMEM((D_MODEL, 2 * KV_WIDTH), BF16),
            pltpu.VMEM((N_KV_HEADS, PAIR, BLOCK), BF16),
            pltpu.VMEM((BLOCK, N_KV_HEADS * PAIR), BF16),
        ],
        compiler_params=pltpu.CompilerParams(
            dimension_semantics=("arbitrary",), vmem_limit_bytes=VMEM_LIMIT),
        name="shared_kv",
    )(x2d, xs, g, w_kv, k_gain_pair)


def _normed_q_pair(q, pr, qg_ref, pair_lo, scale):
    cols = slice(pr * PAIR, (pr + 1) * PAIR)
    qp = q[:, cols]
    return qp * _pair_inv_rms(qp, pair_lo, scale) * qg_ref[:, cols]


def _decode_attention(q, k3, v3, sink_col):
    n, n_keys = k3.shape[0], k3.shape[1]
    rows = n * N_HEADS
    ri = lax.broadcasted_iota(jnp.int32, (rows, n), 0)
    ci = lax.broadcasted_iota(jnp.int32, (rows, n), 1)
    rep = jnp.where(ri // N_HEADS == ci, 1.0, 0.0).astype(BF16)
    row_h = lax.broadcasted_iota(jnp.int32, (rows, ATTN_WIDTH), 0) % N_HEADS
    col_h = lax.broadcasted_iota(jnp.int32, (rows, ATTN_WIDTH), 1) // HEAD_DIM
    own = row_h == col_h
    fj = lax.broadcasted_iota(jnp.int32, (ATTN_WIDTH, KV_WIDTH), 0)
    fc = lax.broadcasted_iota(jnp.int32, (ATTN_WIDTH, KV_WIDTH), 1)
    to_kv = jnp.where((fj % HEAD_DIM == fc % HEAD_DIM) & ((fj // HEAD_DIM) // GQA_GROUP == fc // HEAD_DIM),
                      1.0, 0.0).astype(BF16)
    tc = lax.broadcasted_iota(jnp.int32, (KV_WIDTH, ATTN_WIDTH), 0)
    tj = lax.broadcasted_iota(jnp.int32, (KV_WIDTH, ATTN_WIDTH), 1)
    from_kv = jnp.where((tj % HEAD_DIM == tc % HEAD_DIM) & ((tj // HEAD_DIM) // GQA_GROUP == tc // HEAD_DIM),
                        1.0, 0.0).astype(BF16)

    m1 = jnp.where(own, _dot(rep, q.astype(BF16)), 0.0).astype(BF16)
    qm3 = _dot(m1, to_kv).astype(BF16).reshape(n, N_HEADS, KV_WIDTH)
    s = jnp.einsum("bhc,bkc->bhk", qm3, k3, preferred_element_type=F32)
    h_col = lax.broadcasted_iota(jnp.int32, (N_HEADS, 1), 0).astype(F32)
    slope = jnp.exp2(-8.0 * (h_col + 1.0) / N_HEADS)[None]
    dist = (n_keys - 1 - lax.broadcasted_iota(jnp.int32, (1, n_keys), 1)).astype(F32)[None]
    s = s - slope * dist
    sink = sink_col[None]
    m = jnp.maximum(jnp.max(s, axis=-1, keepdims=True), sink)
    p = jnp.exp(s - m)
    den = jnp.sum(p, axis=-1, keepdims=True) + jnp.exp(sink - m)
    o3 = jnp.einsum("bhk,bkc->bhc", p.astype(BF16), v3, preferred_element_type=F32) / den
    o2 = o3.reshape(rows, KV_WIDTH)
    hi = o2.astype(BF16)
    lo = (o2 - hi.astype(F32)).astype(BF16)
    spread = jnp.where(own, _dot(hi, from_kv) + _dot(lo, from_kv), 0.0)
    return jnp.sum(spread.reshape(n, N_HEADS, ATTN_WIDTH), axis=1)


def _attn_kernel(sink_ref, x_ref, xs_ref, k3_ref, v3_ref, sinkc_ref, g_ref, win_ref, qg_ref,
                 kt_ref, vb_ref, wout_ref,
                 out_ref, outs_ref,
                 win16_ref, wout16_ref, h_ref, qlo_ref, qhi_ref,
                 *, layer, tm, tiles_per_seq, n_tiles):
    s = pl.program_id(0)
    nblk = tm // BLOCK
    pair_lo = lax.broadcasted_iota(jnp.int32, (1, PAIR), 1) < HEAD_DIM
    wc = ATTN_W_CHUNK
    n_w = D_MODEL // wc

    @pl.when(s < n_w)
    def _():
        r0 = pl.multiple_of(s * wc, wc)
        win16_ref[pl.ds(r0, wc), :] = win_ref[0].astype(BF16)
        wout16_ref[pl.ds(r0, wc), :] = wout_ref[0].astype(BF16)

    @pl.when((s >= n_w) & (s < n_w + n_tiles))
    def _():
        i = (s - n_w) % tiles_per_seq
        h_ref[...] = _rms(x_ref[...], g_ref[...]).astype(BF16)
        for kh in range(N_KV_HEADS):
            gcols = slice(kh * GQA_GROUP * HEAD_DIM, (kh + 1) * GQA_GROUP * HEAD_DIM)
            q = _dot(h_ref[...], win16_ref[:, gcols])
            for pr in range(GQA_GROUP // 2):
                cols = slice(pr * PAIR, (pr + 1) * PAIR)
                ocols = slice((kh * (GQA_GROUP // 2) + pr) * PAIR, (kh * (GQA_GROUP // 2) + pr + 1) * PAIR)
                qp = q[:, cols]
                qn = qp * _pair_inv_rms(qp, pair_lo, HEAD_DIM ** -0.5 * LOG2E) * qg_ref[:, ocols]
                qlo_ref[:, ocols] = jnp.where(pair_lo, qn, 0.0).astype(BF16)
                qhi_ref[:, ocols] = jnp.where(pair_lo, 0.0, qn).astype(BF16)

        r_idx = lax.broadcasted_iota(jnp.int32, (BLOCK, 2 * BLOCK), 0)
        c_idx = lax.broadcasted_iota(jnp.int32, (BLOCK, 2 * BLOCK), 1)
        dist = r_idx + BLOCK - c_idx
        in_window = (dist >= 0) & (dist < WINDOW)
        neg_dist = -dist.astype(F32)

        for blk in range(nblk):
            rows = slice(blk * BLOCK, (blk + 1) * BLOCK)
            first_key = (i * nblk + blk - 1) * BLOCK
            valid = in_window & (first_key + c_idx >= 0)
            nd = jnp.where(valid, neg_dist, NEG_INF / MIN_SLOPE)
            for kh in range(N_KV_HEADS):
                kt = kt_ref[0, blk, kh]
                vdup = vb_ref[0, blk, :, kh * PAIR:(kh + 1) * PAIR]
                pieces = []
                for pr in range(GQA_GROUP // 2):
                    cols = slice((kh * (GQA_GROUP // 2) + pr) * PAIR,
                                 (kh * (GQA_GROUP // 2) + pr + 1) * PAIR)
                    pieces.append(qlo_ref[rows, cols])
                    pieces.append(qhi_ref[rows, cols])
                s_all = _dot(jnp.concatenate(pieces, axis=0), kt)
                probs, inv_den = [], []
                for g_idx in range(GQA_GROUP):
                    head = kh * GQA_GROUP + g_idx
                    slope2 = 2.0 ** (-8.0 * (head + 1) / N_HEADS) * LOG2E
                    sc = s_all[g_idx * BLOCK:(g_idx + 1) * BLOCK] + slope2 * nd
                    sink2 = sink_ref[layer, head] * LOG2E
                    m = jnp.maximum(jnp.max(sc, axis=-1, keepdims=True), sink2)
                    p = jnp.exp2(sc - m)
                    den = jnp.sum(p, axis=-1, keepdims=True) + jnp.exp2(sink2 - m)
                    probs.append(p.astype(BF16))
                    inv_den.append(1.0 / den)
                o_all = _dot(jnp.concatenate(probs, axis=0), vdup)
                for pr in range(GQA_GROUP // 2):
                    cols = slice((kh * (GQA_GROUP // 2) + pr) * PAIR,
                                 (kh * (GQA_GROUP // 2) + pr + 1) * PAIR)
                    o_even = o_all[(2 * pr) * BLOCK:(2 * pr + 1) * BLOCK] * inv_den[2 * pr]
                    o_odd = o_all[(2 * pr + 1) * BLOCK:(2 * pr + 2) * BLOCK] * inv_den[2 * pr + 1]
                    out_ref[rows, cols] = jnp.where(pair_lo, o_even, o_odd)

        z = _dot(h_ref[...], win16_ref[:, ATTN_WIDTH:])
        y = out_ref[...] * _silu(z)
        out_ref[...] = x_ref[...] + _dot(y.astype(BF16), wout16_ref[...])

    @pl.when(s == n_w + n_tiles)
    def _():
        x = xs_ref[...]
        h = _rms(x, g_ref[...]).astype(BF16)
        q = _dot(h, win16_ref[:, :ATTN_WIDTH])
        z = _dot(h, win16_ref[:, ATTN_WIDTH:])
        qn = jnp.concatenate(
            [_normed_q_pair(q, pr, qg_ref, pair_lo, HEAD_DIM ** -0.5) for pr in range(N_PAIRS)], axis=-1)
        n_dec = x.shape[0]
        o_rows = []
        for c in range(n_dec // DEC_CHUNK):
            rows = slice(c * DEC_CHUNK, (c + 1) * DEC_CHUNK)
            o_rows.append(_decode_attention(qn[rows], k3_ref[rows], v3_ref[rows],
                                            sinkc_ref[:, layer:layer + 1]))
        y = jnp.concatenate(o_rows, axis=0) * _silu(z)
        outs_ref[...] = x + _dot(y.astype(BF16), wout16_ref[...])


def _attn_layer(x2d, xs, k3, v3, sinks, sinks_t, g, w_in, q_gain, kt_band, v_band, w_out,
                *, layer, seq, tm):
    rows = x2d.shape[0]
    n_dec = xs.shape[0]
    n_tiles = rows // tm
    tiles_per_seq = seq // tm
    nblk = tm // BLOCK
    wc = ATTN_W_CHUNK
    n_w = D_MODEL // wc
    tile = lambda s: (_tile_of(s, n_w, n_tiles), 0)
    band = lambda s: (_tile_of(s, n_w, n_tiles) // tiles_per_seq,
                      _tile_of(s, n_w, n_tiles) % tiles_per_seq)
    w_row = lambda s: (layer, _w_step(s, n_w), 0)
    one = pl.Buffered(1)
    return pl.pallas_call(
        functools.partial(_attn_kernel, layer=layer, tm=tm, tiles_per_seq=tiles_per_seq, n_tiles=n_tiles),
        grid=(n_w + n_tiles + 1,),
        in_specs=[
            pl.BlockSpec(memory_space=pltpu.SMEM),
            pl.BlockSpec((tm, D_MODEL), tile),
            _resident((n_dec, D_MODEL)),
            _resident(k3.shape),
            _resident(v3.shape),
            _resident(sinks_t.shape),
            pl.BlockSpec((None, 1, D_MODEL), lambda s: (layer, 0, 0), pipeline_mode=one),
            pl.BlockSpec((1, wc, 2 * ATTN_WIDTH), w_row),
            pl.BlockSpec((None, 1, ATTN_WIDTH), lambda s: (layer, 0, 0), pipeline_mode=one),
            pl.BlockSpec((1, nblk, N_KV_HEADS, PAIR, 2 * BLOCK), lambda s: band(s) + (0, 0, 0)),
            pl.BlockSpec((1, nblk, 2 * BLOCK, N_KV_HEADS * PAIR), lambda s: band(s) + (0, 0)),
            pl.BlockSpec((1, wc, D_MODEL), w_row),
        ],
        out_specs=[
            pl.BlockSpec((tm, D_MODEL), tile),
            _resident((n_dec, D_MODEL)),
        ],
        out_shape=[
            jax.ShapeDtypeStruct((rows, D_MODEL), F32),
            jax.ShapeDtypeStruct((n_dec, D_MODEL), F32),
        ],
        scratch_shapes=[
            pltpu.VMEM((D_MODEL, 2 * ATTN_WIDTH), BF16),
            pltpu.VMEM((ATTN_WIDTH, D_MODEL), BF16),
            pltpu.VMEM((tm, D_MODEL), BF16),
            pltpu.VMEM((tm, ATTN_WIDTH), BF16),
            pltpu.VMEM((tm, ATTN_WIDTH), BF16),
        ],
        compiler_params=pltpu.CompilerParams(
            dimension_semantics=("arbitrary",), vmem_limit_bytes=VMEM_LIMIT),
        name="attn_layer",
    )(sinks, x2d, xs, k3, v3, sinks_t, g, w_in, q_gain, kt_band, v_band, w_out)


def kernel(x_prompt, x_sample, state_pool, cache_k_win, cache_v_win, norm_a, w_in_a, w_grp_a, scale_a, w_out_a, norm_kv, w_kv, k_norm, norm_b, w_in_b, q_norm, sinks, w_out_b):
    bsz, seq, _ = x_prompt.shape
    n_dec = x_sample.shape[0]
    assert x_sample.shape[1] == 1 and cache_k_win.shape[1] == WINDOW
    tm_a, tm_b, tm_kv = 256, 256, 1024

    hp = x_prompt.reshape(bsz * seq, D_MODEL)
    hs = x_sample[:, 0]
    pool_p, pool_s = [], []
    for l in range(N_A_LAYERS):
        hp, hist, hs, u_new = _pool_layer(
            hp, hs, state_pool, norm_a[:, None], w_in_a, w_grp_a, scale_a[:, None], w_out_a,
            layer=l, seq=seq, tm=tm_a)
        pool_p.append(hist[:, POOL_HIST - POOL_BUF:])
        pool_s.append(jnp.concatenate([state_pool[l][:, 1:], u_new[:, None]], axis=1))

    kt_band, v_band, k_win_p, v_win_p, kn, vn = _shared_kv(
        hp, hs, norm_kv[None], w_kv, jnp.tile(k_norm, 2)[None], seq=seq, tm=tm_kv)

    kv_shape = (n_dec, 1, N_KV_HEADS, HEAD_DIM)
    k_win_s = jnp.concatenate([cache_k_win[:, 1:], kn.reshape(kv_shape)], axis=1)
    v_win_s = jnp.concatenate([cache_v_win[:, 1:], vn.reshape(kv_shape)], axis=1)
    k3 = k_win_s.reshape(n_dec, WINDOW, KV_WIDTH).astype(BF16)
    v3 = v_win_s.reshape(n_dec, WINDOW, KV_WIDTH).astype(BF16)

    q_gain = jnp.tile(q_norm, (1, N_HEADS))[:, None]
    for j in range(N_B_LAYERS):
        hp, hs = _attn_layer(hp, hs, k3, v3, sinks, sinks.T, norm_b[:, None], w_in_b, q_gain,
                             kt_band, v_band, w_out_b, layer=j, seq=seq, tm=tm_b)

    win_shape = (bsz, WINDOW, N_KV_HEADS, HEAD_DIM)
    return (hp.reshape(bsz, seq, D_MODEL), hs[:, None],
            jnp.stack(pool_p, axis=0), jnp.stack(pool_s, axis=0),
            k_win_p.reshape(win_shape), v_win_p.reshape(win_shape), k_win_s, v_win_s)
```

```python
import functools
import math

import jax
import jax.numpy as jnp
from jax import lax
from jax.experimental import pallas as pl
from jax.experimental.pallas import tpu as pltpu

D_MODEL = 2048
N_A_LAYERS = 2
N_B_LAYERS = 2
POOL_WINDOWS = (2, 4, 8, 16)
N_POOL_GROUPS = len(POOL_WINDOWS)
POOL_WIDTH = D_MODEL
POOL_GROUP = POOL_WIDTH // N_POOL_GROUPS
POOL_BUF = max(POOL_WINDOWS) - 1
POOL_HIST = POOL_BUF + 1
HEAD_DIM = 64
N_HEADS = D_MODEL // HEAD_DIM
N_KV_HEADS = N_HEADS // 8
GQA_GROUP = N_HEADS // N_KV_HEADS
ATTN_WIDTH = N_HEADS * HEAD_DIM
KV_WIDTH = N_KV_HEADS * HEAD_DIM
WINDOW = 128
BLOCK = WINDOW
PAST_LEN = 16384
RMS_EPS = 1e-6
NEG_INF = -1e30
LOG2E = math.log2(math.e)
MIN_SLOPE = 2.0 ** -8
PAIR = 2 * HEAD_DIM
N_PAIRS = N_HEADS // 2
POOL_W_CHUNK = 128
ATTN_W_CHUNK = 64
DEC_CHUNK = 8
VMEM_LIMIT = 62 * 1024 * 1024

BF16 = jnp.bfloat16
F32 = jnp.float32


def _rms(x, g):
    return x * lax.rsqrt(jnp.mean(x * x, axis=-1, keepdims=True) + RMS_EPS) * g


def _silu(z):
    return z * jax.nn.sigmoid(z)


def _dot(a, b):
    return jnp.dot(a, b, preferred_element_type=F32)


def _resident(shape):
    return pl.BlockSpec(shape, lambda *_: (0,) * len(shape), pipeline_mode=pl.Buffered(1))


def _w_step(s, n_w):
    return jnp.minimum(s, n_w - 1)


def _tile_of(s, n_w, n_tiles):
    return jnp.clip(s - n_w, 0, n_tiles - 1)


def _pair_inv_rms(pair, pair_lo, scale):
    sq = pair * pair
    lo = jnp.sum(jnp.where(pair_lo, sq, 0.0), axis=-1, keepdims=True)
    hi = jnp.sum(jnp.where(pair_lo, 0.0, sq), axis=-1, keepdims=True)
    ms = jnp.where(pair_lo, lo, hi) * (1.0 / HEAD_DIM)
    return lax.rsqrt(ms + RMS_EPS) * scale


def _pool_tail(x_ref, u, z, window_sums, inv_cnt, wgrp_ref, scale_ref, wout_ref):
    mixed = []
    for g in range(N_POOL_GROUPS):
        cols = slice(g * POOL_GROUP, (g + 1) * POOL_GROUP)
        p = window_sums[g] * inv_cnt[g] - u[:, cols]
        mixed.append(_dot(p.astype(BF16), wgrp_ref[g]))
    y = jnp.concatenate(mixed, axis=-1) * scale_ref[...] * _silu(z)
    return x_ref[...] + _dot(y.astype(BF16), wout_ref[...])


def _pool_kernel(x_ref, xs_ref, st_ref, g_ref, win_ref, wgrp_ref, scale_ref, wout_ref,
                 out_ref, hist_ref, outs_ref, newst_ref,
                 win16_ref, wgrp16_ref, wout16_ref, uext_ref, *, tm, tiles_per_seq, n_tiles):
    s = pl.program_id(0)
    wc = POOL_W_CHUNK
    n_w = D_MODEL // wc

    @pl.when(s < n_w)
    def _():
        r0 = pl.multiple_of(s * wc, wc)
        win16_ref[pl.ds(r0, wc), :] = win_ref[0].astype(BF16)
        wout16_ref[pl.ds(r0, wc), :] = wout_ref[0].astype(BF16)
        chunks_per_group = POOL_GROUP // wc
        g0 = pl.multiple_of((s % chunks_per_group) * wc, wc)
        wgrp16_ref[s // chunks_per_group, pl.ds(g0, wc), :] = wgrp_ref[0, 0].astype(BF16)

    @pl.when((s >= n_w) & (s < n_w + n_tiles))
    def _():
        i = (s - n_w) % tiles_per_seq

        @pl.when(i == 0)
        def _():
            uext_ref[0:POOL_HIST, :] = jnp.zeros((POOL_HIST, POOL_WIDTH), F32)

        h = _rms(x_ref[...], g_ref[...]).astype(BF16)
        u = _dot(h, win16_ref[:, :POOL_WIDTH])
        z = _dot(h, win16_ref[:, POOL_WIDTH:])
        uext_ref[POOL_HIST:POOL_HIST + tm, :] = u

        pos = i * tm + lax.broadcasted_iota(jnp.int32, (tm, 1), 0)
        sums, inv_cnt = [], []
        for g, w in enumerate(POOL_WINDOWS):
            cols = slice(g * POOL_GROUP, (g + 1) * POOL_GROUP)
            acc = u[:, cols]
            for k in range(1, w):
                acc = acc + uext_ref[POOL_HIST - k:POOL_HIST - k + tm, cols]
            sums.append(acc)
            inv_cnt.append(1.0 / jnp.minimum(pos + 1, w).astype(F32))
        out_ref[...] = _pool_tail(x_ref, u, z, sums, inv_cnt, wgrp16_ref, scale_ref, wout16_ref)

        tail = uext_ref[tm:tm + POOL_HIST, :]
        hist_ref[0] = tail
        uext_ref[0:POOL_HIST, :] = tail

    @pl.when(s == n_w + n_tiles)
    def _():
        h = _rms(xs_ref[...], g_ref[...]).astype(BF16)
        u = _dot(h, win16_ref[:, :POOL_WIDTH])
        z = _dot(h, win16_ref[:, POOL_WIDTH:])
        for r in range(POOL_BUF - 1):
            newst_ref[r] = st_ref[0, r + 1]
        newst_ref[POOL_BUF - 1] = u
        sums, inv_cnt = [], []
        for g, w in enumerate(POOL_WINDOWS):
            cols = slice(g * POOL_GROUP, (g + 1) * POOL_GROUP)
            acc = u[:, cols]
            for k in range(1, w):
                acc = acc + st_ref[0, POOL_BUF - k, :, cols]
            sums.append(acc)
            inv_cnt.append(1.0 / float(min(PAST_LEN + 1, w)))
        outs_ref[...] = _pool_tail(xs_ref, u, z, sums, inv_cnt, wgrp16_ref, scale_ref, wout16_ref)


def _pool_layer(x2d, xs, state, g, w_in, w_grp, scale, w_out, *, layer, seq, tm):
    rows = x2d.shape[0]
    n_dec = xs.shape[0]
    n_tiles = rows // tm
    tiles_per_seq = seq // tm
    wc = POOL_W_CHUNK
    n_w = D_MODEL // wc
    chunks_per_group = POOL_GROUP // wc
    tile = lambda s: (_tile_of(s, n_w, n_tiles), 0)
    w_row = lambda s: (layer, _w_step(s, n_w), 0)
    one = pl.Buffered(1)
    return pl.pallas_call(
        functools.partial(_pool_kernel, tm=tm, tiles_per_seq=tiles_per_seq, n_tiles=n_tiles),
        grid=(n_w + n_tiles + 1,),
        in_specs=[
            pl.BlockSpec((tm, D_MODEL), tile),
            _resident((n_dec, D_MODEL)),
            pl.BlockSpec((1, POOL_BUF, n_dec, POOL_WIDTH), lambda s: (layer, 0, 0, 0), pipeline_mode=one),
            pl.BlockSpec((None, 1, D_MODEL), lambda s: (layer, 0, 0), pipeline_mode=one),
            pl.BlockSpec((1, wc, 2 * POOL_WIDTH), w_row),
            pl.BlockSpec((1, 1, wc, POOL_GROUP),
                         lambda s: (layer, _w_step(s, n_w) // chunks_per_group,
                                    _w_step(s, n_w) % chunks_per_group, 0)),
            pl.BlockSpec((None, 1, POOL_WIDTH), lambda s: (layer, 0, 0), pipeline_mode=one),
            pl.BlockSpec((1, wc, D_MODEL), w_row),
        ],
        out_specs=[
            pl.BlockSpec((tm, D_MODEL), tile),
            pl.BlockSpec((1, POOL_HIST, POOL_WIDTH),
                         lambda s: (_tile_of(s, n_w, n_tiles) // tiles_per_seq, 0, 0)),
            _resident((n_dec, D_MODEL)),
            _resident((POOL_BUF, n_dec, POOL_WIDTH)),
        ],
        out_shape=[
            jax.ShapeDtypeStruct((rows, D_MODEL), F32),
            jax.ShapeDtypeStruct((rows // seq, POOL_HIST, POOL_WIDTH), F32),
            jax.ShapeDtypeStruct((n_dec, D_MODEL), F32),
            jax.ShapeDtypeStruct((POOL_BUF, n_dec, POOL_WIDTH), F32),
        ],
        scratch_shapes=[
            pltpu.VMEM((D_MODEL, 2 * POOL_WIDTH), BF16),
            pltpu.VMEM((N_POOL_GROUPS, POOL_GROUP, POOL_GROUP), BF16),
            pltpu.VMEM((POOL_WIDTH, D_MODEL), BF16),
            pltpu.VMEM((POOL_HIST + tm, POOL_WIDTH), F32),
        ],
        compiler_params=pltpu.CompilerParams(
            dimension_semantics=("arbitrary",), vmem_limit_bytes=VMEM_LIMIT),
        name="pool_layer",
    )(x2d, xs, state, g, w_in, w_grp, scale, w_out)


def _kv_project(x, g_ref, w16_ref, kg_ref):
    h = _rms(x, g_ref[...]).astype(BF16)
    kv = _dot(h, w16_ref[...])
    pair_lo = lax.broadcasted_iota(jnp.int32, (1, PAIR), 1) < HEAD_DIM
    k_pairs = []
    for p in range(KV_WIDTH // PAIR):
        kp = kv[:, p * PAIR:(p + 1) * PAIR]
        k_pairs.append(kp * _pair_inv_rms(kp, pair_lo, 1.0) * kg_ref[...])
    return k_pairs, kv[:, KV_WIDTH:]


def _dup_heads(col, pair_lo):
    rolled = pltpu.roll(col, HEAD_DIM, 1)
    return jnp.where(pair_lo, col, rolled), jnp.where(pair_lo, rolled, col)


def _slide_window(cache_ref, new_t, win_ref, win16_ref):
    n, _, n_keys = cache_ref.shape
    last = lax.broadcasted_iota(jnp.int32, (1, n_keys), 1) == n_keys - 1
    for b in range(n):
        slid = jnp.where(last, new_t[:, b:b + 1], pltpu.roll(cache_ref[b], n_keys - 1, 1))
        win_ref[b] = slid
        win16_ref[b] = slid.astype(BF16)


def _kv_kernel(x_ref, xs_ref, ck_ref, cv_ref, g_ref, wkv_ref, kg_ref,
               kt_ref, vb_ref, kwin_ref, vwin_ref, kws_ref, vws_ref, k16_ref, v16_ref,
               w16_ref, pk_ref, pv_ref, *, tm, tiles_per_seq, n_tiles):
    s = pl.program_id(0)
    nblk = tm // BLOCK
    pair_lo = lax.broadcasted_iota(jnp.int32, (1, PAIR), 1) < HEAD_DIM

    @pl.when(s == 0)
    def _():
        w16_ref[...] = wkv_ref[...].astype(BF16)

    @pl.when(s < n_tiles)
    def _():
        @pl.when(s % tiles_per_seq == 0)
        def _():
            pk_ref[...] = jnp.zeros(pk_ref.shape, BF16)
            pv_ref[...] = jnp.zeros(pv_ref.shape, BF16)

        k_pairs, v = _kv_project(x_ref[...], g_ref, w16_ref, kg_ref)
        for jb in range(nblk):
            rows = slice(jb * BLOCK, (jb + 1) * BLOCK)
            for p in range(KV_WIDTH // PAIR):
                k_dup = _dup_heads(k_pairs[p][rows], pair_lo)
                v_dup = _dup_heads(v[rows, p * PAIR:(p + 1) * PAIR], pair_lo)
                for half in range(2):
                    kh = 2 * p + half
                    cols = slice(kh * PAIR, (kh + 1) * PAIR)
                    kt_f32 = k_dup[half].T
                    kt_cur = kt_f32.astype(BF16)
                    v_cur = v_dup[half].astype(BF16)
                    if jb == nblk - 1:
                        hd = slice(kh * HEAD_DIM, (kh + 1) * HEAD_DIM)
                        kwin_ref[0, hd, :] = kt_f32[:HEAD_DIM]
                        vwin_ref[0, hd, :] = v_dup[half].T[:HEAD_DIM]
                    kt_ref[0, jb, kh, :, 0:BLOCK] = pk_ref[kh]
                    kt_ref[0, jb, kh, :, BLOCK:2 * BLOCK] = kt_cur
                    vb_ref[0, jb, 0:BLOCK, cols] = pv_ref[:, cols]
                    vb_ref[0, jb, BLOCK:2 * BLOCK, cols] = v_cur
                    pk_ref[kh] = kt_cur
                    pv_ref[:, cols] = v_cur

    @pl.when(s == n_tiles)
    def _():
        k_pairs, v = _kv_project(xs_ref[...], g_ref, w16_ref, kg_ref)
        n_dec = xs_ref.shape[0]
        pad = jnp.zeros((BLOCK - n_dec, KV_WIDTH), F32)
        kn_t = jnp.concatenate([jnp.concatenate(k_pairs, axis=1), pad], axis=0)
        vn_t = jnp.concatenate([v, pad], axis=0)
        kn_t = jnp.concatenate([kn_t[:, c * BLOCK:(c + 1) * BLOCK].T for c in range(KV_WIDTH // BLOCK)], axis=0)
        vn_t = jnp.concatenate([vn_t[:, c * BLOCK:(c + 1) * BLOCK].T for c in range(KV_WIDTH // BLOCK)], axis=0)
        _slide_window(ck_ref, kn_t, kws_ref, k16_ref)
        _slide_window(cv_ref, vn_t, vws_ref, v16_ref)


def _shared_kv(x2d, xs, cache_kt, cache_vt, g, w_kv, k_gain_pair, *, seq, tm):
    rows = x2d.shape[0]
    bsz = rows // seq
    n_dec = xs.shape[0]
    n_tiles = rows // tm
    tiles_per_seq = seq // tm
    nblk = tm // BLOCK
    t = lambda s: jnp.minimum(s, n_tiles - 1)
    return pl.pallas_call(
        functools.partial(_kv_kernel, tm=tm, tiles_per_seq=tiles_per_seq, n_tiles=n_tiles),
        grid=(n_tiles + 1,),
        in_specs=[
            pl.BlockSpec((tm, D_MODEL), lambda s: (t(s), 0)),
            _resident((n_dec, D_MODEL)),
            _resident(cache_kt.shape),
            _resident(cache_vt.shape),
            _resident((1, D_MODEL)),
            _resident((D_MODEL, 2 * KV_WIDTH)),
            _resident((1, PAIR)),
        ],
        out_specs=[
            pl.BlockSpec((1, nblk, N_KV_HEADS, PAIR, 2 * BLOCK),
                         lambda s: (t(s) // tiles_per_seq, t(s) % tiles_per_seq, 0, 0, 0)),
            pl.BlockSpec((1, nblk, 2 * BLOCK, N_KV_HEADS * PAIR),
                         lambda s: (t(s) // tiles_per_seq, t(s) % tiles_per_seq, 0, 0)),
            pl.BlockSpec((1, KV_WIDTH, BLOCK), lambda s: (t(s) // tiles_per_seq, 0, 0)),
            pl.BlockSpec((1, KV_WIDTH, BLOCK), lambda s: (t(s) // tiles_per_seq, 0, 0)),
            _resident(cache_kt.shape),
            _resident(cache_vt.shape),
            _resident(cache_kt.shape),
            _resident(cache_vt.shape),
        ],
        out_shape=[
            jax.ShapeDtypeStruct((bsz, seq // BLOCK, N_KV_HEADS, PAIR, 2 * BLOCK), BF16),
            jax.ShapeDtypeStruct((bsz, seq // BLOCK, 2 * BLOCK, N_KV_HEADS * PAIR), BF16),
            jax.ShapeDtypeStruct((bsz, KV_WIDTH, BLOCK), F32),
            jax.ShapeDtypeStruct((bsz, KV_WIDTH, BLOCK), F32),
            jax.ShapeDtypeStruct(cache_kt.shape, F32),
            jax.ShapeDtypeStruct(cache_vt.shape, F32),
            jax.ShapeDtypeStruct(cache_kt.shape, BF16),
            jax.ShapeDtypeStruct(cache_vt.shape, BF16),
        ],
        scratch_shapes=[
            pltpu.VMEM((D_MODEL, 2 * KV_WIDTH), BF16),
            pltpu.VMEM((N_KV_HEADS, PAIR, BLOCK), BF16),
            pltpu.VMEM((BLOCK, N_KV_HEADS * PAIR), BF16),
        ],
        compiler_params=pltpu.CompilerParams(
            dimension_semantics=("arbitrary",), vmem_limit_bytes=VMEM_LIMIT),
        name="shared_kv",
    )(x2d, xs, cache_kt, cache_vt, g, w_kv, k_gain_pair)


def _normed_q_pair(q, pr, qg_ref, pair_lo, scale):
    cols = slice(pr * PAIR, (pr + 1) * PAIR)
    qp = q[:, cols]
    return qp * _pair_inv_rms(qp, pair_lo, scale) * qg_ref[:, cols]


def _decode_attention(q, k3, v3, sink_col):
    n, n_keys = k3.shape[0], k3.shape[2]
    rows = n * N_HEADS
    ri = lax.broadcasted_iota(jnp.int32, (rows, n), 0)
    ci = lax.broadcasted_iota(jnp.int32, (rows, n), 1)
    rep = jnp.where(ri // N_HEADS == ci, 1.0, 0.0).astype(BF16)
    row_h = lax.broadcasted_iota(jnp.int32, (rows, ATTN_WIDTH), 0) % N_HEADS
    col_h = lax.broadcasted_iota(jnp.int32, (rows, ATTN_WIDTH), 1) // HEAD_DIM
    own = row_h == col_h
    fj = lax.broadcasted_iota(jnp.int32, (ATTN_WIDTH, KV_WIDTH), 0)
    fc = lax.broadcasted_iota(jnp.int32, (ATTN_WIDTH, KV_WIDTH), 1)
    to_kv = jnp.where((fj % HEAD_DIM == fc % HEAD_DIM) & ((fj // HEAD_DIM) // GQA_GROUP == fc // HEAD_DIM),
                      1.0, 0.0).astype(BF16)
    tc = lax.broadcasted_iota(jnp.int32, (KV_WIDTH, ATTN_WIDTH), 0)
    tj = lax.broadcasted_iota(jnp.int32, (KV_WIDTH, ATTN_WIDTH), 1)
    from_kv = jnp.where((tj % HEAD_DIM == tc % HEAD_DIM) & ((tj // HEAD_DIM) // GQA_GROUP == tc // HEAD_DIM),
                        1.0, 0.0).astype(BF16)

    m1 = jnp.where(own, _dot(rep, q.astype(BF16)), 0.0).astype(BF16)
    qm3 = _dot(m1, to_kv).astype(BF16).reshape(n, N_HEADS, KV_WIDTH)
    s = jnp.einsum("bhc,bck->bhk", qm3, k3, preferred_element_type=F32)
    h_col = lax.broadcasted_iota(jnp.int32, (N_HEADS, 1), 0).astype(F32)
    slope = jnp.exp2(-8.0 * (h_col + 1.0) / N_HEADS)[None]
    dist = (n_keys - 1 - lax.broadcasted_iota(jnp.int32, (1, n_keys), 1)).astype(F32)[None]
    s = s - slope * dist
    sink = sink_col[None]
    m = jnp.maximum(jnp.max(s, axis=-1, keepdims=True), sink)
    p = jnp.exp(s - m)
    den = jnp.sum(p, axis=-1, keepdims=True) + jnp.exp(sink - m)
    o3 = jnp.einsum("bhk,bck->bhc", p.astype(BF16), v3, preferred_element_type=F32) / den
    o2 = o3.reshape(rows, KV_WIDTH)
    hi = o2.astype(BF16)
    lo = (o2 - hi.astype(F32)).astype(BF16)
    spread = jnp.where(own, _dot(hi, from_kv) + _dot(lo, from_kv), 0.0)
    return jnp.sum(spread.reshape(n, N_HEADS, ATTN_WIDTH), axis=1)


def _attn_kernel(sink_ref, x_ref, xs_ref, k3_ref, v3_ref, sinkc_ref, g_ref, win_ref, qg_ref,
                 kt_ref, vb_ref, wout_ref,
                 out_ref, outs_ref,
                 win16_ref, wout16_ref, h_ref, qlo_ref, qhi_ref,
                 *, layer, tm, tiles_per_seq, n_tiles):
    s = pl.program_id(0)
    nblk = tm // BLOCK
    pair_lo = lax.broadcasted_iota(jnp.int32, (1, PAIR), 1) < HEAD_DIM
    wc = ATTN_W_CHUNK
    n_w = D_MODEL // wc

    @pl.when(s < n_w)
    def _():
        r0 = pl.multiple_of(s * wc, wc)
        win16_ref[pl.ds(r0, wc), :] = win_ref[0].astype(BF16)
        wout16_ref[pl.ds(r0, wc), :] = wout_ref[0].astype(BF16)

    @pl.when((s >= n_w) & (s < n_w + n_tiles))
    def _():
        i = (s - n_w) % tiles_per_seq
        h_ref[...] = _rms(x_ref[...], g_ref[...]).astype(BF16)
        for kh in range(N_KV_HEADS):
            gcols = slice(kh * GQA_GROUP * HEAD_DIM, (kh + 1) * GQA_GROUP * HEAD_DIM)
            q = _dot(h_ref[...], win16_ref[:, gcols])
            for pr in range(GQA_GROUP // 2):
                cols = slice(pr * PAIR, (pr + 1) * PAIR)
                ocols = slice((kh * (GQA_GROUP // 2) + pr) * PAIR, (kh * (GQA_GROUP // 2) + pr + 1) * PAIR)
                qp = q[:, cols]
                qn = qp * _pair_inv_rms(qp, pair_lo, HEAD_DIM ** -0.5 * LOG2E) * qg_ref[:, ocols]
                qlo_ref[:, ocols] = jnp.where(pair_lo, qn, 0.0).astype(BF16)
                qhi_ref[:, ocols] = jnp.where(pair_lo, 0.0, qn).astype(BF16)

        r_idx = lax.broadcasted_iota(jnp.int32, (BLOCK, 2 * BLOCK), 0)
        c_idx = lax.broadcasted_iota(jnp.int32, (BLOCK, 2 * BLOCK), 1)
        dist = r_idx + BLOCK - c_idx
        in_window = (dist >= 0) & (dist < WINDOW)
        neg_dist = -dist.astype(F32)

        for blk in range(nblk):
            rows = slice(blk * BLOCK, (blk + 1) * BLOCK)
            first_key = (i * nblk + blk - 1) * BLOCK
            valid = in_window & (first_key + c_idx >= 0)
            nd = jnp.where(valid, neg_dist, NEG_INF / MIN_SLOPE)
            for kh in range(N_KV_HEADS):
                kt = kt_ref[0, blk, kh]
                vdup = vb_ref[0, blk, :, kh * PAIR:(kh + 1) * PAIR]
                pieces = []
                for pr in range(GQA_GROUP // 2):
                    cols = slice((kh * (GQA_GROUP // 2) + pr) * PAIR,
                                 (kh * (GQA_GROUP // 2) + pr + 1) * PAIR)
                    pieces.append(qlo_ref[rows, cols])
                    pieces.append(qhi_ref[rows, cols])
                s_all = _dot(jnp.concatenate(pieces, axis=0), kt)
                probs, inv_den = [], []
                for g_idx in range(GQA_GROUP):
                    head = kh * GQA_GROUP + g_idx
                    slope2 = 2.0 ** (-8.0 * (head + 1) / N_HEADS) * LOG2E
                    sc = s_all[g_idx * BLOCK:(g_idx + 1) * BLOCK] + slope2 * nd
                    sink2 = sink_ref[layer, head] * LOG2E
                    m = jnp.maximum(jnp.max(sc, axis=-1, keepdims=True), sink2)
                    p = jnp.exp2(sc - m)
                    den = jnp.sum(p, axis=-1, keepdims=True) + jnp.exp2(sink2 - m)
                    probs.append(p.astype(BF16))
                    inv_den.append(1.0 / den)
                o_all = _dot(jnp.concatenate(probs, axis=0), vdup)
                for pr in range(GQA_GROUP // 2):
                    cols = slice((kh * (GQA_GROUP // 2) + pr) * PAIR,
                                 (kh * (GQA_GROUP // 2) + pr + 1) * PAIR)
                    o_even = o_all[(2 * pr) * BLOCK:(2 * pr + 1) * BLOCK] * inv_den[2 * pr]
                    o_odd = o_all[(2 * pr + 1) * BLOCK:(2 * pr + 2) * BLOCK] * inv_den[2 * pr + 1]
                    out_ref[rows, cols] = jnp.where(pair_lo, o_even, o_odd)

        z = _dot(h_ref[...], win16_ref[:, ATTN_WIDTH:])
        y = out_ref[...] * _silu(z)
        out_ref[...] = x_ref[...] + _dot(y.astype(BF16), wout16_ref[...])

    @pl.when(s == n_w + n_tiles)
    def _():
        x = xs_ref[...]
        h = _rms(x, g_ref[...]).astype(BF16)
        q = _dot(h, win16_ref[:, :ATTN_WIDTH])
        z = _dot(h, win16_ref[:, ATTN_WIDTH:])
        qn = jnp.concatenate(
            [_normed_q_pair(q, pr, qg_ref, pair_lo, HEAD_DIM ** -0.5) for pr in range(N_PAIRS)], axis=-1)
        n_dec = x.shape[0]
        o_rows = []
        for c in range(n_dec // DEC_CHUNK):
            rows = slice(c * DEC_CHUNK, (c + 1) * DEC_CHUNK)
            o_rows.append(_decode_attention(qn[rows], k3_ref[rows], v3_ref[rows],
                                            sinkc_ref[:, layer:layer + 1]))
        y = jnp.concatenate(o_rows, axis=0) * _silu(z)
        outs_ref[...] = x + _dot(y.astype(BF16), wout16_ref[...])


def _attn_layer(x2d, xs, k3, v3, sinks, sinks_t, g, w_in, q_gain, kt_band, v_band, w_out,
                *, layer, seq, tm):
    rows = x2d.shape[0]
    n_dec = xs.shape[0]
    n_tiles = rows // tm
    tiles_per_seq = seq // tm
    nblk = tm // BLOCK
    wc = ATTN_W_CHUNK
    n_w = D_MODEL // wc
    tile = lambda s: (_tile_of(s, n_w, n_tiles), 0)
    band = lambda s: (_tile_of(s, n_w, n_tiles) // tiles_per_seq,
                      _tile_of(s, n_w, n_tiles) % tiles_per_seq)
    w_row = lambda s: (layer, _w_step(s, n_w), 0)
    one = pl.Buffered(1)
    return pl.pallas_call(
        functools.partial(_attn_kernel, layer=layer, tm=tm, tiles_per_seq=tiles_per_seq, n_tiles=n_tiles),
        grid=(n_w + n_tiles + 1,),
        in_specs=[
            pl.BlockSpec(memory_space=pltpu.SMEM),
            pl.BlockSpec((tm, D_MODEL), tile),
            _resident((n_dec, D_MODEL)),
            _resident(k3.shape),
            _resident(v3.shape),
            _resident(sinks_t.shape),
            pl.BlockSpec((None, 1, D_MODEL), lambda s: (layer, 0, 0), pipeline_mode=one),
            pl.BlockSpec((1, wc, 2 * ATTN_WIDTH), w_row),
            pl.BlockSpec((None, 1, ATTN_WIDTH), lambda s: (layer, 0, 0), pipeline_mode=one),
            pl.BlockSpec((1, nblk, N_KV_HEADS, PAIR, 2 * BLOCK), lambda s: band(s) + (0, 0, 0)),
            pl.BlockSpec((1, nblk, 2 * BLOCK, N_KV_HEADS * PAIR), lambda s: band(s) + (0, 0)),
            pl.BlockSpec((1, wc, D_MODEL), w_row),
        ],
        out_specs=[
            pl.BlockSpec((tm, D_MODEL), tile),
            _resident((n_dec, D_MODEL)),
        ],
        out_shape=[
            jax.ShapeDtypeStruct((rows, D_MODEL), F32),
            jax.ShapeDtypeStruct((n_dec, D_MODEL), F32),
        ],
        scratch_shapes=[
            pltpu.VMEM((D_MODEL, 2 * ATTN_WIDTH), BF16),
            pltpu.VMEM((ATTN_WIDTH, D_MODEL), BF16),
            pltpu.VMEM((tm, D_MODEL), BF16),
            pltpu.VMEM((tm, ATTN_WIDTH), BF16),
            pltpu.VMEM((tm, ATTN_WIDTH), BF16),
        ],
        compiler_params=pltpu.CompilerParams(
            dimension_semantics=("arbitrary",), vmem_limit_bytes=VMEM_LIMIT),
        name="attn_layer",
    )(sinks, x2d, xs, k3, v3, sinks_t, g, w_in, q_gain, kt_band, v_band, w_out)


def kernel(x_prompt, x_sample, state_pool, cache_k_win, cache_v_win, norm_a, w_in_a, w_grp_a, scale_a, w_out_a, norm_kv, w_kv, k_norm, norm_b, w_in_b, q_norm, sinks, w_out_b):
    bsz, seq, _ = x_prompt.shape
    n_dec = x_sample.shape[0]
    assert x_sample.shape[1] == 1 and cache_k_win.shape[1] == WINDOW
    tm_a, tm_b, tm_kv = 256, 256, 1024

    hp = x_prompt.reshape(bsz * seq, D_MODEL)
    hs = x_sample[:, 0]
    state_t = state_pool.transpose(0, 2, 1, 3)
    pool_p, pool_s = [], []
    for l in range(N_A_LAYERS):
        hp, hist, hs, new_state = _pool_layer(
            hp, hs, state_t, norm_a[:, None], w_in_a, w_grp_a, scale_a[:, None], w_out_a,
            layer=l, seq=seq, tm=tm_a)
        pool_p.append(hist[:, POOL_HIST - POOL_BUF:])
        pool_s.append(new_state)

    def keys_minor(t):
        return t.transpose(0, 2, 3, 1).reshape(t.shape[0], KV_WIDTH, t.shape[1])

    def keys_major(t):
        return t.reshape(t.shape[0], N_KV_HEADS, HEAD_DIM, t.shape[2]).transpose(0, 3, 1, 2)

    kt_band, v_band, k_win_p, v_win_p, k_win_s, v_win_s, k16, v16 = _shared_kv(
        hp, hs, keys_minor(cache_k_win), keys_minor(cache_v_win),
        norm_kv[None], w_kv, jnp.tile(k_norm, 2)[None], seq=seq, tm=tm_kv)

    q_gain = jnp.tile(q_norm, (1, N_HEADS))[:, None]
    for j in range(N_B_LAYERS):
        hp, hs = _attn_layer(hp, hs, k16, v16, sinks, sinks.T, norm_b[:, None], w_in_b, q_gain,
                             kt_band, v_band, w_out_b, layer=j, seq=seq, tm=tm_b)

    return (hp.reshape(bsz, seq, D_MODEL), hs[:, None],
            jnp.stack(pool_p, axis=0), jnp.stack(pool_s, axis=0).transpose(0, 2, 1, 3),
            keys_major(k_win_p), keys_major(v_win_p), keys_major(k_win_s), keys_major(v_win_s))
```

```python
import functools
import math

import jax
import jax.numpy as jnp
from jax import lax
from jax.experimental import pallas as pl
from jax.experimental.pallas import tpu as pltpu

D_MODEL = 2048
N_A_LAYERS = 2
N_B_LAYERS = 2
POOL_WINDOWS = (2, 4, 8, 16)
N_POOL_GROUPS = len(POOL_WINDOWS)
POOL_WIDTH = D_MODEL
POOL_GROUP = POOL_WIDTH // N_POOL_GROUPS
POOL_BUF = max(POOL_WINDOWS) - 1
POOL_HIST = POOL_BUF + 1
HEAD_DIM = 64
N_HEADS = D_MODEL // HEAD_DIM
N_KV_HEADS = N_HEADS // 8
GQA_GROUP = N_HEADS // N_KV_HEADS
ATTN_WIDTH = N_HEADS * HEAD_DIM
KV_WIDTH = N_KV_HEADS * HEAD_DIM
WINDOW = 128
BLOCK = WINDOW
PAST_LEN = 16384
RMS_EPS = 1e-6
NEG_INF = -1e30
LOG2E = math.log2(math.e)
MIN_SLOPE = 2.0 ** -8
PAIR = 2 * HEAD_DIM
N_PAIRS = N_HEADS // 2
DEC_CHUNK = 8
VMEM_LIMIT = 62 * 1024 * 1024

BF16 = jnp.bfloat16
F32 = jnp.float32


def _rms(x, g):
    return x * lax.rsqrt(jnp.mean(x * x, axis=-1, keepdims=True) + RMS_EPS) * g


def _silu(z):
    return z * jax.nn.sigmoid(z)


def _dot(a, b):
    return jnp.dot(a, b, preferred_element_type=F32)


def _resident(shape):
    return pl.BlockSpec(shape, lambda *_: (0,) * len(shape), pipeline_mode=pl.Buffered(1))


def _tile_of(s, n_tiles):
    return jnp.clip(s - 1, 0, n_tiles - 1)


def _weight_blocks(src, dst16, block_rows, block_cols):
    rows, cols = dst16.shape
    plan = []
    for r in range(0, rows, block_rows):
        for c in range(0, cols, block_cols):
            w = min(block_cols, cols - c)
            plan.append((src.at[pl.ds(r, block_rows), pl.ds(c, w)],
                         dst16.at[pl.ds(r, block_rows), pl.ds(c, w)], (block_rows, w)))
    return plan


def _load_weights_bf16(plan, slots, sem):
    def copy(c):
        src, _, (r, w) = plan[c]
        return pltpu.make_async_copy(src, slots[c % 2].at[pl.ds(0, r), pl.ds(0, w)], sem.at[c % 2])

    copy(0).start()
    for c in range(len(plan)):
        if c + 1 < len(plan):
            copy(c + 1).start()
        copy(c).wait()
        _, dst, (r, w) = plan[c]
        dst[...] = slots[c % 2][0:r, 0:w].astype(BF16)


def _pair_inv_rms(pair, pair_lo, scale):
    sq = pair * pair
    lo = jnp.sum(jnp.where(pair_lo, sq, 0.0), axis=-1, keepdims=True)
    hi = jnp.sum(jnp.where(pair_lo, 0.0, sq), axis=-1, keepdims=True)
    ms = jnp.where(pair_lo, lo, hi) * (1.0 / HEAD_DIM)
    return lax.rsqrt(ms + RMS_EPS) * scale


def _pool_tail(x_ref, u, z, window_sums, inv_cnt, wgrp_ref, scale_ref, wout_ref):
    mixed = []
    for g in range(N_POOL_GROUPS):
        cols = slice(g * POOL_GROUP, (g + 1) * POOL_GROUP)
        p = window_sums[g] * inv_cnt[g] - u[:, cols]
        mixed.append(_dot(p.astype(BF16), wgrp_ref[g]))
    y = jnp.concatenate(mixed, axis=-1) * scale_ref[...] * _silu(z)
    return x_ref[...] + _dot(y.astype(BF16), wout_ref[...])


def _pool_kernel(x_ref, xs_ref, st_ref, g_ref, win_ref, wgrp_ref, scale_ref, wout_ref,
                 out_ref, hist_ref, outs_ref, newst_ref,
                 win16_ref, wgrp16_ref, wout16_ref, uext_ref, sem_ref,
                 *, layer, tm, tiles_per_seq, n_tiles):
    s = pl.program_id(0)

    @pl.when(s == 0)
    def _():
        slots = (out_ref, uext_ref.at[pl.ds(POOL_HIST, tm)])
        plan = (_weight_blocks(win_ref.at[layer], win16_ref, tm, D_MODEL)
                + _weight_blocks(wout_ref.at[layer], wout16_ref, tm, D_MODEL))
        for g in range(N_POOL_GROUPS):
            plan += _weight_blocks(wgrp_ref.at[layer, g], wgrp16_ref.at[g], tm, D_MODEL)
        _load_weights_bf16(plan, slots, sem_ref)

    @pl.when((s >= 1) & (s <= n_tiles))
    def _():
        i = (s - 1) % tiles_per_seq

        @pl.when(i == 0)
        def _():
            uext_ref[0:POOL_HIST, :] = jnp.zeros((POOL_HIST, POOL_WIDTH), F32)

        h = _rms(x_ref[...], g_ref[...]).astype(BF16)
        u = _dot(h, win16_ref[:, :POOL_WIDTH])
        z = _dot(h, win16_ref[:, POOL_WIDTH:])
        uext_ref[POOL_HIST:POOL_HIST + tm, :] = u

        pos = i * tm + lax.broadcasted_iota(jnp.int32, (tm, 1), 0)
        sums, inv_cnt = [], []
        for g, w in enumerate(POOL_WINDOWS):
            cols = slice(g * POOL_GROUP, (g + 1) * POOL_GROUP)
            acc = u[:, cols]
            for k in range(1, w):
                acc = acc + uext_ref[POOL_HIST - k:POOL_HIST - k + tm, cols]
            sums.append(acc)
            inv_cnt.append(1.0 / jnp.minimum(pos + 1, w).astype(F32))
        out_ref[...] = _pool_tail(x_ref, u, z, sums, inv_cnt, wgrp16_ref, scale_ref, wout16_ref)

        tail = uext_ref[tm:tm + POOL_HIST, :]
        hist_ref[0] = tail
        uext_ref[0:POOL_HIST, :] = tail

    @pl.when(s == n_tiles + 1)
    def _():
        h = _rms(xs_ref[...], g_ref[...]).astype(BF16)
        u = _dot(h, win16_ref[:, :POOL_WIDTH])
        z = _dot(h, win16_ref[:, POOL_WIDTH:])
        for r in range(POOL_BUF - 1):
            newst_ref[r] = st_ref[0, r + 1]
        newst_ref[POOL_BUF - 1] = u
        sums, inv_cnt = [], []
        for g, w in enumerate(POOL_WINDOWS):
            cols = slice(g * POOL_GROUP, (g + 1) * POOL_GROUP)
            acc = u[:, cols]
            for k in range(1, w):
                acc = acc + st_ref[0, POOL_BUF - k, :, cols]
            sums.append(acc)
            inv_cnt.append(1.0 / float(min(PAST_LEN + 1, w)))
        outs_ref[...] = _pool_tail(xs_ref, u, z, sums, inv_cnt, wgrp16_ref, scale_ref, wout16_ref)


def _pool_layer(x2d, xs, state, g, w_in, w_grp, scale, w_out, *, layer, seq, tm):
    rows = x2d.shape[0]
    n_dec = xs.shape[0]
    n_tiles = rows // tm
    tiles_per_seq = seq // tm
    tile = lambda s: (_tile_of(s, n_tiles), 0)
    one = pl.Buffered(1)
    hbm = pl.BlockSpec(memory_space=pl.ANY)
    return pl.pallas_call(
        functools.partial(_pool_kernel, layer=layer, tm=tm, tiles_per_seq=tiles_per_seq, n_tiles=n_tiles),
        grid=(n_tiles + 2,),
        in_specs=[
            pl.BlockSpec((tm, D_MODEL), tile),
            _resident((n_dec, D_MODEL)),
            pl.BlockSpec((1, POOL_BUF, n_dec, POOL_WIDTH), lambda s: (layer, 0, 0, 0), pipeline_mode=one),
            pl.BlockSpec((None, 1, D_MODEL), lambda s: (layer, 0, 0), pipeline_mode=one),
            hbm,
            hbm,
            pl.BlockSpec((None, 1, POOL_WIDTH), lambda s: (layer, 0, 0), pipeline_mode=one),
            hbm,
        ],
        out_specs=[
            pl.BlockSpec((tm, D_MODEL), tile),
            pl.BlockSpec((1, POOL_HIST, POOL_WIDTH),
                         lambda s: (_tile_of(s, n_tiles) // tiles_per_seq, 0, 0)),
            _resident((n_dec, D_MODEL)),
            _resident((POOL_BUF, n_dec, POOL_WIDTH)),
        ],
        out_shape=[
            jax.ShapeDtypeStruct((rows, D_MODEL), F32),
            jax.ShapeDtypeStruct((rows // seq, POOL_HIST, POOL_WIDTH), F32),
            jax.ShapeDtypeStruct((n_dec, D_MODEL), F32),
            jax.ShapeDtypeStruct((POOL_BUF, n_dec, POOL_WIDTH), F32),
        ],
        scratch_shapes=[
            pltpu.VMEM((D_MODEL, 2 * POOL_WIDTH), BF16),
            pltpu.VMEM((N_POOL_GROUPS, POOL_GROUP, POOL_GROUP), BF16),
            pltpu.VMEM((POOL_WIDTH, D_MODEL), BF16),
            pltpu.VMEM((POOL_HIST + tm, POOL_WIDTH), F32),
            pltpu.SemaphoreType.DMA((2,)),
        ],
        compiler_params=pltpu.CompilerParams(
            dimension_semantics=("arbitrary",), vmem_limit_bytes=VMEM_LIMIT),
        name="pool_layer",
    )(x2d, xs, state, g, w_in, w_grp, scale, w_out)


def _kv_project(x, g_ref, w16_ref, kg_ref):
    h = _rms(x, g_ref[...]).astype(BF16)
    kv = _dot(h, w16_ref[...])
    pair_lo = lax.broadcasted_iota(jnp.int32, (1, PAIR), 1) < HEAD_DIM
    k_pairs = []
    for p in range(KV_WIDTH // PAIR):
        kp = kv[:, p * PAIR:(p + 1) * PAIR]
        k_pairs.append(kp * _pair_inv_rms(kp, pair_lo, 1.0) * kg_ref[...])
    return k_pairs, kv[:, KV_WIDTH:]


def _dup_heads(col, pair_lo):
    rolled = pltpu.roll(col, HEAD_DIM, 1)
    return jnp.where(pair_lo, col, rolled), jnp.where(pair_lo, rolled, col)


def _slide_window(cache_ref, new_t, win_ref, win16_ref):
    n, _, n_keys = cache_ref.shape
    last = lax.broadcasted_iota(jnp.int32, (1, n_keys), 1) == n_keys - 1
    for b in range(n):
        slid = jnp.where(last, new_t[:, b:b + 1], pltpu.roll(cache_ref[b], n_keys - 1, 1))
        win_ref[b] = slid
        win16_ref[b] = slid.astype(BF16)


def _kv_kernel(x_ref, xs_ref, ck_ref, cv_ref, g_ref, wkv_ref, kg_ref,
               kt_ref, vb_ref, kwin_ref, vwin_ref, kws_ref, vws_ref, k16_ref, v16_ref,
               w16_ref, pk_ref, pv_ref, *, tm, tiles_per_seq, n_tiles):
    s = pl.program_id(0)
    nblk = tm // BLOCK
    pair_lo = lax.broadcasted_iota(jnp.int32, (1, PAIR), 1) < HEAD_DIM

    @pl.when(s == 0)
    def _():
        w16_ref[...] = wkv_ref[...].astype(BF16)

    @pl.when(s < n_tiles)
    def _():
        @pl.when(s % tiles_per_seq == 0)
        def _():
            pk_ref[...] = jnp.zeros(pk_ref.shape, BF16)
            pv_ref[...] = jnp.zeros(pv_ref.shape, BF16)

        k_pairs, v = _kv_project(x_ref[...], g_ref, w16_ref, kg_ref)
        for jb in range(nblk):
            rows = slice(jb * BLOCK, (jb + 1) * BLOCK)
            for p in range(KV_WIDTH // PAIR):
                k_dup = _dup_heads(k_pairs[p][rows], pair_lo)
                v_dup = _dup_heads(v[rows, p * PAIR:(p + 1) * PAIR], pair_lo)
                for half in range(2):
                    kh = 2 * p + half
                    cols = slice(kh * PAIR, (kh + 1) * PAIR)
                    kt_f32 = k_dup[half].T
                    kt_cur = kt_f32.astype(BF16)
                    v_cur = v_dup[half].astype(BF16)
                    if jb == nblk - 1:
                        hd = slice(kh * HEAD_DIM, (kh + 1) * HEAD_DIM)
                        kwin_ref[0, hd, :] = kt_f32[:HEAD_DIM]
                        vwin_ref[0, hd, :] = v_dup[half].T[:HEAD_DIM]
                    kt_ref[0, jb, kh, :, 0:BLOCK] = pk_ref[kh]
                    kt_ref[0, jb, kh, :, BLOCK:2 * BLOCK] = kt_cur
                    vb_ref[0, jb, 0:BLOCK, cols] = pv_ref[:, cols]
                    vb_ref[0, jb, BLOCK:2 * BLOCK, cols] = v_cur
                    pk_ref[kh] = kt_cur
                    pv_ref[:, cols] = v_cur

    @pl.when(s == n_tiles)
    def _():
        k_pairs, v = _kv_project(xs_ref[...], g_ref, w16_ref, kg_ref)
        n_dec = xs_ref.shape[0]
        pad = jnp.zeros((BLOCK - n_dec, KV_WIDTH), F32)
        kn_t = jnp.concatenate([jnp.concatenate(k_pairs, axis=1), pad], axis=0)
        vn_t = jnp.concatenate([v, pad], axis=0)
        kn_t = jnp.concatenate([kn_t[:, c * BLOCK:(c + 1) * BLOCK].T for c in range(KV_WIDTH // BLOCK)], axis=0)
        vn_t = jnp.concatenate([vn_t[:, c * BLOCK:(c + 1) * BLOCK].T for c in range(KV_WIDTH // BLOCK)], axis=0)
        _slide_window(ck_ref, kn_t, kws_ref, k16_ref)
        _slide_window(cv_ref, vn_t, vws_ref, v16_ref)


def _shared_kv(x2d, xs, cache_kt, cache_vt, g, w_kv, k_gain_pair, *, seq, tm):
    rows = x2d.shape[0]
    bsz = rows // seq
    n_dec = xs.shape[0]
    n_tiles = rows // tm
    tiles_per_seq = seq // tm
    nblk = tm // BLOCK
    t = lambda s: jnp.minimum(s, n_tiles - 1)
    return pl.pallas_call(
        functools.partial(_kv_kernel, tm=tm, tiles_per_seq=tiles_per_seq, n_tiles=n_tiles),
        grid=(n_tiles + 1,),
        in_specs=[
            pl.BlockSpec((tm, D_MODEL), lambda s: (t(s), 0)),
            _resident((n_dec, D_MODEL)),
            _resident(cache_kt.shape),
            _resident(cache_vt.shape),
            _resident((1, D_MODEL)),
            _resident((D_MODEL, 2 * KV_WIDTH)),
            _resident((1, PAIR)),
        ],
        out_specs=[
            pl.BlockSpec((1, nblk, N_KV_HEADS, PAIR, 2 * BLOCK),
                         lambda s: (t(s) // tiles_per_seq, t(s) % tiles_per_seq, 0, 0, 0)),
            pl.BlockSpec((1, nblk, 2 * BLOCK, N_KV_HEADS * PAIR),
                         lambda s: (t(s) // tiles_per_seq, t(s) % tiles_per_seq, 0, 0)),
            pl.BlockSpec((1, KV_WIDTH, BLOCK), lambda s: (t(s) // tiles_per_seq, 0, 0)),
            pl.BlockSpec((1, KV_WIDTH, BLOCK), lambda s: (t(s) // tiles_per_seq, 0, 0)),
            _resident(cache_kt.shape),
            _resident(cache_vt.shape),
            _resident(cache_kt.shape),
            _resident(cache_vt.shape),
        ],
        out_shape=[
            jax.ShapeDtypeStruct((bsz, seq // BLOCK, N_KV_HEADS, PAIR, 2 * BLOCK), BF16),
            jax.ShapeDtypeStruct((bsz, seq // BLOCK, 2 * BLOCK, N_KV_HEADS * PAIR), BF16),
            jax.ShapeDtypeStruct((bsz, KV_WIDTH, BLOCK), F32),
            jax.ShapeDtypeStruct((bsz, KV_WIDTH, BLOCK), F32),
            jax.ShapeDtypeStruct(cache_kt.shape, F32),
            jax.ShapeDtypeStruct(cache_vt.shape, F32),
            jax.ShapeDtypeStruct(cache_kt.shape, BF16),
            jax.ShapeDtypeStruct(cache_vt.shape, BF16),
        ],
        scratch_shapes=[
            pltpu.VMEM((D_MODEL, 2 * KV_WIDTH), BF16),
            pltpu.VMEM((N_KV_HEADS, PAIR, BLOCK), BF16),
            pltpu.VMEM((BLOCK, N_KV_HEADS * PAIR), BF16),
        ],
        compiler_params=pltpu.CompilerParams(
            dimension_semantics=("arbitrary",), vmem_limit_bytes=VMEM_LIMIT),
        name="shared_kv",
    )(x2d, xs, cache_kt, cache_vt, g, w_kv, k_gain_pair)


def _normed_q_pair(q, pr, qg_ref, pair_lo, scale):
    cols = slice(pr * PAIR, (pr + 1) * PAIR)
    qp = q[:, cols]
    return qp * _pair_inv_rms(qp, pair_lo, scale) * qg_ref[:, cols]


def _decode_attention(q, k3, v3, sink_col):
    n, n_keys = k3.shape[0], k3.shape[2]
    rows = n * N_HEADS
    ri = lax.broadcasted_iota(jnp.int32, (rows, n), 0)
    ci = lax.broadcasted_iota(jnp.int32, (rows, n), 1)
    rep = jnp.where(ri // N_HEADS == ci, 1.0, 0.0).astype(BF16)
    row_h = lax.broadcasted_iota(jnp.int32, (rows, ATTN_WIDTH), 0) % N_HEADS
    col_h = lax.broadcasted_iota(jnp.int32, (rows, ATTN_WIDTH), 1) // HEAD_DIM
    own = row_h == col_h
    fj = lax.broadcasted_iota(jnp.int32, (ATTN_WIDTH, KV_WIDTH), 0)
    fc = lax.broadcasted_iota(jnp.int32, (ATTN_WIDTH, KV_WIDTH), 1)
    to_kv = jnp.where((fj % HEAD_DIM == fc % HEAD_DIM) & ((fj // HEAD_DIM) // GQA_GROUP == fc // HEAD_DIM),
                      1.0, 0.0).astype(BF16)
    tc = lax.broadcasted_iota(jnp.int32, (KV_WIDTH, ATTN_WIDTH), 0)
    tj = lax.broadcasted_iota(jnp.int32, (KV_WIDTH, ATTN_WIDTH), 1)
    from_kv = jnp.where((tj % HEAD_DIM == tc % HEAD_DIM) & ((tj // HEAD_DIM) // GQA_GROUP == tc // HEAD_DIM),
                        1.0, 0.0).astype(BF16)

    m1 = jnp.where(own, _dot(rep, q.astype(BF16)), 0.0).astype(BF16)
    qm3 = _dot(m1, to_kv).astype(BF16).reshape(n, N_HEADS, KV_WIDTH)
    s = jnp.einsum("bhc,bck->bhk", qm3, k3, preferred_element_type=F32)
    h_col = lax.broadcasted_iota(jnp.int32, (N_HEADS, 1), 0).astype(F32)
    slope = jnp.exp2(-8.0 * (h_col + 1.0) / N_HEADS)[None]
    dist = (n_keys - 1 - lax.broadcasted_iota(jnp.int32, (1, n_keys), 1)).astype(F32)[None]
    s = s - slope * dist
    sink = sink_col[None]
    m = jnp.maximum(jnp.max(s, axis=-1, keepdims=True), sink)
    p = jnp.exp(s - m)
    den = jnp.sum(p, axis=-1, keepdims=True) + jnp.exp(sink - m)
    o3 = jnp.einsum("bhk,bck->bhc", p.astype(BF16), v3, preferred_element_type=F32) / den
    o2 = o3.reshape(rows, KV_WIDTH)
    hi = o2.astype(BF16)
    lo = (o2 - hi.astype(F32)).astype(BF16)
    spread = jnp.where(own, _dot(hi, from_kv) + _dot(lo, from_kv), 0.0)
    return jnp.sum(spread.reshape(n, N_HEADS, ATTN_WIDTH), axis=1)


def _attn_kernel(sink_ref, x_ref, xs_ref, k3_ref, v3_ref, sinkc_ref, g_ref, win_ref, qg_ref,
                 kt_ref, vb_ref, wout_ref,
                 out_ref, outs_ref,
                 win16_ref, wout16_ref, h_ref, qlo_ref, qhi_ref, stage_ref, sem_ref,
                 *, layer, tm, tiles_per_seq, n_tiles):
    s = pl.program_id(0)
    nblk = tm // BLOCK
    pair_lo = lax.broadcasted_iota(jnp.int32, (1, PAIR), 1) < HEAD_DIM

    @pl.when(s == 0)
    def _():
        plan = (_weight_blocks(win_ref.at[layer], win16_ref, tm, D_MODEL)
                + _weight_blocks(wout_ref.at[layer], wout16_ref, tm, D_MODEL))
        _load_weights_bf16(plan, (out_ref, stage_ref), sem_ref)

    @pl.when((s >= 1) & (s <= n_tiles))
    def _():
        i = (s - 1) % tiles_per_seq
        h_ref[...] = _rms(x_ref[...], g_ref[...]).astype(BF16)
        for kh in range(N_KV_HEADS):
            gcols = slice(kh * GQA_GROUP * HEAD_DIM, (kh + 1) * GQA_GROUP * HEAD_DIM)
            q = _dot(h_ref[...], win16_ref[:, gcols])
            for pr in range(GQA_GROUP // 2):
                cols = slice(pr * PAIR, (pr + 1) * PAIR)
                ocols = slice((kh * (GQA_GROUP // 2) + pr) * PAIR, (kh * (GQA_GROUP // 2) + pr + 1) * PAIR)
                qp = q[:, cols]
                qn = qp * _pair_inv_rms(qp, pair_lo, HEAD_DIM ** -0.5 * LOG2E) * qg_ref[:, ocols]
                qlo_ref[:, ocols] = jnp.where(pair_lo, qn, 0.0).astype(BF16)
                qhi_ref[:, ocols] = jnp.where(pair_lo, 0.0, qn).astype(BF16)

        r_idx = lax.broadcasted_iota(jnp.int32, (BLOCK, 2 * BLOCK), 0)
        c_idx = lax.broadcasted_iota(jnp.int32, (BLOCK, 2 * BLOCK), 1)
        dist = r_idx + BLOCK - c_idx
        in_window = (dist >= 0) & (dist < WINDOW)
        neg_dist = -dist.astype(F32)

        for blk in range(nblk):
            rows = slice(blk * BLOCK, (blk + 1) * BLOCK)
            first_key = (i * nblk + blk - 1) * BLOCK
            valid = in_window & (first_key + c_idx >= 0)
            nd = jnp.where(valid, neg_dist, NEG_INF / MIN_SLOPE)
            for kh in range(N_KV_HEADS):
                kt = kt_ref[0, blk, kh]
                vdup = vb_ref[0, blk, :, kh * PAIR:(kh + 1) * PAIR]
                pieces = []
                for pr in range(GQA_GROUP // 2):
                    cols = slice((kh * (GQA_GROUP // 2) + pr) * PAIR,
                                 (kh * (GQA_GROUP // 2) + pr + 1) * PAIR)
                    pieces.append(qlo_ref[rows, cols])
                    pieces.append(qhi_ref[rows, cols])
                s_all = _dot(jnp.concatenate(pieces, axis=0), kt)
                probs, inv_den = [], []
                for g_idx in range(GQA_GROUP):
                    head = kh * GQA_GROUP + g_idx
                    slope2 = 2.0 ** (-8.0 * (head + 1) / N_HEADS) * LOG2E
                    sc = s_all[g_idx * BLOCK:(g_idx + 1) * BLOCK] + slope2 * nd
                    sink2 = sink_ref[layer, head] * LOG2E
                    m = jnp.maximum(jnp.max(sc, axis=-1, keepdims=True), sink2)
                    p = jnp.exp2(sc - m)
                    den = jnp.sum(p, axis=-1, keepdims=True) + jnp.exp2(sink2 - m)
                    probs.append(p.astype(BF16))
                    inv_den.append(1.0 / den)
                o_all = _dot(jnp.concatenate(probs, axis=0), vdup)
                for pr in range(GQA_GROUP // 2):
                    cols = slice((kh * (GQA_GROUP // 2) + pr) * PAIR,
                                 (kh * (GQA_GROUP // 2) + pr + 1) * PAIR)
                    o_even = o_all[(2 * pr) * BLOCK:(2 * pr + 1) * BLOCK] * inv_den[2 * pr]
                    o_odd = o_all[(2 * pr + 1) * BLOCK:(2 * pr + 2) * BLOCK] * inv_den[2 * pr + 1]
                    out_ref[rows, cols] = jnp.where(pair_lo, o_even, o_odd)

        z = _dot(h_ref[...], win16_ref[:, ATTN_WIDTH:])
        y = out_ref[...] * _silu(z)
        out_ref[...] = x_ref[...] + _dot(y.astype(BF16), wout16_ref[...])

    @pl.when(s == n_tiles + 1)
    def _():
        x = xs_ref[...]
        h = _rms(x, g_ref[...]).astype(BF16)
        q = _dot(h, win16_ref[:, :ATTN_WIDTH])
        z = _dot(h, win16_ref[:, ATTN_WIDTH:])
        qn = jnp.concatenate(
            [_normed_q_pair(q, pr, qg_ref, pair_lo, HEAD_DIM ** -0.5) for pr in range(N_PAIRS)], axis=-1)
        n_dec = x.shape[0]
        o_rows = []
        for c in range(n_dec // DEC_CHUNK):
            rows = slice(c * DEC_CHUNK, (c + 1) * DEC_CHUNK)
            o_rows.append(_decode_attention(qn[rows], k3_ref[rows], v3_ref[rows],
                                            sinkc_ref[:, layer:layer + 1]))
        y = jnp.concatenate(o_rows, axis=0) * _silu(z)
        outs_ref[...] = x + _dot(y.astype(BF16), wout16_ref[...])


def _attn_layer(x2d, xs, k3, v3, sinks, sinks_t, g, w_in, q_gain, kt_band, v_band, w_out,
                *, layer, seq, tm):
    rows = x2d.shape[0]
    n_dec = xs.shape[0]
    n_tiles = rows // tm
    tiles_per_seq = seq // tm
    nblk = tm // BLOCK
    tile = lambda s: (_tile_of(s, n_tiles), 0)
    band = lambda s: (_tile_of(s, n_tiles) // tiles_per_seq, _tile_of(s, n_tiles) % tiles_per_seq)
    one = pl.Buffered(1)
    hbm = pl.BlockSpec(memory_space=pl.ANY)
    return pl.pallas_call(
        functools.partial(_attn_kernel, layer=layer, tm=tm, tiles_per_seq=tiles_per_seq, n_tiles=n_tiles),
        grid=(n_tiles + 2,),
        in_specs=[
            pl.BlockSpec(memory_space=pltpu.SMEM),
            pl.BlockSpec((tm, D_MODEL), tile),
            _resident((n_dec, D_MODEL)),
            _resident(k3.shape),
            _resident(v3.shape),
            _resident(sinks_t.shape),
            pl.BlockSpec((None, 1, D_MODEL), lambda s: (layer, 0, 0), pipeline_mode=one),
            hbm,
            pl.BlockSpec((None, 1, ATTN_WIDTH), lambda s: (layer, 0, 0), pipeline_mode=one),
            pl.BlockSpec((1, nblk, N_KV_HEADS, PAIR, 2 * BLOCK), lambda s: band(s) + (0, 0, 0)),
            pl.BlockSpec((1, nblk, 2 * BLOCK, N_KV_HEADS * PAIR), lambda s: band(s) + (0, 0)),
            hbm,
        ],
        out_specs=[
            pl.BlockSpec((tm, D_MODEL), tile),
            _resident((n_dec, D_MODEL)),
        ],
        out_shape=[
            jax.ShapeDtypeStruct((rows, D_MODEL), F32),
            jax.ShapeDtypeStruct((n_dec, D_MODEL), F32),
        ],
        scratch_shapes=[
            pltpu.VMEM((D_MODEL, 2 * ATTN_WIDTH), BF16),
            pltpu.VMEM((ATTN_WIDTH, D_MODEL), BF16),
            pltpu.VMEM((tm, D_MODEL), BF16),
            pltpu.VMEM((tm, ATTN_WIDTH), BF16),
            pltpu.VMEM((tm, ATTN_WIDTH), BF16),
            pltpu.VMEM((tm, D_MODEL), F32),
            pltpu.SemaphoreType.DMA((2,)),
        ],
        compiler_params=pltpu.CompilerParams(
            dimension_semantics=("arbitrary",), vmem_limit_bytes=VMEM_LIMIT),
        name="attn_layer",
    )(sinks, x2d, xs, k3, v3, sinks_t, g, w_in, q_gain, kt_band, v_band, w_out)


def kernel(x_prompt, x_sample, state_pool, cache_k_win, cache_v_win, norm_a, w_in_a, w_grp_a, scale_a, w_out_a, norm_kv, w_kv, k_norm, norm_b, w_in_b, q_norm, sinks, w_out_b):
    bsz, seq, _ = x_prompt.shape
    n_dec = x_sample.shape[0]
    assert x_sample.shape[1] == 1 and cache_k_win.shape[1] == WINDOW
    tm_a, tm_b, tm_kv = 256, 256, 1024

    hp = x_prompt.reshape(bsz * seq, D_MODEL)
    hs = x_sample[:, 0]
    state_t = state_pool.transpose(0, 2, 1, 3)
    pool_p, pool_s = [], []
    for l in range(N_A_LAYERS):
        hp, hist, hs, new_state = _pool_layer(
            hp, hs, state_t, norm_a[:, None], w_in_a, w_grp_a, scale_a[:, None], w_out_a,
            layer=l, seq=seq, tm=tm_a)
        pool_p.append(hist[:, POOL_HIST - POOL_BUF:])
        pool_s.append(new_state)

    def keys_minor(t):
        return t.transpose(0, 2, 3, 1).reshape(t.shape[0], KV_WIDTH, t.shape[1])

    def keys_major(t):
        return t.reshape(t.shape[0], N_KV_HEADS, HEAD_DIM, t.shape[2]).transpose(0, 3, 1, 2)

    kt_band, v_band, k_win_p, v_win_p, k_win_s, v_win_s, k16, v16 = _shared_kv(
        hp, hs, keys_minor(cache_k_win), keys_minor(cache_v_win),
        norm_kv[None], w_kv, jnp.tile(k_norm, 2)[None], seq=seq, tm=tm_kv)

    q_gain = jnp.tile(q_norm, (1, N_HEADS))[:, None]
    for j in range(N_B_LAYERS):
        hp, hs = _attn_layer(hp, hs, k16, v16, sinks, sinks.T, norm_b[:, None], w_in_b, q_gain,
                             kt_band, v_band, w_out_b, layer=j, seq=seq, tm=tm_b)

    return (hp.reshape(bsz, seq, D_MODEL), hs[:, None],
            jnp.stack(pool_p, axis=0), jnp.stack(pool_s, axis=0).transpose(0, 2, 1, 3),
            keys_major(k_win_p), keys_major(v_win_p), keys_major(k_win_s), keys_major(v_win_s))
```

```python
import functools
import math

import jax
import jax.numpy as jnp
from jax import lax
from jax.experimental import pallas as pl
from jax.experimental.pallas import tpu as pltpu

D_MODEL = 2048
N_A_LAYERS = 2
N_B_LAYERS = 2
POOL_WINDOWS = (2, 4, 8, 16)
N_POOL_GROUPS = len(POOL_WINDOWS)
POOL_WIDTH = D_MODEL
POOL_GROUP = POOL_WIDTH // N_POOL_GROUPS
POOL_BUF = max(POOL_WINDOWS) - 1
POOL_HIST = POOL_BUF + 1
HEAD_DIM = 64
N_HEADS = D_MODEL // HEAD_DIM
N_KV_HEADS = N_HEADS // 8
GQA_GROUP = N_HEADS // N_KV_HEADS
ATTN_WIDTH = N_HEADS * HEAD_DIM
KV_WIDTH = N_KV_HEADS * HEAD_DIM
WINDOW = 128
BLOCK = WINDOW
PAST_LEN = 16384
RMS_EPS = 1e-6
NEG_INF = -1e30
LOG2E = math.log2(math.e)
MIN_SLOPE = 2.0 ** -8
PAIR = 2 * HEAD_DIM
N_PAIRS = N_HEADS // 2
DEC_CHUNK = 8
VMEM_LIMIT = 62 * 1024 * 1024

BF16 = jnp.bfloat16
F32 = jnp.float32


def _rms(x, g):
    return x * lax.rsqrt(jnp.mean(x * x, axis=-1, keepdims=True) + RMS_EPS) * g


def _silu(z):
    return z * jax.nn.sigmoid(z)


def _dot(a, b):
    return jnp.dot(a, b, preferred_element_type=F32)


def _resident(shape):
    return pl.BlockSpec(shape, lambda *_: (0,) * len(shape), pipeline_mode=pl.Buffered(1))


def _tile_of(s, n_tiles):
    return jnp.clip(s - 1, 0, n_tiles - 1)


def _weight_blocks(src, dst16, block_rows, block_cols):
    rows, cols = dst16.shape
    plan = []
    for r in range(0, rows, block_rows):
        for c in range(0, cols, block_cols):
            w = min(block_cols, cols - c)
            plan.append((src.at[pl.ds(r, block_rows), pl.ds(c, w)],
                         dst16.at[pl.ds(r, block_rows), pl.ds(c, w)], (block_rows, w)))
    return plan


def _load_weights_bf16(plan, slots, sem):
    n_slots = len(slots)
    in_flight = n_slots - 1

    def copy(c):
        src, _, (r, w) = plan[c]
        k = c % n_slots
        return pltpu.make_async_copy(src, slots[k].at[pl.ds(0, r), pl.ds(0, w)], sem.at[k])

    for c in range(min(in_flight, len(plan))):
        copy(c).start()
    for c in range(len(plan)):
        if c + in_flight < len(plan):
            copy(c + in_flight).start()
        copy(c).wait()
        _, dst, (r, w) = plan[c]
        dst[...] = slots[c % n_slots][0:r, 0:w].astype(BF16)


def _half_tiles(*refs):
    slots = []
    for ref in refs:
        half = ref.shape[0] // 2
        slots += [ref.at[pl.ds(0, half)], ref.at[pl.ds(half, half)]]
    return slots


def _pair_inv_rms(pair, pair_lo, scale):
    sq = pair * pair
    lo = jnp.sum(jnp.where(pair_lo, sq, 0.0), axis=-1, keepdims=True)
    hi = jnp.sum(jnp.where(pair_lo, 0.0, sq), axis=-1, keepdims=True)
    ms = jnp.where(pair_lo, lo, hi) * (1.0 / HEAD_DIM)
    return lax.rsqrt(ms + RMS_EPS) * scale


def _pool_tail(x_ref, u, z, window_sums, inv_cnt, wgrp_ref, scale_ref, wout_ref):
    mixed = []
    for g in range(N_POOL_GROUPS):
        cols = slice(g * POOL_GROUP, (g + 1) * POOL_GROUP)
        p = window_sums[g] * inv_cnt[g] - u[:, cols]
        mixed.append(_dot(p.astype(BF16), wgrp_ref[g]))
    y = jnp.concatenate(mixed, axis=-1) * scale_ref[...] * _silu(z)
    return x_ref[...] + _dot(y.astype(BF16), wout_ref[...])


def _pool_kernel(x_ref, xs_ref, st_ref, g_ref, win_ref, wgrp_ref, scale_ref, wout_ref,
                 out_ref, hist_ref, outs_ref, newst_ref,
                 win16_ref, wgrp16_ref, wout16_ref, uext_ref, sem_ref,
                 *, layer, tm, tiles_per_seq, n_tiles):
    s = pl.program_id(0)

    @pl.when(s == 0)
    def _():
        slots = _half_tiles(out_ref, uext_ref.at[pl.ds(POOL_HIST, tm)])
        rows = tm // 2
        plan = (_weight_blocks(win_ref.at[layer], win16_ref, rows, D_MODEL)
                + _weight_blocks(wout_ref.at[layer], wout16_ref, rows, D_MODEL))
        for g in range(N_POOL_GROUPS):
            plan += _weight_blocks(wgrp_ref.at[layer, g], wgrp16_ref.at[g], rows, D_MODEL)
        _load_weights_bf16(plan, slots, sem_ref)

    @pl.when((s >= 1) & (s <= n_tiles))
    def _():
        i = (s - 1) % tiles_per_seq

        @pl.when(i == 0)
        def _():
            uext_ref[0:POOL_HIST, :] = jnp.zeros((POOL_HIST, POOL_WIDTH), F32)

        x = x_ref[...]
        xg = (x * g_ref[...]).astype(BF16)
        inv_rms = lax.rsqrt(jnp.mean(x * x, axis=-1, keepdims=True) + RMS_EPS)
        u = _dot(xg, win16_ref[:, :POOL_WIDTH]) * inv_rms
        z = _dot(xg, win16_ref[:, POOL_WIDTH:]) * inv_rms
        uext_ref[POOL_HIST:POOL_HIST + tm, :] = u

        pos = i * tm + lax.broadcasted_iota(jnp.int32, (tm, 1), 0)
        sums, inv_cnt = [], []
        for g, w in enumerate(POOL_WINDOWS):
            cols = slice(g * POOL_GROUP, (g + 1) * POOL_GROUP)
            acc = u[:, cols]
            for k in range(1, w):
                acc = acc + uext_ref[POOL_HIST - k:POOL_HIST - k + tm, cols]
            sums.append(acc)
            inv_cnt.append(1.0 / jnp.minimum(pos + 1, w).astype(F32))
        out_ref[...] = _pool_tail(x_ref, u, z, sums, inv_cnt, wgrp16_ref, scale_ref, wout16_ref)

        tail = uext_ref[tm:tm + POOL_HIST, :]
        hist_ref[0] = tail
        uext_ref[0:POOL_HIST, :] = tail

    @pl.when(s == n_tiles + 1)
    def _():
        h = _rms(xs_ref[...], g_ref[...]).astype(BF16)
        u = _dot(h, win16_ref[:, :POOL_WIDTH])
        z = _dot(h, win16_ref[:, POOL_WIDTH:])
        for r in range(POOL_BUF - 1):
            newst_ref[r] = st_ref[0, r + 1]
        newst_ref[POOL_BUF - 1] = u
        sums, inv_cnt = [], []
        for g, w in enumerate(POOL_WINDOWS):
            cols = slice(g * POOL_GROUP, (g + 1) * POOL_GROUP)
            acc = u[:, cols]
            for k in range(1, w):
                acc = acc + st_ref[0, POOL_BUF - k, :, cols]
            sums.append(acc)
            inv_cnt.append(1.0 / float(min(PAST_LEN + 1, w)))
        outs_ref[...] = _pool_tail(xs_ref, u, z, sums, inv_cnt, wgrp16_ref, scale_ref, wout16_ref)


def _pool_layer(x2d, xs, state, g, w_in, w_grp, scale, w_out, *, layer, seq, tm):
    rows = x2d.shape[0]
    n_dec = xs.shape[0]
    n_tiles = rows // tm
    tiles_per_seq = seq // tm
    tile = lambda s: (_tile_of(s, n_tiles), 0)
    one = pl.Buffered(1)
    hbm = pl.BlockSpec(memory_space=pl.ANY)
    return pl.pallas_call(
        functools.partial(_pool_kernel, layer=layer, tm=tm, tiles_per_seq=tiles_per_seq, n_tiles=n_tiles),
        grid=(n_tiles + 2,),
        in_specs=[
            pl.BlockSpec((tm, D_MODEL), tile),
            _resident((n_dec, D_MODEL)),
            pl.BlockSpec((1, POOL_BUF, n_dec, POOL_WIDTH), lambda s: (layer, 0, 0, 0), pipeline_mode=one),
            pl.BlockSpec((None, 1, D_MODEL), lambda s: (layer, 0, 0), pipeline_mode=one),
            hbm,
            hbm,
            pl.BlockSpec((None, 1, POOL_WIDTH), lambda s: (layer, 0, 0), pipeline_mode=one),
            hbm,
        ],
        out_specs=[
            pl.BlockSpec((tm, D_MODEL), tile),
            pl.BlockSpec((1, POOL_HIST, POOL_WIDTH),
                         lambda s: (_tile_of(s, n_tiles) // tiles_per_seq, 0, 0)),
            _resident((n_dec, D_MODEL)),
            _resident((POOL_BUF, n_dec, POOL_WIDTH)),
        ],
        out_shape=[
            jax.ShapeDtypeStruct((rows, D_MODEL), F32),
            jax.ShapeDtypeStruct((rows // seq, POOL_HIST, POOL_WIDTH), F32),
            jax.ShapeDtypeStruct((n_dec, D_MODEL), F32),
            jax.ShapeDtypeStruct((POOL_BUF, n_dec, POOL_WIDTH), F32),
        ],
        scratch_shapes=[
            pltpu.VMEM((D_MODEL, 2 * POOL_WIDTH), BF16),
            pltpu.VMEM((N_POOL_GROUPS, POOL_GROUP, POOL_GROUP), BF16),
            pltpu.VMEM((POOL_WIDTH, D_MODEL), BF16),
            pltpu.VMEM((POOL_HIST + tm, POOL_WIDTH), F32),
            pltpu.SemaphoreType.DMA((4,)),
        ],
        compiler_params=pltpu.CompilerParams(
            dimension_semantics=("arbitrary",), vmem_limit_bytes=VMEM_LIMIT),
        name="pool_layer",
    )(x2d, xs, state, g, w_in, w_grp, scale, w_out)


def _kv_project(x, g_ref, w16_ref, kg_ref):
    h = _rms(x, g_ref[...]).astype(BF16)
    kv = _dot(h, w16_ref[...])
    pair_lo = lax.broadcasted_iota(jnp.int32, (1, PAIR), 1) < HEAD_DIM
    k_pairs = []
    for p in range(KV_WIDTH // PAIR):
        kp = kv[:, p * PAIR:(p + 1) * PAIR]
        k_pairs.append(kp * _pair_inv_rms(kp, pair_lo, 1.0) * kg_ref[...])
    return k_pairs, kv[:, KV_WIDTH:]


def _dup_heads(col, pair_lo):
    rolled = pltpu.roll(col, HEAD_DIM, 1)
    return jnp.where(pair_lo, col, rolled), jnp.where(pair_lo, rolled, col)


def _slide_window(cache_ref, new_t, win_ref, win16_ref):
    n, _, n_keys = cache_ref.shape
    last = lax.broadcasted_iota(jnp.int32, (1, n_keys), 1) == n_keys - 1
    for b in range(n):
        slid = jnp.where(last, new_t[:, b:b + 1], pltpu.roll(cache_ref[b], n_keys - 1, 1))
        win_ref[b] = slid
        win16_ref[b] = slid.astype(BF16)


def _kv_kernel(x_ref, xs_ref, ck_ref, cv_ref, g_ref, wkv_ref, kg_ref,
               kt_ref, vb_ref, kwin_ref, vwin_ref, kws_ref, vws_ref, k16_ref, v16_ref,
               w16_ref, pk_ref, pv_ref, *, tm, tiles_per_seq, n_tiles):
    s = pl.program_id(0)
    nblk = tm // BLOCK
    pair_lo = lax.broadcasted_iota(jnp.int32, (1, PAIR), 1) < HEAD_DIM

    @pl.when(s == 0)
    def _():
        w16_ref[...] = wkv_ref[...].astype(BF16)

    @pl.when(s < n_tiles)
    def _():
        @pl.when(s % tiles_per_seq == 0)
        def _():
            pk_ref[...] = jnp.zeros(pk_ref.shape, BF16)
            pv_ref[...] = jnp.zeros(pv_ref.shape, BF16)

        k_pairs, v = _kv_project(x_ref[...], g_ref, w16_ref, kg_ref)
        for jb in range(nblk):
            rows = slice(jb * BLOCK, (jb + 1) * BLOCK)
            for p in range(KV_WIDTH // PAIR):
                k_dup = _dup_heads(k_pairs[p][rows], pair_lo)
                v_dup = _dup_heads(v[rows, p * PAIR:(p + 1) * PAIR], pair_lo)
                for half in range(2):
                    kh = 2 * p + half
                    cols = slice(kh * PAIR, (kh + 1) * PAIR)
                    kt_f32 = k_dup[half].T
                    kt_cur = kt_f32.astype(BF16)
                    v_cur = v_dup[half].astype(BF16)
                    if jb == nblk - 1:
                        hd = slice(kh * HEAD_DIM, (kh + 1) * HEAD_DIM)
                        kwin_ref[0, hd, :] = kt_f32[:HEAD_DIM]
                        vwin_ref[0, hd, :] = v_dup[half].T[:HEAD_DIM]
                    kt_ref[0, jb, kh, :, 0:BLOCK] = pk_ref[kh]
                    kt_ref[0, jb, kh, :, BLOCK:2 * BLOCK] = kt_cur
                    vb_ref[0, jb, 0:BLOCK, cols] = pv_ref[:, cols]
                    vb_ref[0, jb, BLOCK:2 * BLOCK, cols] = v_cur
                    pk_ref[kh] = kt_cur
                    pv_ref[:, cols] = v_cur

    @pl.when(s == n_tiles)
    def _():
        k_pairs, v = _kv_project(xs_ref[...], g_ref, w16_ref, kg_ref)
        n_dec = xs_ref.shape[0]
        pad = jnp.zeros((BLOCK - n_dec, KV_WIDTH), F32)
        kn_t = jnp.concatenate([jnp.concatenate(k_pairs, axis=1), pad], axis=0)
        vn_t = jnp.concatenate([v, pad], axis=0)
        kn_t = jnp.concatenate([kn_t[:, c * BLOCK:(c + 1) * BLOCK].T for c in range(KV_WIDTH // BLOCK)], axis=0)
        vn_t = jnp.concatenate([vn_t[:, c * BLOCK:(c + 1) * BLOCK].T for c in range(KV_WIDTH // BLOCK)], axis=0)
        _slide_window(ck_ref, kn_t, kws_ref, k16_ref)
        _slide_window(cv_ref, vn_t, vws_ref, v16_ref)


def _shared_kv(x2d, xs, cache_kt, cache_vt, g, w_kv, k_gain_pair, *, seq, tm):
    rows = x2d.shape[0]
    bsz = rows // seq
    n_dec = xs.shape[0]
    n_tiles = rows // tm
    tiles_per_seq = seq // tm
    nblk = tm // BLOCK
    t = lambda s: jnp.minimum(s, n_tiles - 1)
    return pl.pallas_call(
        functools.partial(_kv_kernel, tm=tm, tiles_per_seq=tiles_per_seq, n_tiles=n_tiles),
        grid=(n_tiles + 1,),
        in_specs=[
            pl.BlockSpec((tm, D_MODEL), lambda s: (t(s), 0)),
            _resident((n_dec, D_MODEL)),
            _resident(cache_kt.shape),
            _resident(cache_vt.shape),
            _resident((1, D_MODEL)),
            _resident((D_MODEL, 2 * KV_WIDTH)),
            _resident((1, PAIR)),
        ],
        out_specs=[
            pl.BlockSpec((1, nblk, N_KV_HEADS, PAIR, 2 * BLOCK),
                         lambda s: (t(s) // tiles_per_seq, t(s) % tiles_per_seq, 0, 0, 0)),
            pl.BlockSpec((1, nblk, 2 * BLOCK, N_KV_HEADS * PAIR),
                         lambda s: (t(s) // tiles_per_seq, t(s) % tiles_per_seq, 0, 0)),
            pl.BlockSpec((1, KV_WIDTH, BLOCK), lambda s: (t(s) // tiles_per_seq, 0, 0)),
            pl.BlockSpec((1, KV_WIDTH, BLOCK), lambda s: (t(s) // tiles_per_seq, 0, 0)),
            _resident(cache_kt.shape),
            _resident(cache_vt.shape),
            _resident(cache_kt.shape),
            _resident(cache_vt.shape),
        ],
        out_shape=[
            jax.ShapeDtypeStruct((bsz, seq // BLOCK, N_KV_HEADS, PAIR, 2 * BLOCK), BF16),
            jax.ShapeDtypeStruct((bsz, seq // BLOCK, 2 * BLOCK, N_KV_HEADS * PAIR), BF16),
            jax.ShapeDtypeStruct((bsz, KV_WIDTH, BLOCK), F32),
            jax.ShapeDtypeStruct((bsz, KV_WIDTH, BLOCK), F32),
            jax.ShapeDtypeStruct(cache_kt.shape, F32),
            jax.ShapeDtypeStruct(cache_vt.shape, F32),
            jax.ShapeDtypeStruct(cache_kt.shape, BF16),
            jax.ShapeDtypeStruct(cache_vt.shape, BF16),
        ],
        scratch_shapes=[
            pltpu.VMEM((D_MODEL, 2 * KV_WIDTH), BF16),
            pltpu.VMEM((N_KV_HEADS, PAIR, BLOCK), BF16),
            pltpu.VMEM((BLOCK, N_KV_HEADS * PAIR), BF16),
        ],
        compiler_params=pltpu.CompilerParams(
            dimension_semantics=("arbitrary",), vmem_limit_bytes=VMEM_LIMIT),
        name="shared_kv",
    )(x2d, xs, cache_kt, cache_vt, g, w_kv, k_gain_pair)


def _normed_q_pair(q, pr, qg_ref, pair_lo, scale):
    cols = slice(pr * PAIR, (pr + 1) * PAIR)
    qp = q[:, cols]
    return qp * _pair_inv_rms(qp, pair_lo, scale) * qg_ref[:, cols]


def _decode_attention(q, k3, v3, sink_col):
    n, n_keys = k3.shape[0], k3.shape[2]
    rows = n * N_HEADS
    ri = lax.broadcasted_iota(jnp.int32, (rows, n), 0)
    ci = lax.broadcasted_iota(jnp.int32, (rows, n), 1)
    rep = jnp.where(ri // N_HEADS == ci, 1.0, 0.0).astype(BF16)
    row_h = lax.broadcasted_iota(jnp.int32, (rows, ATTN_WIDTH), 0) % N_HEADS
    col_h = lax.broadcasted_iota(jnp.int32, (rows, ATTN_WIDTH), 1) // HEAD_DIM
    own = row_h == col_h
    fj = lax.broadcasted_iota(jnp.int32, (ATTN_WIDTH, KV_WIDTH), 0)
    fc = lax.broadcasted_iota(jnp.int32, (ATTN_WIDTH, KV_WIDTH), 1)
    to_kv = jnp.where((fj % HEAD_DIM == fc % HEAD_DIM) & ((fj // HEAD_DIM) // GQA_GROUP == fc // HEAD_DIM),
                      1.0, 0.0).astype(BF16)
    tc = lax.broadcasted_iota(jnp.int32, (KV_WIDTH, ATTN_WIDTH), 0)
    tj = lax.broadcasted_iota(jnp.int32, (KV_WIDTH, ATTN_WIDTH), 1)
    from_kv = jnp.where((tj % HEAD_DIM == tc % HEAD_DIM) & ((tj // HEAD_DIM) // GQA_GROUP == tc // HEAD_DIM),
                        1.0, 0.0).astype(BF16)

    m1 = jnp.where(own, _dot(rep, q.astype(BF16)), 0.0).astype(BF16)
    qm3 = _dot(m1, to_kv).astype(BF16).reshape(n, N_HEADS, KV_WIDTH)
    s = jnp.einsum("bhc,bck->bhk", qm3, k3, preferred_element_type=F32)
    h_col = lax.broadcasted_iota(jnp.int32, (N_HEADS, 1), 0).astype(F32)
    slope = jnp.exp2(-8.0 * (h_col + 1.0) / N_HEADS)[None]
    dist = (n_keys - 1 - lax.broadcasted_iota(jnp.int32, (1, n_keys), 1)).astype(F32)[None]
    s = s - slope * dist
    sink = sink_col[None]
    m = jnp.maximum(jnp.max(s, axis=-1, keepdims=True), sink)
    p = jnp.exp(s - m)
    den = jnp.sum(p, axis=-1, keepdims=True) + jnp.exp(sink - m)
    o3 = jnp.einsum("bhk,bck->bhc", p.astype(BF16), v3, preferred_element_type=F32) / den
    o2 = o3.reshape(rows, KV_WIDTH)
    hi = o2.astype(BF16)
    lo = (o2 - hi.astype(F32)).astype(BF16)
    spread = jnp.where(own, _dot(hi, from_kv) + _dot(lo, from_kv), 0.0)
    return jnp.sum(spread.reshape(n, N_HEADS, ATTN_WIDTH), axis=1)


def _attn_kernel(sink_ref, x_ref, xs_ref, k3_ref, v3_ref, sinkc_ref, g_ref, win_ref, qg_ref,
                 kt_ref, vb_ref, wout_ref,
                 out_ref, outs_ref,
                 win16_ref, wout16_ref, h_ref, qlo_ref, qhi_ref, stage_ref, sem_ref,
                 *, layer, tm, tiles_per_seq, n_tiles):
    s = pl.program_id(0)
    nblk = tm // BLOCK
    pair_lo = lax.broadcasted_iota(jnp.int32, (1, PAIR), 1) < HEAD_DIM

    @pl.when(s == 0)
    def _():
        rows = tm // 2
        plan = (_weight_blocks(win_ref.at[layer], win16_ref, rows, D_MODEL)
                + _weight_blocks(wout_ref.at[layer], wout16_ref, rows, D_MODEL))
        _load_weights_bf16(plan, _half_tiles(out_ref, stage_ref), sem_ref)

    @pl.when((s >= 1) & (s <= n_tiles))
    def _():
        i = (s - 1) % tiles_per_seq
        h_ref[...] = _rms(x_ref[...], g_ref[...]).astype(BF16)
        for kh in range(N_KV_HEADS):
            gcols = slice(kh * GQA_GROUP * HEAD_DIM, (kh + 1) * GQA_GROUP * HEAD_DIM)
            q = _dot(h_ref[...], win16_ref[:, gcols])
            for pr in range(GQA_GROUP // 2):
                cols = slice(pr * PAIR, (pr + 1) * PAIR)
                ocols = slice((kh * (GQA_GROUP // 2) + pr) * PAIR, (kh * (GQA_GROUP // 2) + pr + 1) * PAIR)
                qp = q[:, cols]
                qn = qp * _pair_inv_rms(qp, pair_lo, HEAD_DIM ** -0.5 * LOG2E) * qg_ref[:, ocols]
                qlo_ref[:, ocols] = jnp.where(pair_lo, qn, 0.0).astype(BF16)
                qhi_ref[:, ocols] = jnp.where(pair_lo, 0.0, qn).astype(BF16)

        r_idx = lax.broadcasted_iota(jnp.int32, (BLOCK, 2 * BLOCK), 0)
        c_idx = lax.broadcasted_iota(jnp.int32, (BLOCK, 2 * BLOCK), 1)
        dist = r_idx + BLOCK - c_idx
        in_window = (dist >= 0) & (dist < WINDOW)
        neg_dist = -dist.astype(F32)

        for blk in range(nblk):
            rows = slice(blk * BLOCK, (blk + 1) * BLOCK)
            first_key = (i * nblk + blk - 1) * BLOCK
            valid = in_window & (first_key + c_idx >= 0)
            nd = jnp.where(valid, neg_dist, NEG_INF / MIN_SLOPE)
            for kh in range(N_KV_HEADS):
                kt = kt_ref[0, blk, kh]
                vdup = vb_ref[0, blk, :, kh * PAIR:(kh + 1) * PAIR]
                pieces = []
                for pr in range(GQA_GROUP // 2):
                    cols = slice((kh * (GQA_GROUP // 2) + pr) * PAIR,
                                 (kh * (GQA_GROUP // 2) + pr + 1) * PAIR)
                    pieces.append(qlo_ref[rows, cols])
                    pieces.append(qhi_ref[rows, cols])
                s_all = _dot(jnp.concatenate(pieces, axis=0), kt)
                probs, inv_den = [], []
                for g_idx in range(GQA_GROUP):
                    head = kh * GQA_GROUP + g_idx
                    slope2 = 2.0 ** (-8.0 * (head + 1) / N_HEADS) * LOG2E
                    sc = s_all[g_idx * BLOCK:(g_idx + 1) * BLOCK] + slope2 * nd
                    sink2 = sink_ref[layer, head] * LOG2E
                    m = jnp.maximum(jnp.max(sc, axis=-1, keepdims=True), sink2)
                    p = jnp.exp2(sc - m)
                    den = jnp.sum(p, axis=-1, keepdims=True) + jnp.exp2(sink2 - m)
                    probs.append(p.astype(BF16))
                    inv_den.append(1.0 / den)
                o_all = _dot(jnp.concatenate(probs, axis=0), vdup)
                for pr in range(GQA_GROUP // 2):
                    cols = slice((kh * (GQA_GROUP // 2) + pr) * PAIR,
                                 (kh * (GQA_GROUP // 2) + pr + 1) * PAIR)
                    o_even = o_all[(2 * pr) * BLOCK:(2 * pr + 1) * BLOCK] * inv_den[2 * pr]
                    o_odd = o_all[(2 * pr + 1) * BLOCK:(2 * pr + 2) * BLOCK] * inv_den[2 * pr + 1]
                    out_ref[rows, cols] = jnp.where(pair_lo, o_even, o_odd)

        z = _dot(h_ref[...], win16_ref[:, ATTN_WIDTH:])
        y = out_ref[...] * _silu(z)
        out_ref[...] = x_ref[...] + _dot(y.astype(BF16), wout16_ref[...])

    @pl.when(s == n_tiles + 1)
    def _():
        x = xs_ref[...]
        h = _rms(x, g_ref[...]).astype(BF16)
        q = _dot(h, win16_ref[:, :ATTN_WIDTH])
        z = _dot(h, win16_ref[:, ATTN_WIDTH:])
        qn = jnp.concatenate(
            [_normed_q_pair(q, pr, qg_ref, pair_lo, HEAD_DIM ** -0.5) for pr in range(N_PAIRS)], axis=-1)
        n_dec = x.shape[0]
        o_rows = []
        for c in range(n_dec // DEC_CHUNK):
            rows = slice(c * DEC_CHUNK, (c + 1) * DEC_CHUNK)
            o_rows.append(_decode_attention(qn[rows], k3_ref[rows], v3_ref[rows],
                                            sinkc_ref[:, layer:layer + 1]))
        y = jnp.concatenate(o_rows, axis=0) * _silu(z)
        outs_ref[...] = x + _dot(y.astype(BF16), wout16_ref[...])


def _attn_layer(x2d, xs, k3, v3, sinks, sinks_t, g, w_in, q_gain, kt_band, v_band, w_out,
                *, layer, seq, tm):
    rows = x2d.shape[0]
    n_dec = xs.shape[0]
    n_tiles = rows // tm
    tiles_per_seq = seq // tm
    nblk = tm // BLOCK
    tile = lambda s: (_tile_of(s, n_tiles), 0)
    band = lambda s: (_tile_of(s, n_tiles) // tiles_per_seq, _tile_of(s, n_tiles) % tiles_per_seq)
    one = pl.Buffered(1)
    hbm = pl.BlockSpec(memory_space=pl.ANY)
    return pl.pallas_call(
        functools.partial(_attn_kernel, layer=layer, tm=tm, tiles_per_seq=tiles_per_seq, n_tiles=n_tiles),
        grid=(n_tiles + 2,),
        in_specs=[
            pl.BlockSpec(memory_space=pltpu.SMEM),
            pl.BlockSpec((tm, D_MODEL), tile),
            _resident((n_dec, D_MODEL)),
            _resident(k3.shape),
            _resident(v3.shape),
            _resident(sinks_t.shape),
            pl.BlockSpec((None, 1, D_MODEL), lambda s: (layer, 0, 0), pipeline_mode=one),
            hbm,
            pl.BlockSpec((None, 1, ATTN_WIDTH), lambda s: (layer, 0, 0), pipeline_mode=one),
            pl.BlockSpec((1, nblk, N_KV_HEADS, PAIR, 2 * BLOCK), lambda s: band(s) + (0, 0, 0)),
            pl.BlockSpec((1, nblk, 2 * BLOCK, N_KV_HEADS * PAIR), lambda s: band(s) + (0, 0)),
            hbm,
        ],
        out_specs=[
            pl.BlockSpec((tm, D_MODEL), tile),
            _resident((n_dec, D_MODEL)),
        ],
        out_shape=[
            jax.ShapeDtypeStruct((rows, D_MODEL), F32),
            jax.ShapeDtypeStruct((n_dec, D_MODEL), F32),
        ],
        scratch_shapes=[
            pltpu.VMEM((D_MODEL, 2 * ATTN_WIDTH), BF16),
            pltpu.VMEM((ATTN_WIDTH, D_MODEL), BF16),
            pltpu.VMEM((tm, D_MODEL), BF16),
            pltpu.VMEM((tm, ATTN_WIDTH), BF16),
            pltpu.VMEM((tm, ATTN_WIDTH), BF16),
            pltpu.VMEM((tm, D_MODEL), F32),
            pltpu.SemaphoreType.DMA((4,)),
        ],
        compiler_params=pltpu.CompilerParams(
            dimension_semantics=("arbitrary",), vmem_limit_bytes=VMEM_LIMIT),
        name="attn_layer",
    )(sinks, x2d, xs, k3, v3, sinks_t, g, w_in, q_gain, kt_band, v_band, w_out)


def kernel(x_prompt, x_sample, state_pool, cache_k_win, cache_v_win, norm_a, w_in_a, w_grp_a, scale_a, w_out_a, norm_kv, w_kv, k_norm, norm_b, w_in_b, q_norm, sinks, w_out_b):
    bsz, seq, _ = x_prompt.shape
    n_dec = x_sample.shape[0]
    assert x_sample.shape[1] == 1 and cache_k_win.shape[1] == WINDOW
    tm_a, tm_b, tm_kv = 256, 256, 1024

    hp = x_prompt.reshape(bsz * seq, D_MODEL)
    hs = x_sample[:, 0]
    state_t = state_pool.transpose(0, 2, 1, 3)
    pool_p, pool_s = [], []
    for l in range(N_A_LAYERS):
        hp, hist, hs, new_state = _pool_layer(
            hp, hs, state_t, norm_a[:, None], w_in_a, w_grp_a, scale_a[:, None], w_out_a,
            layer=l, seq=seq, tm=tm_a)
        pool_p.append(hist[:, POOL_HIST - POOL_BUF:])
        pool_s.append(new_state)

    def keys_minor(t):
        return t.transpose(0, 2, 3, 1).reshape(t.shape[0], KV_WIDTH, t.shape[1])

    def keys_major(t):
        return t.reshape(t.shape[0], N_KV_HEADS, HEAD_DIM, t.shape[2]).transpose(0, 3, 1, 2)

    kt_band, v_band, k_win_p, v_win_p, k_win_s, v_win_s, k16, v16 = _shared_kv(
        hp, hs, keys_minor(cache_k_win), keys_minor(cache_v_win),
        norm_kv[None], w_kv, jnp.tile(k_norm, 2)[None], seq=seq, tm=tm_kv)

    q_gain = jnp.tile(q_norm, (1, N_HEADS))[:, None]
    for j in range(N_B_LAYERS):
        hp, hs = _attn_layer(hp, hs, k16, v16, sinks, sinks.T, norm_b[:, None], w_in_b, q_gain,
                             kt_band, v_band, w_out_b, layer=j, seq=seq, tm=tm_b)

    return (hp.reshape(bsz, seq, D_MODEL), hs[:, None],
            jnp.stack(pool_p, axis=0), jnp.stack(pool_s, axis=0).transpose(0, 2, 1, 3),
            keys_major(k_win_p), keys_major(v_win_p), keys_major(k_win_s), keys_major(v_win_s))
```

```python
import functools
import math

import jax
import jax.numpy as jnp
from jax import lax
from jax.experimental import pallas as pl
from jax.experimental.pallas import tpu as pltpu

D_MODEL = 2048
N_A_LAYERS = 2
N_B_LAYERS = 2
POOL_WINDOWS = (2, 4, 8, 16)
N_POOL_GROUPS = len(POOL_WINDOWS)
POOL_WIDTH = D_MODEL
POOL_GROUP = POOL_WIDTH // N_POOL_GROUPS
POOL_BUF = max(POOL_WINDOWS) - 1
POOL_HIST = POOL_BUF + 1
HEAD_DIM = 64
N_HEADS = D_MODEL // HEAD_DIM
N_KV_HEADS = N_HEADS // 8
GQA_GROUP = N_HEADS // N_KV_HEADS
ATTN_WIDTH = N_HEADS * HEAD_DIM
KV_WIDTH = N_KV_HEADS * HEAD_DIM
WINDOW = 128
BLOCK = WINDOW
PAST_LEN = 16384
RMS_EPS = 1e-6
NEG_INF = -1e30
LOG2E = math.log2(math.e)
MIN_SLOPE = 2.0 ** -8
PAIR = 2 * HEAD_DIM
N_PAIRS = N_HEADS // 2
VMEM_LIMIT = 62 * 1024 * 1024

BF16 = jnp.bfloat16
F32 = jnp.float32


def _rms(x, g):
    return x * lax.rsqrt(jnp.mean(x * x, axis=-1, keepdims=True) + RMS_EPS) * g


def _silu(z):
    return z * jax.nn.sigmoid(z)


def _dot(a, b):
    return jnp.dot(a, b, preferred_element_type=F32)


def _resident(shape):
    return pl.BlockSpec(shape, lambda *_: (0,) * len(shape), pipeline_mode=pl.Buffered(1))


def _tile_of(s, n_tiles):
    return jnp.clip(s - 1, 0, n_tiles - 1)


def _weight_blocks(src, dst16, block_rows, block_cols):
    rows, cols = dst16.shape
    plan = []
    for r in range(0, rows, block_rows):
        for c in range(0, cols, block_cols):
            w = min(block_cols, cols - c)
            plan.append((src.at[pl.ds(r, block_rows), pl.ds(c, w)],
                         dst16.at[pl.ds(r, block_rows), pl.ds(c, w)], (block_rows, w)))
    return plan


def _load_weights_bf16(plan, slots, sem):
    n_slots = len(slots)
    in_flight = n_slots - 1

    def copy(c):
        src, _, (r, w) = plan[c]
        k = c % n_slots
        return pltpu.make_async_copy(src, slots[k].at[pl.ds(0, r), pl.ds(0, w)], sem.at[k])

    for c in range(min(in_flight, len(plan))):
        copy(c).start()
    for c in range(len(plan)):
        if c + in_flight < len(plan):
            copy(c + in_flight).start()
        copy(c).wait()
        _, dst, (r, w) = plan[c]
        dst[...] = slots[c % n_slots][0:r, 0:w].astype(BF16)


def _half_tiles(*refs):
    slots = []
    for ref in refs:
        half = ref.shape[0] // 2
        slots += [ref.at[pl.ds(0, half)], ref.at[pl.ds(half, half)]]
    return slots


def _pair_inv_rms(pair, pair_lo, scale):
    sq = pair * pair
    lo = jnp.sum(jnp.where(pair_lo, sq, 0.0), axis=-1, keepdims=True)
    hi = jnp.sum(jnp.where(pair_lo, 0.0, sq), axis=-1, keepdims=True)
    ms = jnp.where(pair_lo, lo, hi) * (1.0 / HEAD_DIM)
    return lax.rsqrt(ms + RMS_EPS) * scale


def _pool_tail(x_ref, u, z, window_sums, inv_cnt, wgrp_ref, scale_ref, wout_ref):
    mixed = []
    for g in range(N_POOL_GROUPS):
        cols = slice(g * POOL_GROUP, (g + 1) * POOL_GROUP)
        p = window_sums[g] * inv_cnt[g] - u[:, cols]
        mixed.append(_dot(p.astype(BF16), wgrp_ref[g]))
    y = jnp.concatenate(mixed, axis=-1) * scale_ref[...] * _silu(z)
    return x_ref[...] + _dot(y.astype(BF16), wout_ref[...])


def _pool_kernel(x_ref, xs_ref, st_ref, g_ref, win_ref, wgrp_ref, scale_ref, wout_ref,
                 out_ref, hist_ref, outs_ref, newst_ref,
                 win16_ref, wgrp16_ref, wout16_ref, uext_ref, sem_ref,
                 *, layer, tm, tiles_per_seq, n_tiles):
    s = pl.program_id(0)

    @pl.when(s == 0)
    def _():
        slots = _half_tiles(out_ref, uext_ref.at[pl.ds(POOL_HIST, tm)])
        rows = tm // 2
        assert POOL_GROUP // rows == len(slots) and N_POOL_GROUPS * POOL_GROUP == D_MODEL
        grp = [(wgrp_ref.at[layer, g, pl.ds(r * rows, rows)],
                slots[r].at[:, pl.ds(g * POOL_GROUP, POOL_GROUP)],
                wgrp16_ref.at[g, pl.ds(r * rows, rows)])
               for g in range(N_POOL_GROUPS) for r in range(len(slots))]
        grp_copies = [pltpu.make_async_copy(src, stage, sem_ref.at[0]) for src, stage, _ in grp]
        for cp in grp_copies:
            cp.start()
        for cp in grp_copies:
            cp.wait()
        for _, stage, dst in grp:
            dst[...] = stage[...].astype(BF16)
        plan = (_weight_blocks(win_ref.at[layer], win16_ref, rows, D_MODEL)
                + _weight_blocks(wout_ref.at[layer], wout16_ref, rows, D_MODEL))
        _load_weights_bf16(plan, slots, sem_ref)

    @pl.when((s >= 1) & (s <= n_tiles))
    def _():
        i = (s - 1) % tiles_per_seq

        @pl.when(i == 0)
        def _():
            uext_ref[0:POOL_HIST, :] = jnp.zeros((POOL_HIST, POOL_WIDTH), F32)

        x = x_ref[...]
        xg = (x * g_ref[...]).astype(BF16)
        inv_rms = lax.rsqrt(jnp.mean(x * x, axis=-1, keepdims=True) + RMS_EPS)
        u = _dot(xg, win16_ref[:, :POOL_WIDTH]) * inv_rms
        z = _dot(xg, win16_ref[:, POOL_WIDTH:]) * inv_rms
        uext_ref[POOL_HIST:POOL_HIST + tm, :] = u

        pos = i * tm + lax.broadcasted_iota(jnp.int32, (tm, 1), 0)
        sums, inv_cnt = [], []
        for g, w in enumerate(POOL_WINDOWS):
            cols = slice(g * POOL_GROUP, (g + 1) * POOL_GROUP)
            acc = u[:, cols]
            for k in range(1, w):
                acc = acc + uext_ref[POOL_HIST - k:POOL_HIST - k + tm, cols]
            sums.append(acc)
            inv_cnt.append(1.0 / jnp.minimum(pos + 1, w).astype(F32))
        out_ref[...] = _pool_tail(x_ref, u, z, sums, inv_cnt, wgrp16_ref, scale_ref, wout16_ref)

        tail = uext_ref[tm:tm + POOL_HIST, :]
        hist_ref[0] = tail
        uext_ref[0:POOL_HIST, :] = tail

    @pl.when(s == n_tiles + 1)
    def _():
        h = _rms(xs_ref[...], g_ref[...]).astype(BF16)
        u = _dot(h, win16_ref[:, :POOL_WIDTH])
        z = _dot(h, win16_ref[:, POOL_WIDTH:])
        for r in range(POOL_BUF - 1):
            newst_ref[r] = st_ref[0, r + 1]
        newst_ref[POOL_BUF - 1] = u
        sums, inv_cnt = [], []
        for g, w in enumerate(POOL_WINDOWS):
            cols = slice(g * POOL_GROUP, (g + 1) * POOL_GROUP)
            acc = u[:, cols]
            for k in range(1, w):
                acc = acc + st_ref[0, POOL_BUF - k, :, cols]
            sums.append(acc)
            inv_cnt.append(1.0 / float(min(PAST_LEN + 1, w)))
        outs_ref[...] = _pool_tail(xs_ref, u, z, sums, inv_cnt, wgrp16_ref, scale_ref, wout16_ref)


def _pool_layer(x2d, xs, state, g, w_in, w_grp, scale, w_out, *, layer, seq, tm):
    rows = x2d.shape[0]
    n_dec = xs.shape[0]
    n_tiles = rows // tm
    tiles_per_seq = seq // tm
    tile = lambda s: (_tile_of(s, n_tiles), 0)
    one = pl.Buffered(1)
    hbm = pl.BlockSpec(memory_space=pl.ANY)
    return pl.pallas_call(
        functools.partial(_pool_kernel, layer=layer, tm=tm, tiles_per_seq=tiles_per_seq, n_tiles=n_tiles),
        grid=(n_tiles + 2,),
        in_specs=[
            pl.BlockSpec((tm, D_MODEL), tile),
            _resident((n_dec, D_MODEL)),
            pl.BlockSpec((1, POOL_BUF, n_dec, POOL_WIDTH), lambda s: (layer, 0, 0, 0), pipeline_mode=one),
            pl.BlockSpec((None, 1, D_MODEL), lambda s: (layer, 0, 0), pipeline_mode=one),
            hbm,
            hbm,
            pl.BlockSpec((None, 1, POOL_WIDTH), lambda s: (layer, 0, 0), pipeline_mode=one),
            hbm,
        ],
        out_specs=[
            pl.BlockSpec((tm, D_MODEL), tile),
            pl.BlockSpec((1, POOL_HIST, POOL_WIDTH),
                         lambda s: (_tile_of(s, n_tiles) // tiles_per_seq, 0, 0)),
            _resident((n_dec, D_MODEL)),
            _resident((POOL_BUF, n_dec, POOL_WIDTH)),
        ],
        out_shape=[
            jax.ShapeDtypeStruct((rows, D_MODEL), F32),
            jax.ShapeDtypeStruct((rows // seq, POOL_HIST, POOL_WIDTH), F32),
            jax.ShapeDtypeStruct((n_dec, D_MODEL), F32),
            jax.ShapeDtypeStruct((POOL_BUF, n_dec, POOL_WIDTH), F32),
        ],
        scratch_shapes=[
            pltpu.VMEM((D_MODEL, 2 * POOL_WIDTH), BF16),
            pltpu.VMEM((N_POOL_GROUPS, POOL_GROUP, POOL_GROUP), BF16),
            pltpu.VMEM((POOL_WIDTH, D_MODEL), BF16),
            pltpu.VMEM((POOL_HIST + tm, POOL_WIDTH), F32),
            pltpu.SemaphoreType.DMA((4,)),
        ],
        compiler_params=pltpu.CompilerParams(
            dimension_semantics=("arbitrary",), vmem_limit_bytes=VMEM_LIMIT),
        name="pool_layer",
    )(x2d, xs, state, g, w_in, w_grp, scale, w_out)


def _kv_project(x, g_ref, w16_ref, kg_ref):
    h = _rms(x, g_ref[...]).astype(BF16)
    kv = _dot(h, w16_ref[...])
    pair_lo = lax.broadcasted_iota(jnp.int32, (1, PAIR), 1) < HEAD_DIM
    k_pairs = []
    for p in range(KV_WIDTH // PAIR):
        kp = kv[:, p * PAIR:(p + 1) * PAIR]
        k_pairs.append(kp * _pair_inv_rms(kp, pair_lo, 1.0) * kg_ref[...])
    return k_pairs, kv[:, KV_WIDTH:]


def _dup_heads(col, pair_lo):
    rolled = pltpu.roll(col, HEAD_DIM, 1)
    return jnp.where(pair_lo, col, rolled), jnp.where(pair_lo, rolled, col)


def _slide_window(cache_ref, new_t, win_ref, win16_ref):
    n, _, n_keys = cache_ref.shape
    last = lax.broadcasted_iota(jnp.int32, (1, n_keys), 1) == n_keys - 1
    for b in range(n):
        slid = jnp.where(last, new_t[:, b:b + 1], pltpu.roll(cache_ref[b], n_keys - 1, 1))
        win_ref[b] = slid
        win16_ref[b] = slid.astype(BF16)


def _kv_kernel(x_ref, xs_ref, ck_ref, cv_ref, g_ref, wkv_ref, kg_ref,
               kt_ref, vb_ref, kwin_ref, vwin_ref, kws_ref, vws_ref, k16_ref, v16_ref,
               w16_ref, pk_ref, pv_ref, *, tm, chunk, tiles_per_seq, dec_steps):
    s = pl.program_id(0)
    nblk = tm // BLOCK
    blk_per_chunk = chunk // BLOCK
    pair_lo = lax.broadcasted_iota(jnp.int32, (1, PAIR), 1) < HEAD_DIM

    @pl.when(s == 0)
    def _():
        w16_ref[...] = wkv_ref[...].astype(BF16)

    @pl.when(s % tiles_per_seq == 0)
    def _():
        pk_ref[...] = jnp.zeros(pk_ref.shape, BF16)
        pv_ref[...] = jnp.zeros(pv_ref.shape, BF16)

    @pl.when(s < dec_steps)
    def _():
        k_pairs, v = _kv_project(xs_ref[...], g_ref, w16_ref, kg_ref)
        n_dec = xs_ref.shape[0]
        pad = jnp.zeros((BLOCK - n_dec, KV_WIDTH), F32)
        kn_t = jnp.concatenate([jnp.concatenate(k_pairs, axis=1), pad], axis=0)
        vn_t = jnp.concatenate([v, pad], axis=0)
        kn_t = jnp.concatenate([kn_t[:, c * BLOCK:(c + 1) * BLOCK].T for c in range(KV_WIDTH // BLOCK)], axis=0)
        vn_t = jnp.concatenate([vn_t[:, c * BLOCK:(c + 1) * BLOCK].T for c in range(KV_WIDTH // BLOCK)], axis=0)
        _slide_window(ck_ref, kn_t, kws_ref, k16_ref)
        _slide_window(cv_ref, vn_t, vws_ref, v16_ref)

    for ch in range(tm // chunk):
        k_pairs, v = _kv_project(x_ref[ch * chunk:(ch + 1) * chunk, :], g_ref, w16_ref, kg_ref)
        for jc in range(blk_per_chunk):
            jb = ch * blk_per_chunk + jc
            rows = slice(jc * BLOCK, (jc + 1) * BLOCK)
            for p in range(KV_WIDTH // PAIR):
                k_dup = _dup_heads(k_pairs[p][rows], pair_lo)
                v_dup = _dup_heads(v[rows, p * PAIR:(p + 1) * PAIR], pair_lo)
                for half in range(2):
                    kh = 2 * p + half
                    cols = slice(kh * PAIR, (kh + 1) * PAIR)
                    kt_f32 = k_dup[half].T
                    kt_cur = kt_f32.astype(BF16)
                    v_cur = v_dup[half].astype(BF16)
                    if jb == nblk - 1:
                        hd = slice(kh * HEAD_DIM, (kh + 1) * HEAD_DIM)
                        kwin_ref[0, hd, :] = kt_f32[:HEAD_DIM]
                        vwin_ref[0, hd, :] = v_dup[half].T[:HEAD_DIM]
                    kt_ref[0, jb, kh, :, 0:BLOCK] = pk_ref[kh]
                    kt_ref[0, jb, kh, :, BLOCK:2 * BLOCK] = kt_cur
                    vb_ref[0, jb, 0:BLOCK, cols] = pv_ref[:, cols]
                    vb_ref[0, jb, BLOCK:2 * BLOCK, cols] = v_cur
                    pk_ref[kh] = kt_cur
                    pv_ref[:, cols] = v_cur


def _shared_kv(x2d, xs, cache_kt, cache_vt, g, w_kv, k_gain_pair, *, seq, tm, chunk, dec_rows):
    rows = x2d.shape[0]
    bsz = rows // seq
    n_dec = xs.shape[0]
    n_tiles = rows // tm
    tiles_per_seq = seq // tm
    nblk = tm // BLOCK
    dec_steps = n_dec // dec_rows
    assert dec_steps <= n_tiles and dec_steps * dec_rows == n_dec
    dec = lambda s: (jnp.minimum(s, dec_steps - 1), 0, 0)
    win = (dec_rows,) + cache_kt.shape[1:]
    return pl.pallas_call(
        functools.partial(_kv_kernel, tm=tm, chunk=chunk, tiles_per_seq=tiles_per_seq, dec_steps=dec_steps),
        grid=(n_tiles,),
        in_specs=[
            pl.BlockSpec((tm, D_MODEL), lambda s: (s, 0)),
            pl.BlockSpec((dec_rows, D_MODEL), lambda s: (jnp.minimum(s, dec_steps - 1), 0)),
            pl.BlockSpec(win, dec),
            pl.BlockSpec(win, dec),
            _resident((1, D_MODEL)),
            _resident((D_MODEL, 2 * KV_WIDTH)),
            _resident((1, PAIR)),
        ],
        out_specs=[
            pl.BlockSpec((1, nblk, N_KV_HEADS, PAIR, 2 * BLOCK),
                         lambda s: (s // tiles_per_seq, s % tiles_per_seq, 0, 0, 0)),
            pl.BlockSpec((1, nblk, 2 * BLOCK, N_KV_HEADS * PAIR),
                         lambda s: (s // tiles_per_seq, s % tiles_per_seq, 0, 0)),
            pl.BlockSpec((1, KV_WIDTH, BLOCK), lambda s: (s // tiles_per_seq, 0, 0)),
            pl.BlockSpec((1, KV_WIDTH, BLOCK), lambda s: (s // tiles_per_seq, 0, 0)),
            pl.BlockSpec(win, dec),
            pl.BlockSpec(win, dec),
            pl.BlockSpec(win, dec),
            pl.BlockSpec(win, dec),
        ],
        out_shape=[
            jax.ShapeDtypeStruct((bsz, seq // BLOCK, N_KV_HEADS, PAIR, 2 * BLOCK), BF16),
            jax.ShapeDtypeStruct((bsz, seq // BLOCK, 2 * BLOCK, N_KV_HEADS * PAIR), BF16),
            jax.ShapeDtypeStruct((bsz, KV_WIDTH, BLOCK), F32),
            jax.ShapeDtypeStruct((bsz, KV_WIDTH, BLOCK), F32),
            jax.ShapeDtypeStruct(cache_kt.shape, F32),
            jax.ShapeDtypeStruct(cache_vt.shape, F32),
            jax.ShapeDtypeStruct(cache_kt.shape, BF16),
            jax.ShapeDtypeStruct(cache_vt.shape, BF16),
        ],
        scratch_shapes=[
            pltpu.VMEM((D_MODEL, 2 * KV_WIDTH), BF16),
            pltpu.VMEM((N_KV_HEADS, PAIR, BLOCK), BF16),
            pltpu.VMEM((BLOCK, N_KV_HEADS * PAIR), BF16),
        ],
        compiler_params=pltpu.CompilerParams(
            dimension_semantics=("arbitrary",), vmem_limit_bytes=VMEM_LIMIT),
        name="shared_kv",
    )(x2d, xs, cache_kt, cache_vt, g, w_kv, k_gain_pair)


def _normed_q_pair(q, pr, qg_ref, pair_lo, scale):
    cols = slice(pr * PAIR, (pr + 1) * PAIR)
    qp = q[:, cols]
    return qp * _pair_inv_rms(qp, pair_lo, scale) * qg_ref[:, cols]


def _decode_attention(q, k3, v3, sink_col):
    n, n_keys = k3.shape[0], k3.shape[2]
    gw = GQA_GROUP * HEAD_DIM
    rows = n * GQA_GROUP
    ri = lax.broadcasted_iota(jnp.int32, (rows, n), 0)
    ci = lax.broadcasted_iota(jnp.int32, (rows, n), 1)
    rep = jnp.where(ri // GQA_GROUP == ci, 1.0, 0.0).astype(BF16)
    row_g = lax.broadcasted_iota(jnp.int32, (rows, gw), 0) % GQA_GROUP
    col_g = lax.broadcasted_iota(jnp.int32, (rows, gw), 1) // HEAD_DIM
    own = row_g == col_g
    fj = lax.broadcasted_iota(jnp.int32, (gw, KV_WIDTH), 0)
    fc = lax.broadcasted_iota(jnp.int32, (gw, KV_WIDTH), 1)
    tc = lax.broadcasted_iota(jnp.int32, (KV_WIDTH, gw), 0)
    tj = lax.broadcasted_iota(jnp.int32, (KV_WIDTH, gw), 1)
    qb = q.astype(BF16)

    q_groups = []
    for kh in range(N_KV_HEADS):
        to_kv = jnp.where((fj % HEAD_DIM == fc % HEAD_DIM) & (fc // HEAD_DIM == kh), 1.0, 0.0).astype(BF16)
        mine = jnp.where(own, _dot(rep, qb[:, kh * gw:(kh + 1) * gw]), 0.0).astype(BF16)
        q_groups.append(_dot(mine, to_kv).astype(BF16).reshape(n, GQA_GROUP, KV_WIDTH))
    qm3 = jnp.concatenate(q_groups, axis=1)
    s = jnp.einsum("bhc,bck->bhk", qm3, k3, preferred_element_type=F32)
    h_col = lax.broadcasted_iota(jnp.int32, (N_HEADS, 1), 0).astype(F32)
    slope = jnp.exp2(-8.0 * (h_col + 1.0) / N_HEADS)[None]
    dist = (n_keys - 1 - lax.broadcasted_iota(jnp.int32, (1, n_keys), 1)).astype(F32)[None]
    s = s - slope * dist
    sink = sink_col[None]
    m = jnp.maximum(jnp.max(s, axis=-1, keepdims=True), sink)
    p = jnp.exp(s - m)
    den = jnp.sum(p, axis=-1, keepdims=True) + jnp.exp(sink - m)
    o3 = jnp.einsum("bhk,bck->bhc", p.astype(BF16), v3, preferred_element_type=F32) / den
    outs = []
    for kh in range(N_KV_HEADS):
        from_kv = jnp.where((tj % HEAD_DIM == tc % HEAD_DIM) & (tc // HEAD_DIM == kh), 1.0, 0.0).astype(BF16)
        o2 = o3[:, kh * GQA_GROUP:(kh + 1) * GQA_GROUP, :].reshape(rows, KV_WIDTH)
        hi = o2.astype(BF16)
        lo = (o2 - hi.astype(F32)).astype(BF16)
        spread = jnp.where(own, _dot(hi, from_kv) + _dot(lo, from_kv), 0.0)
        outs.append(jnp.sum(spread.reshape(n, GQA_GROUP, gw), axis=1))
    return jnp.concatenate(outs, axis=-1)


def _attn_kernel(sink_ref, x_ref, xs_ref, k3_ref, v3_ref, sinkc_ref, g_ref, win_ref, qg_ref,
                 kt_ref, vb_ref, wout_ref,
                 out_ref, outs_ref,
                 win16_ref, wout16_ref, h_ref, qlo_ref, qhi_ref, stage_ref, sem_ref,
                 *, layer, tm, tiles_per_seq, n_tiles):
    s = pl.program_id(0)
    nblk = tm // BLOCK
    pair_lo = lax.broadcasted_iota(jnp.int32, (1, PAIR), 1) < HEAD_DIM

    @pl.when(s == 0)
    def _():
        rows = tm // 2
        plan = (_weight_blocks(win_ref.at[layer], win16_ref, rows, D_MODEL)
                + _weight_blocks(wout_ref.at[layer], wout16_ref, rows, D_MODEL))
        _load_weights_bf16(plan, _half_tiles(out_ref, stage_ref), sem_ref)

    @pl.when((s >= 1) & (s <= n_tiles))
    def _():
        i = (s - 1) % tiles_per_seq
        h_ref[...] = _rms(x_ref[...], g_ref[...]).astype(BF16)
        for kh in range(N_KV_HEADS):
            gcols = slice(kh * GQA_GROUP * HEAD_DIM, (kh + 1) * GQA_GROUP * HEAD_DIM)
            q = _dot(h_ref[...], win16_ref[:, gcols])
            for pr in range(GQA_GROUP // 2):
                cols = slice(pr * PAIR, (pr + 1) * PAIR)
                ocols = slice((kh * (GQA_GROUP // 2) + pr) * PAIR, (kh * (GQA_GROUP // 2) + pr + 1) * PAIR)
                qp = q[:, cols]
                qn = qp * _pair_inv_rms(qp, pair_lo, HEAD_DIM ** -0.5 * LOG2E) * qg_ref[:, ocols]
                qlo_ref[:, ocols] = jnp.where(pair_lo, qn, 0.0).astype(BF16)
                qhi_ref[:, ocols] = jnp.where(pair_lo, 0.0, qn).astype(BF16)

        r_idx = lax.broadcasted_iota(jnp.int32, (BLOCK, 2 * BLOCK), 0)
        c_idx = lax.broadcasted_iota(jnp.int32, (BLOCK, 2 * BLOCK), 1)
        dist = r_idx + BLOCK - c_idx
        in_window = (dist >= 0) & (dist < WINDOW)
        neg_dist = -dist.astype(F32)

        for blk in range(nblk):
            rows = slice(blk * BLOCK, (blk + 1) * BLOCK)
            first_key = (i * nblk + blk - 1) * BLOCK
            valid = in_window & (first_key + c_idx >= 0)
            nd = jnp.where(valid, neg_dist, NEG_INF / MIN_SLOPE)
            for kh in range(N_KV_HEADS):
                kt = kt_ref[0, blk, kh]
                vdup = vb_ref[0, blk, :, kh * PAIR:(kh + 1) * PAIR]
                pieces = []
                for pr in range(GQA_GROUP // 2):
                    cols = slice((kh * (GQA_GROUP // 2) + pr) * PAIR,
                                 (kh * (GQA_GROUP // 2) + pr + 1) * PAIR)
                    pieces.append(qlo_ref[rows, cols])
                    pieces.append(qhi_ref[rows, cols])
                s_all = _dot(jnp.concatenate(pieces, axis=0), kt)
                probs, inv_den = [], []
                for g_idx in range(GQA_GROUP):
                    head = kh * GQA_GROUP + g_idx
                    slope2 = 2.0 ** (-8.0 * (head + 1) / N_HEADS) * LOG2E
                    sc = s_all[g_idx * BLOCK:(g_idx + 1) * BLOCK] + slope2 * nd
                    sink2 = sink_ref[layer, head] * LOG2E
                    m = jnp.maximum(jnp.max(sc, axis=-1, keepdims=True), sink2)
                    p = jnp.exp2(sc - m)
                    den = jnp.sum(p, axis=-1, keepdims=True) + jnp.exp2(sink2 - m)
                    probs.append(p.astype(BF16))
                    inv_den.append(1.0 / den)
                o_all = _dot(jnp.concatenate(probs, axis=0), vdup)
                for pr in range(GQA_GROUP // 2):
                    cols = slice((kh * (GQA_GROUP // 2) + pr) * PAIR,
                                 (kh * (GQA_GROUP // 2) + pr + 1) * PAIR)
                    o_even = o_all[(2 * pr) * BLOCK:(2 * pr + 1) * BLOCK] * inv_den[2 * pr]
                    o_odd = o_all[(2 * pr + 1) * BLOCK:(2 * pr + 2) * BLOCK] * inv_den[2 * pr + 1]
                    out_ref[rows, cols] = jnp.where(pair_lo, o_even, o_odd)

        z = _dot(h_ref[...], win16_ref[:, ATTN_WIDTH:])
        y = out_ref[...] * _silu(z)
        out_ref[...] = x_ref[...] + _dot(y.astype(BF16), wout16_ref[...])

    @pl.when(s == n_tiles + 1)
    def _():
        x = xs_ref[...]
        h = _rms(x, g_ref[...]).astype(BF16)
        q = _dot(h, win16_ref[:, :ATTN_WIDTH])
        z = _dot(h, win16_ref[:, ATTN_WIDTH:])
        qn = jnp.concatenate(
            [_normed_q_pair(q, pr, qg_ref, pair_lo, HEAD_DIM ** -0.5) for pr in range(N_PAIRS)], axis=-1)
        o = _decode_attention(qn, k3_ref[...], v3_ref[...], sinkc_ref[:, layer:layer + 1])
        y = o * _silu(z)
        outs_ref[...] = x + _dot(y.astype(BF16), wout16_ref[...])


def _attn_layer(x2d, xs, k3, v3, sinks, sinks_t, g, w_in, q_gain, kt_band, v_band, w_out,
                *, layer, seq, tm):
    rows = x2d.shape[0]
    n_dec = xs.shape[0]
    n_tiles = rows // tm
    tiles_per_seq = seq // tm
    nblk = tm // BLOCK
    tile = lambda s: (_tile_of(s, n_tiles), 0)
    band = lambda s: (_tile_of(s, n_tiles) // tiles_per_seq, _tile_of(s, n_tiles) % tiles_per_seq)
    one = pl.Buffered(1)
    hbm = pl.BlockSpec(memory_space=pl.ANY)
    return pl.pallas_call(
        functools.partial(_attn_kernel, layer=layer, tm=tm, tiles_per_seq=tiles_per_seq, n_tiles=n_tiles),
        grid=(n_tiles + 2,),
        in_specs=[
            pl.BlockSpec(memory_space=pltpu.SMEM),
            pl.BlockSpec((tm, D_MODEL), tile),
            _resident((n_dec, D_MODEL)),
            _resident(k3.shape),
            _resident(v3.shape),
            _resident(sinks_t.shape),
            pl.BlockSpec((None, 1, D_MODEL), lambda s: (layer, 0, 0), pipeline_mode=one),
            hbm,
            pl.BlockSpec((None, 1, ATTN_WIDTH), lambda s: (layer, 0, 0), pipeline_mode=one),
            pl.BlockSpec((1, nblk, N_KV_HEADS, PAIR, 2 * BLOCK), lambda s: band(s) + (0, 0, 0)),
            pl.BlockSpec((1, nblk, 2 * BLOCK, N_KV_HEADS * PAIR), lambda s: band(s) + (0, 0)),
            hbm,
        ],
        out_specs=[
            pl.BlockSpec((tm, D_MODEL), tile),
            _resident((n_dec, D_MODEL)),
        ],
        out_shape=[
            jax.ShapeDtypeStruct((rows, D_MODEL), F32),
            jax.ShapeDtypeStruct((n_dec, D_MODEL), F32),
        ],
        scratch_shapes=[
            pltpu.VMEM((D_MODEL, 2 * ATTN_WIDTH), BF16),
            pltpu.VMEM((ATTN_WIDTH, D_MODEL), BF16),
            pltpu.VMEM((tm, D_MODEL), BF16),
            pltpu.VMEM((tm, ATTN_WIDTH), BF16),
            pltpu.VMEM((tm, ATTN_WIDTH), BF16),
            pltpu.VMEM((tm, D_MODEL), F32),
            pltpu.SemaphoreType.DMA((4,)),
        ],
        compiler_params=pltpu.CompilerParams(
            dimension_semantics=("arbitrary",), vmem_limit_bytes=VMEM_LIMIT),
        name="attn_layer",
    )(sinks, x2d, xs, k3, v3, sinks_t, g, w_in, q_gain, kt_band, v_band, w_out)


def kernel(x_prompt, x_sample, state_pool, cache_k_win, cache_v_win, norm_a, w_in_a, w_grp_a, scale_a, w_out_a, norm_kv, w_kv, k_norm, norm_b, w_in_b, q_norm, sinks, w_out_b):
    bsz, seq, _ = x_prompt.shape
    n_dec = x_sample.shape[0]
    assert x_sample.shape[1] == 1 and cache_k_win.shape[1] == WINDOW
    tm_a, tm_b, tm_kv = 256, 256, 1024

    hp = x_prompt.reshape(bsz * seq, D_MODEL)
    hs = x_sample[:, 0]
    state_t = state_pool.transpose(0, 2, 1, 3)
    pool_p, pool_s = [], []
    for l in range(N_A_LAYERS):
        hp, hist, hs, new_state = _pool_layer(
            hp, hs, state_t, norm_a[:, None], w_in_a, w_grp_a, scale_a[:, None], w_out_a,
            layer=l, seq=seq, tm=tm_a)
        pool_p.append(hist[:, POOL_HIST - POOL_BUF:])
        pool_s.append(new_state)

    def keys_minor(t):
        return t.transpose(0, 2, 3, 1).reshape(t.shape[0], KV_WIDTH, t.shape[1])

    def keys_major(t):
        return t.reshape(t.shape[0], N_KV_HEADS, HEAD_DIM, t.shape[2]).transpose(0, 3, 1, 2)

    kt_band, v_band, k_win_p, v_win_p, k_win_s, v_win_s, k16, v16 = _shared_kv(
        hp, hs, keys_minor(cache_k_win), keys_minor(cache_v_win),
        norm_kv[None], w_kv, jnp.tile(k_norm, 2)[None], seq=seq, tm=tm_kv, chunk=256, dec_rows=8)

    q_gain = jnp.tile(q_norm, (1, N_HEADS))[:, None]
    for j in range(N_B_LAYERS):
        hp, hs = _attn_layer(hp, hs, k16, v16, sinks, sinks.T, norm_b[:, None], w_in_b, q_gain,
                             kt_band, v_band, w_out_b, layer=j, seq=seq, tm=tm_b)

    return (hp.reshape(bsz, seq, D_MODEL), hs[:, None],
            jnp.stack(pool_p, axis=0), jnp.stack(pool_s, axis=0).transpose(0, 2, 1, 3),
            keys_major(k_win_p), keys_major(v_win_p), keys_major(k_win_s), keys_major(v_win_s))
```

```python
import functools
import math

import jax
import jax.numpy as jnp
from jax import lax
from jax.experimental import pallas as pl
from jax.experimental.pallas import tpu as pltpu

D_MODEL = 2048
N_A_LAYERS = 2
N_B_LAYERS = 2
POOL_WINDOWS = (2, 4, 8, 16)
N_POOL_GROUPS = len(POOL_WINDOWS)
POOL_WIDTH = D_MODEL
POOL_GROUP = POOL_WIDTH // N_POOL_GROUPS
POOL_BUF = max(POOL_WINDOWS) - 1
POOL_HIST = POOL_BUF + 1
HEAD_DIM = 64
N_HEADS = D_MODEL // HEAD_DIM
N_KV_HEADS = N_HEADS // 8
GQA_GROUP = N_HEADS // N_KV_HEADS
ATTN_WIDTH = N_HEADS * HEAD_DIM
KV_WIDTH = N_KV_HEADS * HEAD_DIM
WINDOW = 128
BLOCK = WINDOW
PAST_LEN = 16384
RMS_EPS = 1e-6
NEG_INF = -1e30
LOG2E = math.log2(math.e)
MIN_SLOPE = 2.0 ** -8
PAIR = 2 * HEAD_DIM
N_PAIRS = N_HEADS // 2
VMEM_LIMIT = 62 * 1024 * 1024

BF16 = jnp.bfloat16
F32 = jnp.float32


def _rms(x, g):
    return x * lax.rsqrt(jnp.mean(x * x, axis=-1, keepdims=True) + RMS_EPS) * g


def _silu(z):
    return z * jax.nn.sigmoid(z)


def _dot(a, b):
    return jnp.dot(a, b, preferred_element_type=F32)


def _resident(shape):
    return pl.BlockSpec(shape, lambda *_: (0,) * len(shape), pipeline_mode=pl.Buffered(1))


def _tile_of(s, n_tiles):
    return jnp.clip(s - 1, 0, n_tiles - 1)


def _weight_blocks(src, dst16, block_rows, block_cols):
    rows, cols = dst16.shape
    plan = []
    for r in range(0, rows, block_rows):
        for c in range(0, cols, block_cols):
            w = min(block_cols, cols - c)
            plan.append((src.at[pl.ds(r, block_rows), pl.ds(c, w)],
                         dst16.at[pl.ds(r, block_rows), pl.ds(c, w)], (block_rows, w)))
    return plan


def _load_weights_bf16(plan, slots, sem):
    n_slots = len(slots)
    in_flight = n_slots - 1

    def copy(c):
        src, _, (r, w) = plan[c]
        k = c % n_slots
        return pltpu.make_async_copy(src, slots[k].at[pl.ds(0, r), pl.ds(0, w)], sem.at[k])

    for c in range(min(in_flight, len(plan))):
        copy(c).start()
    for c in range(len(plan)):
        if c + in_flight < len(plan):
            copy(c + in_flight).start()
        copy(c).wait()
        _, dst, (r, w) = plan[c]
        dst[...] = slots[c % n_slots][0:r, 0:w].astype(BF16)


def _fetch_bf16(pairs, sem):
    copies = [pltpu.make_async_copy(src, dst, sem.at[i]) for i, (src, dst) in enumerate(pairs)]
    for cp in copies:
        cp.start()
    for cp in copies:
        cp.wait()


def _half_tiles(*refs):
    slots = []
    for ref in refs:
        half = ref.shape[0] // 2
        slots += [ref.at[pl.ds(0, half)], ref.at[pl.ds(half, half)]]
    return slots


def _pair_inv_rms(pair, pair_lo, scale):
    sq = pair * pair
    lo = jnp.sum(jnp.where(pair_lo, sq, 0.0), axis=-1, keepdims=True)
    hi = jnp.sum(jnp.where(pair_lo, 0.0, sq), axis=-1, keepdims=True)
    ms = jnp.where(pair_lo, lo, hi) * (1.0 / HEAD_DIM)
    return lax.rsqrt(ms + RMS_EPS) * scale


def _pool_tail(x_ref, u, z, window_sums, inv_cnt, wgrp_ref, scale_ref, wout_ref):
    mixed = []
    for g in range(N_POOL_GROUPS):
        cols = slice(g * POOL_GROUP, (g + 1) * POOL_GROUP)
        p = window_sums[g] * inv_cnt[g] - u[:, cols]
        mixed.append(_dot(p.astype(BF16), wgrp_ref[cols, :]))
    y = jnp.concatenate(mixed, axis=-1) * scale_ref[...] * _silu(z)
    return x_ref[...] + _dot(y.astype(BF16), wout_ref[...])


def _pool_kernel(*refs, layer, tm, tiles_per_seq, n_tiles, own_bf16, n_conv):
    (x_ref, xs_ref, st_ref, g_ref, win_ref, wgrp_ref, scale_ref, wout_ref) = refs[:8]
    conv_src = refs[8:8 + n_conv]
    (out_ref, hist_ref, outs_ref, newst_ref) = refs[8 + n_conv:12 + n_conv]
    conv_dst = refs[12 + n_conv:12 + 2 * n_conv]
    (win16_ref, wgrp16_ref, wout16_ref, uext_ref, sem_ref) = refs[12 + 2 * n_conv:]
    s = pl.program_id(0)

    @pl.when(s == 0)
    def _():
        if own_bf16:
            _fetch_bf16([(win_ref, win16_ref), (wgrp_ref, wgrp16_ref), (wout_ref, wout16_ref)], sem_ref)
            return
        slots = _half_tiles(out_ref, uext_ref.at[pl.ds(POOL_HIST, tm)])
        rows = tm // 2
        assert POOL_GROUP // rows == len(slots) and N_POOL_GROUPS * POOL_GROUP == D_MODEL
        grp = [(wgrp_ref.at[layer, pl.ds(g * POOL_GROUP + r * rows, rows)],
                slots[r].at[:, pl.ds(g * POOL_GROUP, POOL_GROUP)],
                wgrp16_ref.at[pl.ds(g * POOL_GROUP + r * rows, rows)])
               for g in range(N_POOL_GROUPS) for r in range(len(slots))]
        grp_copies = [pltpu.make_async_copy(src, stage, sem_ref.at[0]) for src, stage, _ in grp]
        for cp in grp_copies:
            cp.start()
        for cp in grp_copies:
            cp.wait()
        for _, stage, dst in grp:
            dst[...] = stage[...].astype(BF16)
        plan = (_weight_blocks(win_ref.at[layer], win16_ref, rows, D_MODEL)
                + _weight_blocks(wout_ref.at[layer], wout16_ref, rows, D_MODEL))
        _load_weights_bf16(plan, slots, sem_ref)

    @pl.when((s >= 1) & (s <= n_tiles))
    def _():
        i = (s - 1) % tiles_per_seq

        @pl.when(i == 0)
        def _():
            uext_ref[0:POOL_HIST, :] = jnp.zeros((POOL_HIST, POOL_WIDTH), F32)

        x = x_ref[...]
        xg = (x * g_ref[...]).astype(BF16)
        inv_rms = lax.rsqrt(jnp.mean(x * x, axis=-1, keepdims=True) + RMS_EPS)
        u = _dot(xg, win16_ref[:, :POOL_WIDTH]) * inv_rms
        z = _dot(xg, win16_ref[:, POOL_WIDTH:]) * inv_rms
        uext_ref[POOL_HIST:POOL_HIST + tm, :] = u

        pos = i * tm + lax.broadcasted_iota(jnp.int32, (tm, 1), 0)
        sums, inv_cnt = [], []
        for g, w in enumerate(POOL_WINDOWS):
            cols = slice(g * POOL_GROUP, (g + 1) * POOL_GROUP)
            acc = u[:, cols]
            for k in range(1, w):
                acc = acc + uext_ref[POOL_HIST - k:POOL_HIST - k + tm, cols]
            sums.append(acc)
            inv_cnt.append(1.0 / jnp.minimum(pos + 1, w).astype(F32))
        out_ref[...] = _pool_tail(x_ref, u, z, sums, inv_cnt, wgrp16_ref, scale_ref, wout16_ref)

        tail = uext_ref[tm:tm + POOL_HIST, :]
        hist_ref[0] = tail
        uext_ref[0:POOL_HIST, :] = tail

        for src, dst in zip(conv_src, conv_dst):
            dst[...] = src[...].astype(BF16)

    @pl.when(s == n_tiles + 1)
    def _():
        h = _rms(xs_ref[...], g_ref[...]).astype(BF16)
        u = _dot(h, win16_ref[:, :POOL_WIDTH])
        z = _dot(h, win16_ref[:, POOL_WIDTH:])
        for r in range(POOL_BUF - 1):
            newst_ref[r] = st_ref[0, r + 1]
        newst_ref[POOL_BUF - 1] = u
        sums, inv_cnt = [], []
        for g, w in enumerate(POOL_WINDOWS):
            cols = slice(g * POOL_GROUP, (g + 1) * POOL_GROUP)
            acc = u[:, cols]
            for k in range(1, w):
                acc = acc + st_ref[0, POOL_BUF - k, :, cols]
            sums.append(acc)
            inv_cnt.append(1.0 / float(min(PAST_LEN + 1, w)))
        outs_ref[...] = _pool_tail(xs_ref, u, z, sums, inv_cnt, wgrp16_ref, scale_ref, wout16_ref)


def _convert_specs(convert, n_tiles):
    in_specs, out_specs, out_shape = [], [], []
    for arr, lyr in convert:
        _, rows, cols = arr.shape
        step_rows = rows // n_tiles
        in_specs.append(pl.BlockSpec((None, step_rows, cols),
                                     lambda s, lyr=lyr: (lyr, _tile_of(s, n_tiles), 0)))
        out_specs.append(pl.BlockSpec((step_rows, cols), lambda s: (_tile_of(s, n_tiles), 0)))
        out_shape.append(jax.ShapeDtypeStruct((rows, cols), BF16))
    return in_specs, out_specs, out_shape


def _pool_layer(x2d, xs, state, g, weights, scale, *, layer, own_bf16, convert, seq, tm):
    rows = x2d.shape[0]
    n_dec = xs.shape[0]
    n_tiles = rows // tm
    tiles_per_seq = seq // tm
    tile = lambda s: (_tile_of(s, n_tiles), 0)
    one = pl.Buffered(1)
    hbm = pl.BlockSpec(memory_space=pl.ANY)
    conv_in, conv_out, conv_shape = _convert_specs(convert, n_tiles)
    w_in, w_grp, w_out = weights
    return pl.pallas_call(
        functools.partial(_pool_kernel, layer=layer, tm=tm, tiles_per_seq=tiles_per_seq, n_tiles=n_tiles,
                          own_bf16=own_bf16, n_conv=len(convert)),
        grid=(n_tiles + 2,),
        in_specs=[
            pl.BlockSpec((tm, D_MODEL), tile),
            _resident((n_dec, D_MODEL)),
            pl.BlockSpec((1, POOL_BUF, n_dec, POOL_WIDTH), lambda s: (layer, 0, 0, 0), pipeline_mode=one),
            pl.BlockSpec((None, 1, D_MODEL), lambda s: (layer, 0, 0), pipeline_mode=one),
            hbm,
            hbm,
            pl.BlockSpec((None, 1, POOL_WIDTH), lambda s: (layer, 0, 0), pipeline_mode=one),
            hbm,
        ] + conv_in,
        out_specs=[
            pl.BlockSpec((tm, D_MODEL), tile),
            pl.BlockSpec((1, POOL_HIST, POOL_WIDTH),
                         lambda s: (_tile_of(s, n_tiles) // tiles_per_seq, 0, 0)),
            _resident((n_dec, D_MODEL)),
            _resident((POOL_BUF, n_dec, POOL_WIDTH)),
        ] + conv_out,
        out_shape=[
            jax.ShapeDtypeStruct((rows, D_MODEL), F32),
            jax.ShapeDtypeStruct((rows // seq, POOL_HIST, POOL_WIDTH), F32),
            jax.ShapeDtypeStruct((n_dec, D_MODEL), F32),
            jax.ShapeDtypeStruct((POOL_BUF, n_dec, POOL_WIDTH), F32),
        ] + conv_shape,
        scratch_shapes=[
            pltpu.VMEM((D_MODEL, 2 * POOL_WIDTH), BF16),
            pltpu.VMEM((N_POOL_GROUPS * POOL_GROUP, POOL_GROUP), BF16),
            pltpu.VMEM((POOL_WIDTH, D_MODEL), BF16),
            pltpu.VMEM((POOL_HIST + tm, POOL_WIDTH), F32),
            pltpu.SemaphoreType.DMA((4,)),
        ],
        compiler_params=pltpu.CompilerParams(
            dimension_semantics=("arbitrary",), vmem_limit_bytes=VMEM_LIMIT),
        name="pool_layer",
    )(x2d, xs, state, g, w_in, w_grp, scale, w_out, *[arr for arr, _ in convert])


def _kv_project(x, g_ref, w16_ref, kg_ref):
    h = _rms(x, g_ref[...]).astype(BF16)
    kv = _dot(h, w16_ref[...])
    pair_lo = lax.broadcasted_iota(jnp.int32, (1, PAIR), 1) < HEAD_DIM
    k_pairs = []
    for p in range(KV_WIDTH // PAIR):
        kp = kv[:, p * PAIR:(p + 1) * PAIR]
        k_pairs.append(kp * _pair_inv_rms(kp, pair_lo, 1.0) * kg_ref[...])
    return k_pairs, kv[:, KV_WIDTH:]


def _dup_heads(col, pair_lo):
    rolled = pltpu.roll(col, HEAD_DIM, 1)
    return jnp.where(pair_lo, col, rolled), jnp.where(pair_lo, rolled, col)


def _slide_window(cache_ref, new_t, win_ref, win16_ref):
    n, _, n_keys = cache_ref.shape
    last = lax.broadcasted_iota(jnp.int32, (1, n_keys), 1) == n_keys - 1
    for b in range(n):
        slid = jnp.where(last, new_t[:, b:b + 1], pltpu.roll(cache_ref[b], n_keys - 1, 1))
        win_ref[b] = slid
        win16_ref[b] = slid.astype(BF16)


def _kv_kernel(x_ref, xs_ref, ck_ref, cv_ref, g_ref, wkv_ref, kg_ref,
               kt_ref, vb_ref, kwin_ref, vwin_ref, kws_ref, vws_ref, k16_ref, v16_ref,
               w16_ref, pk_ref, pv_ref, *, tm, chunk, tiles_per_seq, dec_steps):
    s = pl.program_id(0)
    nblk = tm // BLOCK
    blk_per_chunk = chunk // BLOCK
    pair_lo = lax.broadcasted_iota(jnp.int32, (1, PAIR), 1) < HEAD_DIM

    @pl.when(s == 0)
    def _():
        w16_ref[...] = wkv_ref[...].astype(BF16)

    @pl.when(s % tiles_per_seq == 0)
    def _():
        pk_ref[...] = jnp.zeros(pk_ref.shape, BF16)
        pv_ref[...] = jnp.zeros(pv_ref.shape, BF16)

    @pl.when(s < dec_steps)
    def _():
        k_pairs, v = _kv_project(xs_ref[...], g_ref, w16_ref, kg_ref)
        n_dec = xs_ref.shape[0]
        pad = jnp.zeros((BLOCK - n_dec, KV_WIDTH), F32)
        kn_t = jnp.concatenate([jnp.concatenate(k_pairs, axis=1), pad], axis=0)
        vn_t = jnp.concatenate([v, pad], axis=0)
        kn_t = jnp.concatenate([kn_t[:, c * BLOCK:(c + 1) * BLOCK].T for c in range(KV_WIDTH // BLOCK)], axis=0)
        vn_t = jnp.concatenate([vn_t[:, c * BLOCK:(c + 1) * BLOCK].T for c in range(KV_WIDTH // BLOCK)], axis=0)
        _slide_window(ck_ref, kn_t, kws_ref, k16_ref)
        _slide_window(cv_ref, vn_t, vws_ref, v16_ref)

    for ch in range(tm // chunk):
        k_pairs, v = _kv_project(x_ref[ch * chunk:(ch + 1) * chunk, :], g_ref, w16_ref, kg_ref)
        for jc in range(blk_per_chunk):
            jb = ch * blk_per_chunk + jc
            rows = slice(jc * BLOCK, (jc + 1) * BLOCK)
            for p in range(KV_WIDTH // PAIR):
                k_dup = _dup_heads(k_pairs[p][rows], pair_lo)
                v_dup = _dup_heads(v[rows, p * PAIR:(p + 1) * PAIR], pair_lo)
                for half in range(2):
                    kh = 2 * p + half
                    cols = slice(kh * PAIR, (kh + 1) * PAIR)
                    kt_f32 = k_dup[half].T
                    kt_cur = kt_f32.astype(BF16)
                    v_cur = v_dup[half].astype(BF16)
                    if jb == nblk - 1:
                        hd = slice(kh * HEAD_DIM, (kh + 1) * HEAD_DIM)
                        kwin_ref[0, hd, :] = kt_f32[:HEAD_DIM]
                        vwin_ref[0, hd, :] = v_dup[half].T[:HEAD_DIM]
                    kt_ref[0, jb, kh, :, 0:BLOCK] = pk_ref[kh]
                    kt_ref[0, jb, kh, :, BLOCK:2 * BLOCK] = kt_cur
                    vb_ref[0, jb, 0:BLOCK, cols] = pv_ref[:, cols]
                    vb_ref[0, jb, BLOCK:2 * BLOCK, cols] = v_cur
                    pk_ref[kh] = kt_cur
                    pv_ref[:, cols] = v_cur


def _shared_kv(x2d, xs, cache_kt, cache_vt, g, w_kv, k_gain_pair, *, seq, tm, chunk, dec_rows):
    rows = x2d.shape[0]
    bsz = rows // seq
    n_dec = xs.shape[0]
    n_tiles = rows // tm
    tiles_per_seq = seq // tm
    nblk = tm // BLOCK
    dec_steps = n_dec // dec_rows
    assert dec_steps <= n_tiles and dec_steps * dec_rows == n_dec
    dec = lambda s: (jnp.minimum(s, dec_steps - 1), 0, 0)
    win = (dec_rows,) + cache_kt.shape[1:]
    return pl.pallas_call(
        functools.partial(_kv_kernel, tm=tm, chunk=chunk, tiles_per_seq=tiles_per_seq, dec_steps=dec_steps),
        grid=(n_tiles,),
        in_specs=[
            pl.BlockSpec((tm, D_MODEL), lambda s: (s, 0)),
            pl.BlockSpec((dec_rows, D_MODEL), lambda s: (jnp.minimum(s, dec_steps - 1), 0)),
            pl.BlockSpec(win, dec),
            pl.BlockSpec(win, dec),
            _resident((1, D_MODEL)),
            _resident((D_MODEL, 2 * KV_WIDTH)),
            _resident((1, PAIR)),
        ],
        out_specs=[
            pl.BlockSpec((1, nblk, N_KV_HEADS, PAIR, 2 * BLOCK),
                         lambda s: (s // tiles_per_seq, s % tiles_per_seq, 0, 0, 0)),
            pl.BlockSpec((1, nblk, 2 * BLOCK, N_KV_HEADS * PAIR),
                         lambda s: (s // tiles_per_seq, s % tiles_per_seq, 0, 0)),
            pl.BlockSpec((1, KV_WIDTH, BLOCK), lambda s: (s // tiles_per_seq, 0, 0)),
            pl.BlockSpec((1, KV_WIDTH, BLOCK), lambda s: (s // tiles_per_seq, 0, 0)),
            pl.BlockSpec(win, dec),
            pl.BlockSpec(win, dec),
            pl.BlockSpec(win, dec),
            pl.BlockSpec(win, dec),
        ],
        out_shape=[
            jax.ShapeDtypeStruct((bsz, seq // BLOCK, N_KV_HEADS, PAIR, 2 * BLOCK), BF16),
            jax.ShapeDtypeStruct((bsz, seq // BLOCK, 2 * BLOCK, N_KV_HEADS * PAIR), BF16),
            jax.ShapeDtypeStruct((bsz, KV_WIDTH, BLOCK), F32),
            jax.ShapeDtypeStruct((bsz, KV_WIDTH, BLOCK), F32),
            jax.ShapeDtypeStruct(cache_kt.shape, F32),
            jax.ShapeDtypeStruct(cache_vt.shape, F32),
            jax.ShapeDtypeStruct(cache_kt.shape, BF16),
            jax.ShapeDtypeStruct(cache_vt.shape, BF16),
        ],
        scratch_shapes=[
            pltpu.VMEM((D_MODEL, 2 * KV_WIDTH), BF16),
            pltpu.VMEM((N_KV_HEADS, PAIR, BLOCK), BF16),
            pltpu.VMEM((BLOCK, N_KV_HEADS * PAIR), BF16),
        ],
        compiler_params=pltpu.CompilerParams(
            dimension_semantics=("arbitrary",), vmem_limit_bytes=VMEM_LIMIT),
        name="shared_kv",
    )(x2d, xs, cache_kt, cache_vt, g, w_kv, k_gain_pair)


def _normed_q_pair(q, pr, qg_ref, pair_lo, scale):
    cols = slice(pr * PAIR, (pr + 1) * PAIR)
    qp = q[:, cols]
    return qp * _pair_inv_rms(qp, pair_lo, scale) * qg_ref[:, cols]


def _decode_attention(q, k3, v3, sink_col):
    n, n_keys = k3.shape[0], k3.shape[2]
    gw = GQA_GROUP * HEAD_DIM
    rows = n * GQA_GROUP
    ri = lax.broadcasted_iota(jnp.int32, (rows, n), 0)
    ci = lax.broadcasted_iota(jnp.int32, (rows, n), 1)
    rep = jnp.where(ri // GQA_GROUP == ci, 1.0, 0.0).astype(BF16)
    row_g = lax.broadcasted_iota(jnp.int32, (rows, gw), 0) % GQA_GROUP
    col_g = lax.broadcasted_iota(jnp.int32, (rows, gw), 1) // HEAD_DIM
    own = row_g == col_g
    fj = lax.broadcasted_iota(jnp.int32, (gw, KV_WIDTH), 0)
    fc = lax.broadcasted_iota(jnp.int32, (gw, KV_WIDTH), 1)
    tc = lax.broadcasted_iota(jnp.int32, (KV_WIDTH, gw), 0)
    tj = lax.broadcasted_iota(jnp.int32, (KV_WIDTH, gw), 1)
    qb = q.astype(BF16)

    q_groups = []
    for kh in range(N_KV_HEADS):
        to_kv = jnp.where((fj % HEAD_DIM == fc % HEAD_DIM) & (fc // HEAD_DIM == kh), 1.0, 0.0).astype(BF16)
        mine = jnp.where(own, _dot(rep, qb[:, kh * gw:(kh + 1) * gw]), 0.0).astype(BF16)
        q_groups.append(_dot(mine, to_kv).astype(BF16).reshape(n, GQA_GROUP, KV_WIDTH))
    qm3 = jnp.concatenate(q_groups, axis=1)
    s = jnp.einsum("bhc,bck->bhk", qm3, k3, preferred_element_type=F32)
    h_col = lax.broadcasted_iota(jnp.int32, (N_HEADS, 1), 0).astype(F32)
    slope = jnp.exp2(-8.0 * (h_col + 1.0) / N_HEADS)[None]
    dist = (n_keys - 1 - lax.broadcasted_iota(jnp.int32, (1, n_keys), 1)).astype(F32)[None]
    s = s - slope * dist
    sink = sink_col[None]
    m = jnp.maximum(jnp.max(s, axis=-1, keepdims=True), sink)
    p = jnp.exp(s - m)
    den = jnp.sum(p, axis=-1, keepdims=True) + jnp.exp(sink - m)
    o3 = jnp.einsum("bhk,bck->bhc", p.astype(BF16), v3, preferred_element_type=F32) / den
    outs = []
    for kh in range(N_KV_HEADS):
        from_kv = jnp.where((tj % HEAD_DIM == tc % HEAD_DIM) & (tc // HEAD_DIM == kh), 1.0, 0.0).astype(BF16)
        o2 = o3[:, kh * GQA_GROUP:(kh + 1) * GQA_GROUP, :].reshape(rows, KV_WIDTH)
        hi = o2.astype(BF16)
        lo = (o2 - hi.astype(F32)).astype(BF16)
        spread = jnp.where(own, _dot(hi, from_kv) + _dot(lo, from_kv), 0.0)
        outs.append(jnp.sum(spread.reshape(n, GQA_GROUP, gw), axis=1))
    return jnp.concatenate(outs, axis=-1)


def _attn_kernel(sink_ref, x_ref, xs_ref, k3_ref, v3_ref, sinkc_ref, g_ref, win_ref, qg_ref,
                 kt_ref, vb_ref, wout_ref,
                 out_ref, outs_ref,
                 win16_ref, wout16_ref, h_ref, qlo_ref, qhi_ref, sem_ref,
                 *, layer, tm, tiles_per_seq, n_tiles):
    s = pl.program_id(0)
    nblk = tm // BLOCK
    pair_lo = lax.broadcasted_iota(jnp.int32, (1, PAIR), 1) < HEAD_DIM

    @pl.when(s == 0)
    def _():
        _fetch_bf16([(win_ref, win16_ref), (wout_ref, wout16_ref)], sem_ref)

    @pl.when((s >= 1) & (s <= n_tiles))
    def _():
        i = (s - 1) % tiles_per_seq
        h_ref[...] = _rms(x_ref[...], g_ref[...]).astype(BF16)
        for kh in range(N_KV_HEADS):
            gcols = slice(kh * GQA_GROUP * HEAD_DIM, (kh + 1) * GQA_GROUP * HEAD_DIM)
            q = _dot(h_ref[...], win16_ref[:, gcols])
            for pr in range(GQA_GROUP // 2):
                cols = slice(pr * PAIR, (pr + 1) * PAIR)
                ocols = slice((kh * (GQA_GROUP // 2) + pr) * PAIR, (kh * (GQA_GROUP // 2) + pr + 1) * PAIR)
                qp = q[:, cols]
                qn = qp * _pair_inv_rms(qp, pair_lo, HEAD_DIM ** -0.5 * LOG2E) * qg_ref[:, ocols]
                qlo_ref[:, ocols] = jnp.where(pair_lo, qn, 0.0).astype(BF16)
                qhi_ref[:, ocols] = jnp.where(pair_lo, 0.0, qn).astype(BF16)

        r_idx = lax.broadcasted_iota(jnp.int32, (BLOCK, 2 * BLOCK), 0)
        c_idx = lax.broadcasted_iota(jnp.int32, (BLOCK, 2 * BLOCK), 1)
        dist = r_idx + BLOCK - c_idx
        in_window = (dist >= 0) & (dist < WINDOW)
        neg_dist = -dist.astype(F32)

        for blk in range(nblk):
            rows = slice(blk * BLOCK, (blk + 1) * BLOCK)
            first_key = (i * nblk + blk - 1) * BLOCK
            valid = in_window & (first_key + c_idx >= 0)
            nd = jnp.where(valid, neg_dist, NEG_INF / MIN_SLOPE)
            for kh in range(N_KV_HEADS):
                kt = kt_ref[0, blk, kh]
                vdup = vb_ref[0, blk, :, kh * PAIR:(kh + 1) * PAIR]
                pieces = []
                for pr in range(GQA_GROUP // 2):
                    cols = slice((kh * (GQA_GROUP // 2) + pr) * PAIR,
                                 (kh * (GQA_GROUP // 2) + pr + 1) * PAIR)
                    pieces.append(qlo_ref[rows, cols])
                    pieces.append(qhi_ref[rows, cols])
                s_all = _dot(jnp.concatenate(pieces, axis=0), kt)
                probs, inv_den = [], []
                for g_idx in range(GQA_GROUP):
                    head = kh * GQA_GROUP + g_idx
                    slope2 = 2.0 ** (-8.0 * (head + 1) / N_HEADS) * LOG2E
                    sc = s_all[g_idx * BLOCK:(g_idx + 1) * BLOCK] + slope2 * nd
                    sink2 = sink_ref[layer, head] * LOG2E
                    m = jnp.maximum(jnp.max(sc, axis=-1, keepdims=True), sink2)
                    p = jnp.exp2(sc - m)
                    den = jnp.sum(p, axis=-1, keepdims=True) + jnp.exp2(sink2 - m)
                    probs.append(p.astype(BF16))
                    inv_den.append(1.0 / den)
                o_all = _dot(jnp.concatenate(probs, axis=0), vdup)
                for pr in range(GQA_GROUP // 2):
                    cols = slice((kh * (GQA_GROUP // 2) + pr) * PAIR,
                                 (kh * (GQA_GROUP // 2) + pr + 1) * PAIR)
                    o_even = o_all[(2 * pr) * BLOCK:(2 * pr + 1) * BLOCK] * inv_den[2 * pr]
                    o_odd = o_all[(2 * pr + 1) * BLOCK:(2 * pr + 2) * BLOCK] * inv_den[2 * pr + 1]
                    out_ref[rows, cols] = jnp.where(pair_lo, o_even, o_odd)

        z = _dot(h_ref[...], win16_ref[:, ATTN_WIDTH:])
        y = out_ref[...] * _silu(z)
        out_ref[...] = x_ref[...] + _dot(y.astype(BF16), wout16_ref[...])

    @pl.when(s == n_tiles + 1)
    def _():
        x = xs_ref[...]
        h = _rms(x, g_ref[...]).astype(BF16)
        q = _dot(h, win16_ref[:, :ATTN_WIDTH])
        z = _dot(h, win16_ref[:, ATTN_WIDTH:])
        qn = jnp.concatenate(
            [_normed_q_pair(q, pr, qg_ref, pair_lo, HEAD_DIM ** -0.5) for pr in range(N_PAIRS)], axis=-1)
        o = _decode_attention(qn, k3_ref[...], v3_ref[...], sinkc_ref[:, layer:layer + 1])
        y = o * _silu(z)
        outs_ref[...] = x + _dot(y.astype(BF16), wout16_ref[...])


def _attn_layer(x2d, xs, k3, v3, sinks, sinks_t, g, w_in, q_gain, kt_band, v_band, w_out,
                *, layer, seq, tm):
    rows = x2d.shape[0]
    n_dec = xs.shape[0]
    n_tiles = rows // tm
    tiles_per_seq = seq // tm
    nblk = tm // BLOCK
    tile = lambda s: (_tile_of(s, n_tiles), 0)
    band = lambda s: (_tile_of(s, n_tiles) // tiles_per_seq, _tile_of(s, n_tiles) % tiles_per_seq)
    one = pl.Buffered(1)
    hbm = pl.BlockSpec(memory_space=pl.ANY)
    return pl.pallas_call(
        functools.partial(_attn_kernel, layer=layer, tm=tm, tiles_per_seq=tiles_per_seq, n_tiles=n_tiles),
        grid=(n_tiles + 2,),
        in_specs=[
            pl.BlockSpec(memory_space=pltpu.SMEM),
            pl.BlockSpec((tm, D_MODEL), tile),
            _resident((n_dec, D_MODEL)),
            _resident(k3.shape),
            _resident(v3.shape),
            _resident(sinks_t.shape),
            pl.BlockSpec((None, 1, D_MODEL), lambda s: (layer, 0, 0), pipeline_mode=one),
            hbm,
            pl.BlockSpec((None, 1, ATTN_WIDTH), lambda s: (layer, 0, 0), pipeline_mode=one),
            pl.BlockSpec((1, nblk, N_KV_HEADS, PAIR, 2 * BLOCK), lambda s: band(s) + (0, 0, 0)),
            pl.BlockSpec((1, nblk, 2 * BLOCK, N_KV_HEADS * PAIR), lambda s: band(s) + (0, 0)),
            hbm,
        ],
        out_specs=[
            pl.BlockSpec((tm, D_MODEL), tile),
            _resident((n_dec, D_MODEL)),
        ],
        out_shape=[
            jax.ShapeDtypeStruct((rows, D_MODEL), F32),
            jax.ShapeDtypeStruct((n_dec, D_MODEL), F32),
        ],
        scratch_shapes=[
            pltpu.VMEM((D_MODEL, 2 * ATTN_WIDTH), BF16),
            pltpu.VMEM((ATTN_WIDTH, D_MODEL), BF16),
            pltpu.VMEM((tm, D_MODEL), BF16),
            pltpu.VMEM((tm, ATTN_WIDTH), BF16),
            pltpu.VMEM((tm, ATTN_WIDTH), BF16),
            pltpu.SemaphoreType.DMA((2,)),
        ],
        compiler_params=pltpu.CompilerParams(
            dimension_semantics=("arbitrary",), vmem_limit_bytes=VMEM_LIMIT),
        name="attn_layer",
    )(sinks, x2d, xs, k3, v3, sinks_t, g, w_in, q_gain, kt_band, v_band, w_out)


def kernel(x_prompt, x_sample, state_pool, cache_k_win, cache_v_win, norm_a, w_in_a, w_grp_a, scale_a, w_out_a, norm_kv, w_kv, k_norm, norm_b, w_in_b, q_norm, sinks, w_out_b):
    bsz, seq, _ = x_prompt.shape
    n_dec = x_sample.shape[0]
    assert x_sample.shape[1] == 1 and cache_k_win.shape[1] == WINDOW
    tm_a, tm_b, tm_kv = 256, 256, 1024

    hp = x_prompt.reshape(bsz * seq, D_MODEL)
    hs = x_sample[:, 0]
    state_t = state_pool.transpose(0, 2, 1, 3)
    w_grp_rows = w_grp_a.reshape(N_A_LAYERS, N_POOL_GROUPS * POOL_GROUP, POOL_GROUP)

    assert N_A_LAYERS == 2 and N_B_LAYERS == 2
    pool_args = (hp, hs, state_t, norm_a[:, None])
    hp, hist0, hs, state0, a1_in, a1_grp, a1_out, b1_in, b1_out = _pool_layer(
        *pool_args, (w_in_a, w_grp_rows, w_out_a), scale_a[:, None], layer=0, own_bf16=False,
        convert=[(w_in_a, 1), (w_grp_rows, 1), (w_out_a, 1), (w_in_b, 1), (w_out_b, 1)], seq=seq, tm=tm_a)
    hp, hist1, hs, state1, b0_in, b0_out = _pool_layer(
        hp, hs, state_t, norm_a[:, None], (a1_in, a1_grp, a1_out), scale_a[:, None], layer=1, own_bf16=True,
        convert=[(w_in_b, 0), (w_out_b, 0)], seq=seq, tm=tm_a)
    pool_p = [h[:, POOL_HIST - POOL_BUF:] for h in (hist0, hist1)]
    pool_s = [state0, state1]
    attn_weights = [(b0_in, b0_out), (b1_in, b1_out)]

    def keys_minor(t):
        return t.transpose(0, 2, 3, 1).reshape(t.shape[0], KV_WIDTH, t.shape[1])

    def keys_major(t):
        return t.reshape(t.shape[0], N_KV_HEADS, HEAD_DIM, t.shape[2]).transpose(0, 3, 1, 2)

    kt_band, v_band, k_win_p, v_win_p, k_win_s, v_win_s, k16, v16 = _shared_kv(
        hp, hs, keys_minor(cache_k_win), keys_minor(cache_v_win),
        norm_kv[None], w_kv, jnp.tile(k_norm, 2)[None], seq=seq, tm=tm_kv, chunk=256, dec_rows=8)

    q_gain = jnp.tile(q_norm, (1, N_HEADS))[:, None]
    for j in range(N_B_LAYERS):
        hp, hs = _attn_layer(hp, hs, k16, v16, sinks, sinks.T, norm_b[:, None], attn_weights[j][0], q_gain,
                             kt_band, v_band, attn_weights[j][1], layer=j, seq=seq, tm=tm_b)

    return (hp.reshape(bsz, seq, D_MODEL), hs[:, None],
            jnp.stack(pool_p, axis=0), jnp.stack(pool_s, axis=0).transpose(0, 2, 1, 3),
            keys_major(k_win_p), keys_major(v_win_p), keys_major(k_win_s), keys_major(v_win_s))
```

```python
import functools
import math

import jax
import jax.numpy as jnp
from jax import lax
from jax.experimental import pallas as pl
from jax.experimental.pallas import tpu as pltpu

D_MODEL = 2048
N_A_LAYERS = 2
N_B_LAYERS = 2
POOL_WINDOWS = (2, 4, 8, 16)
N_POOL_GROUPS = len(POOL_WINDOWS)
POOL_WIDTH = D_MODEL
POOL_GROUP = POOL_WIDTH // N_POOL_GROUPS
POOL_BUF = max(POOL_WINDOWS) - 1
POOL_HIST = POOL_BUF + 1
HEAD_DIM = 64
N_HEADS = D_MODEL // HEAD_DIM
N_KV_HEADS = N_HEADS // 8
GQA_GROUP = N_HEADS // N_KV_HEADS
ATTN_WIDTH = N_HEADS * HEAD_DIM
KV_WIDTH = N_KV_HEADS * HEAD_DIM
WINDOW = 128
BLOCK = WINDOW
PAST_LEN = 16384
RMS_EPS = 1e-6
NEG_INF = -1e30
LOG2E = math.log2(math.e)
MIN_SLOPE = 2.0 ** -8
PAIR = 2 * HEAD_DIM
N_PAIRS = N_HEADS // 2
VMEM_LIMIT = 62 * 1024 * 1024

BF16 = jnp.bfloat16
F32 = jnp.float32


def _rms(x, g):
    return x * lax.rsqrt(jnp.mean(x * x, axis=-1, keepdims=True) + RMS_EPS) * g


def _silu(z):
    return z * jax.nn.sigmoid(z)


def _dot(a, b):
    return jnp.dot(a, b, preferred_element_type=F32)


def _resident(shape):
    return pl.BlockSpec(shape, lambda *_: (0,) * len(shape), pipeline_mode=pl.Buffered(1))


def _tile_of(s, n_tiles):
    return jnp.clip(s - 1, 0, n_tiles - 1)


def _weight_blocks(src, dst16, block_rows, block_cols):
    rows, cols = dst16.shape
    plan = []
    for r in range(0, rows, block_rows):
        for c in range(0, cols, block_cols):
            w = min(block_cols, cols - c)
            plan.append((src.at[pl.ds(r, block_rows), pl.ds(c, w)],
                         dst16.at[pl.ds(r, block_rows), pl.ds(c, w)], (block_rows, w)))
    return plan


def _load_weights_bf16(plan, slots, sem):
    n_slots = len(slots)
    in_flight = n_slots - 1

    def copy(c):
        src, _, (r, w) = plan[c]
        k = c % n_slots
        return pltpu.make_async_copy(src, slots[k].at[pl.ds(0, r), pl.ds(0, w)], sem.at[k])

    for c in range(min(in_flight, len(plan))):
        copy(c).start()
    for c in range(len(plan)):
        if c + in_flight < len(plan):
            copy(c + in_flight).start()
        copy(c).wait()
        _, dst, (r, w) = plan[c]
        dst[...] = slots[c % n_slots][0:r, 0:w].astype(BF16)


def _fetch_bf16(pairs, sem):
    copies = [pltpu.make_async_copy(src, dst, sem.at[i]) for i, (src, dst) in enumerate(pairs)]
    for cp in copies:
        cp.start()
    for cp in copies:
        cp.wait()


def _half_tiles(*refs):
    slots = []
    for ref in refs:
        half = ref.shape[0] // 2
        slots += [ref.at[pl.ds(0, half)], ref.at[pl.ds(half, half)]]
    return slots


def _pair_inv_rms(pair, pair_lo, scale):
    sq = pair * pair
    lo = jnp.sum(jnp.where(pair_lo, sq, 0.0), axis=-1, keepdims=True)
    hi = jnp.sum(jnp.where(pair_lo, 0.0, sq), axis=-1, keepdims=True)
    ms = jnp.where(pair_lo, lo, hi) * (1.0 / HEAD_DIM)
    return lax.rsqrt(ms + RMS_EPS) * scale


def _pool_tail(x_ref, u, z, window_sums, inv_cnt, wgrp_ref, scale_ref, wout_ref):
    mixed = []
    for g in range(N_POOL_GROUPS):
        cols = slice(g * POOL_GROUP, (g + 1) * POOL_GROUP)
        p = window_sums[g] * inv_cnt[g] - u[:, cols]
        mixed.append(_dot(p.astype(BF16), wgrp_ref[cols, :]))
    y = jnp.concatenate(mixed, axis=-1) * scale_ref[...] * _silu(z)
    return x_ref[...] + _dot(y.astype(BF16), wout_ref[...])


def _pool_kernel(*refs, layer, tm, tiles_per_seq, n_tiles, own_bf16, n_conv):
    (x_ref, xs_ref, st_ref, g_ref, win_ref, wgrp_ref, scale_ref, wout_ref) = refs[:8]
    conv_src = refs[8:8 + n_conv]
    (out_ref, hist_ref, outs_ref, newst_ref) = refs[8 + n_conv:12 + n_conv]
    conv_dst = refs[12 + n_conv:12 + 2 * n_conv]
    (win16_ref, wgrp16_ref, wout16_ref, uext_ref, sem_ref) = refs[12 + 2 * n_conv:]
    s = pl.program_id(0)

    @pl.when(s == 0)
    def _():
        if own_bf16:
            _fetch_bf16([(win_ref, win16_ref), (wgrp_ref, wgrp16_ref), (wout_ref, wout16_ref)], sem_ref)
            return
        slots = _half_tiles(out_ref, uext_ref.at[pl.ds(POOL_HIST, tm)])
        rows = tm // 2
        assert POOL_GROUP // rows == len(slots) and N_POOL_GROUPS * POOL_GROUP == D_MODEL
        grp = [(wgrp_ref.at[layer, pl.ds(g * POOL_GROUP + r * rows, rows)],
                slots[r].at[:, pl.ds(g * POOL_GROUP, POOL_GROUP)],
                wgrp16_ref.at[pl.ds(g * POOL_GROUP + r * rows, rows)])
               for g in range(N_POOL_GROUPS) for r in range(len(slots))]
        grp_copies = [pltpu.make_async_copy(src, stage, sem_ref.at[0]) for src, stage, _ in grp]
        for cp in grp_copies:
            cp.start()
        for cp in grp_copies:
            cp.wait()
        for _, stage, dst in grp:
            dst[...] = stage[...].astype(BF16)
        plan = (_weight_blocks(win_ref.at[layer], win16_ref, rows, D_MODEL)
                + _weight_blocks(wout_ref.at[layer], wout16_ref, rows, D_MODEL))
        _load_weights_bf16(plan, slots, sem_ref)

    @pl.when((s >= 1) & (s <= n_tiles))
    def _():
        i = (s - 1) % tiles_per_seq

        @pl.when(i == 0)
        def _():
            uext_ref[0:POOL_HIST, :] = jnp.zeros((POOL_HIST, POOL_WIDTH), F32)

        x = x_ref[...]
        xg = (x * g_ref[...]).astype(BF16)
        inv_rms = lax.rsqrt(jnp.mean(x * x, axis=-1, keepdims=True) + RMS_EPS)
        u = _dot(xg, win16_ref[:, :POOL_WIDTH]) * inv_rms
        z = _dot(xg, win16_ref[:, POOL_WIDTH:]) * inv_rms
        uext_ref[POOL_HIST:POOL_HIST + tm, :] = u

        pos = i * tm + lax.broadcasted_iota(jnp.int32, (tm, 1), 0)
        sums, inv_cnt = [], []
        for g, w in enumerate(POOL_WINDOWS):
            cols = slice(g * POOL_GROUP, (g + 1) * POOL_GROUP)
            acc = u[:, cols]
            for k in range(1, w):
                acc = acc + uext_ref[POOL_HIST - k:POOL_HIST - k + tm, cols]
            sums.append(acc)
            inv_cnt.append(1.0 / jnp.minimum(pos + 1, w).astype(F32))
        out_ref[...] = _pool_tail(x_ref, u, z, sums, inv_cnt, wgrp16_ref, scale_ref, wout16_ref)

        tail = uext_ref[tm:tm + POOL_HIST, :]
        hist_ref[0] = tail
        uext_ref[0:POOL_HIST, :] = tail

        for src, dst in zip(conv_src, conv_dst):
            dst[...] = src[...].astype(BF16)

    @pl.when(s == n_tiles + 1)
    def _():
        h = _rms(xs_ref[...], g_ref[...]).astype(BF16)
        u = _dot(h, win16_ref[:, :POOL_WIDTH])
        z = _dot(h, win16_ref[:, POOL_WIDTH:])
        for r in range(POOL_BUF - 1):
            newst_ref[r] = st_ref[0, r + 1]
        newst_ref[POOL_BUF - 1] = u
        sums, inv_cnt = [], []
        for g, w in enumerate(POOL_WINDOWS):
            cols = slice(g * POOL_GROUP, (g + 1) * POOL_GROUP)
            acc = u[:, cols]
            for k in range(1, w):
                acc = acc + st_ref[0, POOL_BUF - k, :, cols]
            sums.append(acc)
            inv_cnt.append(1.0 / float(min(PAST_LEN + 1, w)))
        outs_ref[...] = _pool_tail(xs_ref, u, z, sums, inv_cnt, wgrp16_ref, scale_ref, wout16_ref)


def _convert_specs(convert, n_tiles):
    in_specs, out_specs, out_shape = [], [], []
    for arr, lyr in convert:
        _, rows, cols = arr.shape
        step_rows = rows // n_tiles
        in_specs.append(pl.BlockSpec((None, step_rows, cols),
                                     lambda s, lyr=lyr: (lyr, _tile_of(s, n_tiles), 0)))
        out_specs.append(pl.BlockSpec((step_rows, cols), lambda s: (_tile_of(s, n_tiles), 0)))
        out_shape.append(jax.ShapeDtypeStruct((rows, cols), BF16))
    return in_specs, out_specs, out_shape


def _pool_layer(x2d, xs, state, g, weights, scale, *, layer, own_bf16, convert, seq, tm):
    rows = x2d.shape[0]
    n_dec = xs.shape[0]
    n_tiles = rows // tm
    tiles_per_seq = seq // tm
    tile = lambda s: (_tile_of(s, n_tiles), 0)
    one = pl.Buffered(1)
    hbm = pl.BlockSpec(memory_space=pl.ANY)
    conv_in, conv_out, conv_shape = _convert_specs(convert, n_tiles)
    w_in, w_grp, w_out = weights
    return pl.pallas_call(
        functools.partial(_pool_kernel, layer=layer, tm=tm, tiles_per_seq=tiles_per_seq, n_tiles=n_tiles,
                          own_bf16=own_bf16, n_conv=len(convert)),
        grid=(n_tiles + 2,),
        in_specs=[
            pl.BlockSpec((tm, D_MODEL), tile),
            _resident((n_dec, D_MODEL)),
            pl.BlockSpec((1, POOL_BUF, n_dec, POOL_WIDTH), lambda s: (layer, 0, 0, 0), pipeline_mode=one),
            pl.BlockSpec((None, 1, D_MODEL), lambda s: (layer, 0, 0), pipeline_mode=one),
            hbm,
            hbm,
            pl.BlockSpec((None, 1, POOL_WIDTH), lambda s: (layer, 0, 0), pipeline_mode=one),
            hbm,
        ] + conv_in,
        out_specs=[
            pl.BlockSpec((tm, D_MODEL), tile),
            pl.BlockSpec((1, POOL_HIST, POOL_WIDTH),
                         lambda s: (_tile_of(s, n_tiles) // tiles_per_seq, 0, 0)),
            _resident((n_dec, D_MODEL)),
            _resident((POOL_BUF, n_dec, POOL_WIDTH)),
        ] + conv_out,
        out_shape=[
            jax.ShapeDtypeStruct((rows, D_MODEL), F32),
            jax.ShapeDtypeStruct((rows // seq, POOL_HIST, POOL_WIDTH), F32),
            jax.ShapeDtypeStruct((n_dec, D_MODEL), F32),
            jax.ShapeDtypeStruct((POOL_BUF, n_dec, POOL_WIDTH), F32),
        ] + conv_shape,
        scratch_shapes=[
            pltpu.VMEM((D_MODEL, 2 * POOL_WIDTH), BF16),
            pltpu.VMEM((N_POOL_GROUPS * POOL_GROUP, POOL_GROUP), BF16),
            pltpu.VMEM((POOL_WIDTH, D_MODEL), BF16),
            pltpu.VMEM((POOL_HIST + tm, POOL_WIDTH), F32),
            pltpu.SemaphoreType.DMA((4,)),
        ],
        compiler_params=pltpu.CompilerParams(
            dimension_semantics=("arbitrary",), vmem_limit_bytes=VMEM_LIMIT),
        name="pool_layer",
    )(x2d, xs, state, g, w_in, w_grp, scale, w_out, *[arr for arr, _ in convert])


def _kv_project(x, g_ref, w16_ref, kg_ref):
    h = _rms(x, g_ref[...]).astype(BF16)
    kv = _dot(h, w16_ref[...])
    pair_lo = lax.broadcasted_iota(jnp.int32, (1, PAIR), 1) < HEAD_DIM
    k_pairs = []
    for p in range(KV_WIDTH // PAIR):
        kp = kv[:, p * PAIR:(p + 1) * PAIR]
        k_pairs.append(kp * _pair_inv_rms(kp, pair_lo, 1.0) * kg_ref[...])
    return k_pairs, kv[:, KV_WIDTH:]


def _dup_heads(col, pair_lo):
    rolled = pltpu.roll(col, HEAD_DIM, 1)
    return jnp.where(pair_lo, col, rolled), jnp.where(pair_lo, rolled, col)


def _slide_window(cache_ref, new_t, win_ref, win16_ref):
    n, _, n_keys = cache_ref.shape
    last = lax.broadcasted_iota(jnp.int32, (1, n_keys), 1) == n_keys - 1
    for b in range(n):
        slid = jnp.where(last, new_t[:, b:b + 1], pltpu.roll(cache_ref[b], n_keys - 1, 1))
        win_ref[b] = slid
        win16_ref[b] = slid.astype(BF16)


def _kv_kernel(x_ref, xs_ref, ck_ref, cv_ref, g_ref, wkv_ref, kg_ref, kgc_ref,
               kt_ref, vb_ref, kwin_ref, vwin_ref, kws_ref, vws_ref, k16_ref, v16_ref,
               w16_ref, wkt16_ref, pk_ref, pv_ref, *, tm, chunk, tiles_per_seq, dec_steps):
    s = pl.program_id(0)
    nblk = tm // BLOCK
    blk_per_chunk = chunk // BLOCK
    pair_lo = lax.broadcasted_iota(jnp.int32, (1, PAIR), 1) < HEAD_DIM

    @pl.when(s == 0)
    def _():
        w16_ref[...] = wkv_ref[...].astype(BF16)
        for c in range(KV_WIDTH // BLOCK):
            cols = slice(c * BLOCK, (c + 1) * BLOCK)
            wkt16_ref[cols, :] = wkv_ref[:, cols].T.astype(BF16)

    @pl.when(s % tiles_per_seq == 0)
    def _():
        pk_ref[...] = jnp.zeros(pk_ref.shape, BF16)
        pv_ref[...] = jnp.zeros(pv_ref.shape, BF16)

    @pl.when(s < dec_steps)
    def _():
        k_pairs, v = _kv_project(xs_ref[...], g_ref, w16_ref, kg_ref)
        n_dec = xs_ref.shape[0]
        pad = jnp.zeros((BLOCK - n_dec, KV_WIDTH), F32)
        kn_t = jnp.concatenate([jnp.concatenate(k_pairs, axis=1), pad], axis=0)
        vn_t = jnp.concatenate([v, pad], axis=0)
        kn_t = jnp.concatenate([kn_t[:, c * BLOCK:(c + 1) * BLOCK].T for c in range(KV_WIDTH // BLOCK)], axis=0)
        vn_t = jnp.concatenate([vn_t[:, c * BLOCK:(c + 1) * BLOCK].T for c in range(KV_WIDTH // BLOCK)], axis=0)
        _slide_window(ck_ref, kn_t, kws_ref, k16_ref)
        _slide_window(cv_ref, vn_t, vws_ref, v16_ref)

    for ch in range(tm // chunk):
        h = _rms(x_ref[ch * chunk:(ch + 1) * chunk, :], g_ref[...]).astype(BF16)
        v = _dot(h, w16_ref[:, KV_WIDTH:])
        k_t = lax.dot_general(wkt16_ref[...], h, (((1,), (1,)), ((), ())),
                              preferred_element_type=F32)
        k_heads = []
        for kh in range(N_KV_HEADS):
            kc = k_t[kh * HEAD_DIM:(kh + 1) * HEAD_DIM, :]
            ms = jnp.mean(kc * kc, axis=0, keepdims=True)
            k_heads.append(kc * lax.rsqrt(ms + RMS_EPS) * kgc_ref[...])
        for jc in range(blk_per_chunk):
            jb = ch * blk_per_chunk + jc
            rows = slice(jc * BLOCK, (jc + 1) * BLOCK)
            for p in range(KV_WIDTH // PAIR):
                v_dup = _dup_heads(v[rows, p * PAIR:(p + 1) * PAIR], pair_lo)
                for half in range(2):
                    kh = 2 * p + half
                    cols = slice(kh * PAIR, (kh + 1) * PAIR)
                    kt_f32 = k_heads[kh][:, rows]
                    kt_cur = jnp.concatenate([kt_f32, kt_f32], axis=0).astype(BF16)
                    v_cur = v_dup[half].astype(BF16)
                    if jb == nblk - 1:
                        hd = slice(kh * HEAD_DIM, (kh + 1) * HEAD_DIM)
                        kwin_ref[0, hd, :] = kt_f32
                        vwin_ref[0, hd, :] = v_dup[half].T[:HEAD_DIM]
                    kt_ref[0, jb, kh, :, 0:BLOCK] = pk_ref[kh]
                    kt_ref[0, jb, kh, :, BLOCK:2 * BLOCK] = kt_cur
                    vb_ref[0, jb, 0:BLOCK, cols] = pv_ref[:, cols]
                    vb_ref[0, jb, BLOCK:2 * BLOCK, cols] = v_cur
                    pk_ref[kh] = kt_cur
                    pv_ref[:, cols] = v_cur


def _shared_kv(x2d, xs, cache_kt, cache_vt, g, w_kv, k_gain_pair, k_gain_col, *, seq, tm, chunk, dec_rows):
    rows = x2d.shape[0]
    bsz = rows // seq
    n_dec = xs.shape[0]
    n_tiles = rows // tm
    tiles_per_seq = seq // tm
    nblk = tm // BLOCK
    dec_steps = n_dec // dec_rows
    assert dec_steps <= n_tiles and dec_steps * dec_rows == n_dec
    dec = lambda s: (jnp.minimum(s, dec_steps - 1), 0, 0)
    win = (dec_rows,) + cache_kt.shape[1:]
    return pl.pallas_call(
        functools.partial(_kv_kernel, tm=tm, chunk=chunk, tiles_per_seq=tiles_per_seq, dec_steps=dec_steps),
        grid=(n_tiles,),
        in_specs=[
            pl.BlockSpec((tm, D_MODEL), lambda s: (s, 0)),
            pl.BlockSpec((dec_rows, D_MODEL), lambda s: (jnp.minimum(s, dec_steps - 1), 0)),
            pl.BlockSpec(win, dec),
            pl.BlockSpec(win, dec),
            _resident((1, D_MODEL)),
            _resident((D_MODEL, 2 * KV_WIDTH)),
            _resident((1, PAIR)),
            _resident((HEAD_DIM, 1)),
        ],
        out_specs=[
            pl.BlockSpec((1, nblk, N_KV_HEADS, PAIR, 2 * BLOCK),
                         lambda s: (s // tiles_per_seq, s % tiles_per_seq, 0, 0, 0)),
            pl.BlockSpec((1, nblk, 2 * BLOCK, N_KV_HEADS * PAIR),
                         lambda s: (s // tiles_per_seq, s % tiles_per_seq, 0, 0)),
            pl.BlockSpec((1, KV_WIDTH, BLOCK), lambda s: (s // tiles_per_seq, 0, 0)),
            pl.BlockSpec((1, KV_WIDTH, BLOCK), lambda s: (s // tiles_per_seq, 0, 0)),
            pl.BlockSpec(win, dec),
            pl.BlockSpec(win, dec),
            pl.BlockSpec(win, dec),
            pl.BlockSpec(win, dec),
        ],
        out_shape=[
            jax.ShapeDtypeStruct((bsz, seq // BLOCK, N_KV_HEADS, PAIR, 2 * BLOCK), BF16),
            jax.ShapeDtypeStruct((bsz, seq // BLOCK, 2 * BLOCK, N_KV_HEADS * PAIR), BF16),
            jax.ShapeDtypeStruct((bsz, KV_WIDTH, BLOCK), F32),
            jax.ShapeDtypeStruct((bsz, KV_WIDTH, BLOCK), F32),
            jax.ShapeDtypeStruct(cache_kt.shape, F32),
            jax.ShapeDtypeStruct(cache_vt.shape, F32),
            jax.ShapeDtypeStruct(cache_kt.shape, BF16),
            jax.ShapeDtypeStruct(cache_vt.shape, BF16),
        ],
        scratch_shapes=[
            pltpu.VMEM((D_MODEL, 2 * KV_WIDTH), BF16),
            pltpu.VMEM((KV_WIDTH, D_MODEL), BF16),
            pltpu.VMEM((N_KV_HEADS, PAIR, BLOCK), BF16),
            pltpu.VMEM((BLOCK, N_KV_HEADS * PAIR), BF16),
        ],
        compiler_params=pltpu.CompilerParams(
            dimension_semantics=("arbitrary",), vmem_limit_bytes=VMEM_LIMIT),
        name="shared_kv",
    )(x2d, xs, cache_kt, cache_vt, g, w_kv, k_gain_pair, k_gain_col)


def _normed_q_pair(q, pr, qg_ref, pair_lo, scale):
    cols = slice(pr * PAIR, (pr + 1) * PAIR)
    qp = q[:, cols]
    return qp * _pair_inv_rms(qp, pair_lo, scale) * qg_ref[:, cols]


def _decode_attention(q, k3, v3, sink_col):
    n, n_keys = k3.shape[0], k3.shape[2]
    gw = GQA_GROUP * HEAD_DIM
    rows = n * GQA_GROUP
    ri = lax.broadcasted_iota(jnp.int32, (rows, n), 0)
    ci = lax.broadcasted_iota(jnp.int32, (rows, n), 1)
    rep = jnp.where(ri // GQA_GROUP == ci, 1.0, 0.0).astype(BF16)
    row_g = lax.broadcasted_iota(jnp.int32, (rows, gw), 0) % GQA_GROUP
    col_g = lax.broadcasted_iota(jnp.int32, (rows, gw), 1) // HEAD_DIM
    own = row_g == col_g
    fj = lax.broadcasted_iota(jnp.int32, (gw, KV_WIDTH), 0)
    fc = lax.broadcasted_iota(jnp.int32, (gw, KV_WIDTH), 1)
    tc = lax.broadcasted_iota(jnp.int32, (KV_WIDTH, gw), 0)
    tj = lax.broadcasted_iota(jnp.int32, (KV_WIDTH, gw), 1)
    qb = q.astype(BF16)

    q_groups = []
    for kh in range(N_KV_HEADS):
        to_kv = jnp.where((fj % HEAD_DIM == fc % HEAD_DIM) & (fc // HEAD_DIM == kh), 1.0, 0.0).astype(BF16)
        mine = jnp.where(own, _dot(rep, qb[:, kh * gw:(kh + 1) * gw]), 0.0).astype(BF16)
        q_groups.append(_dot(mine, to_kv).astype(BF16).reshape(n, GQA_GROUP, KV_WIDTH))
    qm3 = jnp.concatenate(q_groups, axis=1)
    s = jnp.einsum("bhc,bck->bhk", qm3, k3, preferred_element_type=F32)
    h_col = lax.broadcasted_iota(jnp.int32, (N_HEADS, 1), 0).astype(F32)
    slope = jnp.exp2(-8.0 * (h_col + 1.0) / N_HEADS)[None]
    dist = (n_keys - 1 - lax.broadcasted_iota(jnp.int32, (1, n_keys), 1)).astype(F32)[None]
    s = s - slope * dist
    sink = sink_col[None]
    m = jnp.maximum(jnp.max(s, axis=-1, keepdims=True), sink)
    p = jnp.exp(s - m)
    den = jnp.sum(p, axis=-1, keepdims=True) + jnp.exp(sink - m)
    o3 = jnp.einsum("bhk,bck->bhc", p.astype(BF16), v3, preferred_element_type=F32) / den
    outs = []
    for kh in range(N_KV_HEADS):
        from_kv = jnp.where((tj % HEAD_DIM == tc % HEAD_DIM) & (tc // HEAD_DIM == kh), 1.0, 0.0).astype(BF16)
        o2 = o3[:, kh * GQA_GROUP:(kh + 1) * GQA_GROUP, :].reshape(rows, KV_WIDTH)
        hi = o2.astype(BF16)
        lo = (o2 - hi.astype(F32)).astype(BF16)
        spread = jnp.where(own, _dot(hi, from_kv) + _dot(lo, from_kv), 0.0)
        outs.append(jnp.sum(spread.reshape(n, GQA_GROUP, gw), axis=1))
    return jnp.concatenate(outs, axis=-1)


def _attn_kernel(sink_ref, x_ref, xs_ref, k3_ref, v3_ref, sinkc_ref, g_ref, win_ref, qg_ref,
                 kt_ref, vb_ref, wout_ref,
                 out_ref, outs_ref,
                 win16_ref, wout16_ref, h_ref, qlo_ref, qhi_ref, sem_ref,
                 *, layer, tm, tiles_per_seq, n_tiles):
    s = pl.program_id(0)
    nblk = tm // BLOCK
    pair_lo = lax.broadcasted_iota(jnp.int32, (1, PAIR), 1) < HEAD_DIM

    @pl.when(s == 0)
    def _():
        _fetch_bf16([(win_ref, win16_ref), (wout_ref, wout16_ref)], sem_ref)

    @pl.when((s >= 1) & (s <= n_tiles))
    def _():
        i = (s - 1) % tiles_per_seq
        h_ref[...] = _rms(x_ref[...], g_ref[...]).astype(BF16)
        for kh in range(N_KV_HEADS):
            gcols = slice(kh * GQA_GROUP * HEAD_DIM, (kh + 1) * GQA_GROUP * HEAD_DIM)
            q = _dot(h_ref[...], win16_ref[:, gcols])
            for pr in range(GQA_GROUP // 2):
                cols = slice(pr * PAIR, (pr + 1) * PAIR)
                ocols = slice((kh * (GQA_GROUP // 2) + pr) * PAIR, (kh * (GQA_GROUP // 2) + pr + 1) * PAIR)
                qp = q[:, cols]
                qn = qp * _pair_inv_rms(qp, pair_lo, HEAD_DIM ** -0.5 * LOG2E) * qg_ref[:, ocols]
                qlo_ref[:, ocols] = jnp.where(pair_lo, qn, 0.0).astype(BF16)
                qhi_ref[:, ocols] = jnp.where(pair_lo, 0.0, qn).astype(BF16)

        r_idx = lax.broadcasted_iota(jnp.int32, (BLOCK, 2 * BLOCK), 0)
        c_idx = lax.broadcasted_iota(jnp.int32, (BLOCK, 2 * BLOCK), 1)
        dist = r_idx + BLOCK - c_idx
        in_window = (dist >= 0) & (dist < WINDOW)
        neg_dist = -dist.astype(F32)

        for blk in range(nblk):
            rows = slice(blk * BLOCK, (blk + 1) * BLOCK)
            first_key = (i * nblk + blk - 1) * BLOCK
            valid = in_window & (first_key + c_idx >= 0)
            nd = jnp.where(valid, neg_dist, NEG_INF / MIN_SLOPE)
            for kh in range(N_KV_HEADS):
                kt = kt_ref[0, blk, kh]
                vdup = vb_ref[0, blk, :, kh * PAIR:(kh + 1) * PAIR]
                pieces = []
                for pr in range(GQA_GROUP // 2):
                    cols = slice((kh * (GQA_GROUP // 2) + pr) * PAIR,
                                 (kh * (GQA_GROUP // 2) + pr + 1) * PAIR)
                    pieces.append(qlo_ref[rows, cols])
                    pieces.append(qhi_ref[rows, cols])
                s_all = _dot(jnp.concatenate(pieces, axis=0), kt)
                probs, inv_den = [], []
                for g_idx in range(GQA_GROUP):
                    head = kh * GQA_GROUP + g_idx
                    slope2 = 2.0 ** (-8.0 * (head + 1) / N_HEADS) * LOG2E
                    sc = s_all[g_idx * BLOCK:(g_idx + 1) * BLOCK] + slope2 * nd
                    sink2 = sink_ref[layer, head] * LOG2E
                    m = jnp.maximum(jnp.max(sc, axis=-1, keepdims=True), sink2)
                    p = jnp.exp2(sc - m)
                    den = jnp.sum(p, axis=-1, keepdims=True) + jnp.exp2(sink2 - m)
                    probs.append(p.astype(BF16))
                    inv_den.append(1.0 / den)
                o_all = _dot(jnp.concatenate(probs, axis=0), vdup)
                for pr in range(GQA_GROUP // 2):
                    cols = slice((kh * (GQA_GROUP // 2) + pr) * PAIR,
                                 (kh * (GQA_GROUP // 2) + pr + 1) * PAIR)
                    o_even = o_all[(2 * pr) * BLOCK:(2 * pr + 1) * BLOCK] * inv_den[2 * pr]
                    o_odd = o_all[(2 * pr + 1) * BLOCK:(2 * pr + 2) * BLOCK] * inv_den[2 * pr + 1]
                    out_ref[rows, cols] = jnp.where(pair_lo, o_even, o_odd)

        z = _dot(h_ref[...], win16_ref[:, ATTN_WIDTH:])
        y = out_ref[...] * _silu(z)
        out_ref[...] = x_ref[...] + _dot(y.astype(BF16), wout16_ref[...])

    @pl.when(s == n_tiles + 1)
    def _():
        x = xs_ref[...]
        h = _rms(x, g_ref[...]).astype(BF16)
        q = _dot(h, win16_ref[:, :ATTN_WIDTH])
        z = _dot(h, win16_ref[:, ATTN_WIDTH:])
        qn = jnp.concatenate(
            [_normed_q_pair(q, pr, qg_ref, pair_lo, HEAD_DIM ** -0.5) for pr in range(N_PAIRS)], axis=-1)
        o = _decode_attention(qn, k3_ref[...], v3_ref[...], sinkc_ref[:, layer:layer + 1])
        y = o * _silu(z)
        outs_ref[...] = x + _dot(y.astype(BF16), wout16_ref[...])


def _attn_layer(x2d, xs, k3, v3, sinks, sinks_t, g, w_in, q_gain, kt_band, v_band, w_out,
                *, layer, seq, tm):
    rows = x2d.shape[0]
    n_dec = xs.shape[0]
    n_tiles = rows // tm
    tiles_per_seq = seq // tm
    nblk = tm // BLOCK
    tile = lambda s: (_tile_of(s, n_tiles), 0)
    band = lambda s: (_tile_of(s, n_tiles) // tiles_per_seq, _tile_of(s, n_tiles) % tiles_per_seq)
    one = pl.Buffered(1)
    hbm = pl.BlockSpec(memory_space=pl.ANY)
    return pl.pallas_call(
        functools.partial(_attn_kernel, layer=layer, tm=tm, tiles_per_seq=tiles_per_seq, n_tiles=n_tiles),
        grid=(n_tiles + 2,),
        in_specs=[
            pl.BlockSpec(memory_space=pltpu.SMEM),
            pl.BlockSpec((tm, D_MODEL), tile),
            _resident((n_dec, D_MODEL)),
            _resident(k3.shape),
            _resident(v3.shape),
            _resident(sinks_t.shape),
            pl.BlockSpec((None, 1, D_MODEL), lambda s: (layer, 0, 0), pipeline_mode=one),
            hbm,
            pl.BlockSpec((None, 1, ATTN_WIDTH), lambda s: (layer, 0, 0), pipeline_mode=one),
            pl.BlockSpec((1, nblk, N_KV_HEADS, PAIR, 2 * BLOCK), lambda s: band(s) + (0, 0, 0)),
            pl.BlockSpec((1, nblk, 2 * BLOCK, N_KV_HEADS * PAIR), lambda s: band(s) + (0, 0)),
            hbm,
        ],
        out_specs=[
            pl.BlockSpec((tm, D_MODEL), tile),
            _resident((n_dec, D_MODEL)),
        ],
        out_shape=[
            jax.ShapeDtypeStruct((rows, D_MODEL), F32),
            jax.ShapeDtypeStruct((n_dec, D_MODEL), F32),
        ],
        scratch_shapes=[
            pltpu.VMEM((D_MODEL, 2 * ATTN_WIDTH), BF16),
            pltpu.VMEM((ATTN_WIDTH, D_MODEL), BF16),
            pltpu.VMEM((tm, D_MODEL), BF16),
            pltpu.VMEM((tm, ATTN_WIDTH), BF16),
            pltpu.VMEM((tm, ATTN_WIDTH), BF16),
            pltpu.SemaphoreType.DMA((2,)),
        ],
        compiler_params=pltpu.CompilerParams(
            dimension_semantics=("arbitrary",), vmem_limit_bytes=VMEM_LIMIT),
        name="attn_layer",
    )(sinks, x2d, xs, k3, v3, sinks_t, g, w_in, q_gain, kt_band, v_band, w_out)


def kernel(x_prompt, x_sample, state_pool, cache_k_win, cache_v_win, norm_a, w_in_a, w_grp_a, scale_a, w_out_a, norm_kv, w_kv, k_norm, norm_b, w_in_b, q_norm, sinks, w_out_b):
    bsz, seq, _ = x_prompt.shape
    n_dec = x_sample.shape[0]
    assert x_sample.shape[1] == 1 and cache_k_win.shape[1] == WINDOW
    tm_a, tm_b, tm_kv = 256, 256, 1024
    kv_chunk, kv_dec_rows = 256, 8

    hp = x_prompt.reshape(bsz * seq, D_MODEL)
    hs = x_sample[:, 0]
    state_t = state_pool.transpose(0, 2, 1, 3)
    w_grp_rows = w_grp_a.reshape(N_A_LAYERS, N_POOL_GROUPS * POOL_GROUP, POOL_GROUP)

    assert N_A_LAYERS == 2 and N_B_LAYERS == 2
    pool_args = (hp, hs, state_t, norm_a[:, None])
    hp, hist0, hs, state0, a1_in, a1_grp, a1_out, b1_in, b1_out = _pool_layer(
        *pool_args, (w_in_a, w_grp_rows, w_out_a), scale_a[:, None], layer=0, own_bf16=False,
        convert=[(w_in_a, 1), (w_grp_rows, 1), (w_out_a, 1), (w_in_b, 1), (w_out_b, 1)], seq=seq, tm=tm_a)
    hp, hist1, hs, state1, b0_in, b0_out = _pool_layer(
        hp, hs, state_t, norm_a[:, None], (a1_in, a1_grp, a1_out), scale_a[:, None], layer=1, own_bf16=True,
        convert=[(w_in_b, 0), (w_out_b, 0)], seq=seq, tm=tm_a)
    pool_p = [h[:, POOL_HIST - POOL_BUF:] for h in (hist0, hist1)]
    pool_s = [state0, state1]
    attn_weights = [(b0_in, b0_out), (b1_in, b1_out)]

    def keys_minor(t):
        return t.transpose(0, 2, 3, 1).reshape(t.shape[0], KV_WIDTH, t.shape[1])

    def keys_major(t):
        return t.reshape(t.shape[0], N_KV_HEADS, HEAD_DIM, t.shape[2]).transpose(0, 3, 1, 2)

    kt_band, v_band, k_win_p, v_win_p, k_win_s, v_win_s, k16, v16 = _shared_kv(
        hp, hs, keys_minor(cache_k_win), keys_minor(cache_v_win),
        norm_kv[None], w_kv, jnp.tile(k_norm, 2)[None], k_norm[:, None],
        seq=seq, tm=tm_kv, chunk=kv_chunk, dec_rows=kv_dec_rows)

    q_gain = jnp.tile(q_norm, (1, N_HEADS))[:, None]
    for j in range(N_B_LAYERS):
        hp, hs = _attn_layer(hp, hs, k16, v16, sinks, sinks.T, norm_b[:, None], attn_weights[j][0], q_gain,
                             kt_band, v_band, attn_weights[j][1], layer=j, seq=seq, tm=tm_b)

    return (hp.reshape(bsz, seq, D_MODEL), hs[:, None],
            jnp.stack(pool_p, axis=0), jnp.stack(pool_s, axis=0).transpose(0, 2, 1, 3),
            keys_major(k_win_p), keys_major(v_win_p), keys_major(k_win_s), keys_major(v_win_s))
```

```python
import functools
import math

import jax
import jax.numpy as jnp
from jax import lax
from jax.experimental import pallas as pl
from jax.experimental.pallas import tpu as pltpu

D_MODEL = 2048
N_A_LAYERS = 2
N_B_LAYERS = 2
POOL_WINDOWS = (2, 4, 8, 16)
N_POOL_GROUPS = len(POOL_WINDOWS)
POOL_WIDTH = D_MODEL
POOL_GROUP = POOL_WIDTH // N_POOL_GROUPS
POOL_BUF = max(POOL_WINDOWS) - 1
POOL_HIST = POOL_BUF + 1
HEAD_DIM = 64
N_HEADS = D_MODEL // HEAD_DIM
N_KV_HEADS = N_HEADS // 8
GQA_GROUP = N_HEADS // N_KV_HEADS
ATTN_WIDTH = N_HEADS * HEAD_DIM
KV_WIDTH = N_KV_HEADS * HEAD_DIM
WINDOW = 128
BLOCK = WINDOW
PAST_LEN = 16384
RMS_EPS = 1e-6
NEG_INF = -1e30
LOG2E = math.log2(math.e)
MIN_SLOPE = 2.0 ** -8
PAIR = 2 * HEAD_DIM
N_PAIRS = N_HEADS // 2
VMEM_LIMIT = 62 * 1024 * 1024

BF16 = jnp.bfloat16
F32 = jnp.float32


def _rms(x, g):
    return x * lax.rsqrt(jnp.mean(x * x, axis=-1, keepdims=True) + RMS_EPS) * g


def _silu(z):
    return z * jax.nn.sigmoid(z)


def _dot(a, b):
    return jnp.dot(a, b, preferred_element_type=F32)


def _resident(shape):
    return pl.BlockSpec(shape, lambda *_: (0,) * len(shape), pipeline_mode=pl.Buffered(1))


def _tile_of(s, n_tiles):
    return jnp.clip(s - 1, 0, n_tiles - 1)


def _weight_blocks(src, dst16, block_rows, block_cols):
    rows, cols = dst16.shape
    plan = []
    for r in range(0, rows, block_rows):
        for c in range(0, cols, block_cols):
            w = min(block_cols, cols - c)
            plan.append((src.at[pl.ds(r, block_rows), pl.ds(c, w)],
                         dst16.at[pl.ds(r, block_rows), pl.ds(c, w)], (block_rows, w)))
    return plan


def _load_weights_bf16(plan, slots, sem):
    n_slots = len(slots)
    in_flight = n_slots - 1

    def copy(c):
        src, _, (r, w) = plan[c]
        k = c % n_slots
        return pltpu.make_async_copy(src, slots[k].at[pl.ds(0, r), pl.ds(0, w)], sem.at[k])

    for c in range(min(in_flight, len(plan))):
        copy(c).start()
    for c in range(len(plan)):
        if c + in_flight < len(plan):
            copy(c + in_flight).start()
        copy(c).wait()
        _, dst, (r, w) = plan[c]
        dst[...] = slots[c % n_slots][0:r, 0:w].astype(BF16)


def _fetch_bf16(pairs, sem):
    copies = [pltpu.make_async_copy(src, dst, sem.at[i]) for i, (src, dst) in enumerate(pairs)]
    for cp in copies:
        cp.start()
    for cp in copies:
        cp.wait()


def _half_tiles(*refs):
    slots = []
    for ref in refs:
        half = ref.shape[0] // 2
        slots += [ref.at[pl.ds(0, half)], ref.at[pl.ds(half, half)]]
    return slots


def _pair_inv_rms(pair, pair_lo, scale):
    sq = pair * pair
    lo = jnp.sum(jnp.where(pair_lo, sq, 0.0), axis=-1, keepdims=True)
    hi = jnp.sum(jnp.where(pair_lo, 0.0, sq), axis=-1, keepdims=True)
    ms = jnp.where(pair_lo, lo, hi) * (1.0 / HEAD_DIM)
    return lax.rsqrt(ms + RMS_EPS) * scale


def _pool_tail(x_ref, u, z, window_sums, inv_cnt, wgrp_ref, scale_ref, wout_ref):
    mixed = []
    for g in range(N_POOL_GROUPS):
        cols = slice(g * POOL_GROUP, (g + 1) * POOL_GROUP)
        p = window_sums[g] * inv_cnt[g] - u[:, cols]
        mixed.append(_dot(p.astype(BF16), wgrp_ref[cols, :]))
    y = jnp.concatenate(mixed, axis=-1) * scale_ref[...] * _silu(z)
    return x_ref[...] + _dot(y.astype(BF16), wout_ref[...])


def _pool_kernel(*refs, layer, tm, tiles_per_seq, n_tiles, own_bf16, n_conv, n_prev):
    (x_ref, xs_ref, st_ref, g_ref, win_ref, wgrp_ref, scale_ref, wout_ref) = refs[:8]
    conv_src = refs[8:8 + n_conv]
    prev_states = refs[8 + n_conv:8 + n_conv + n_prev]
    n_in = 8 + n_conv + n_prev
    (out_ref, hist_ref, outs_ref, newst_ref) = refs[n_in:n_in + 4]
    conv_dst = refs[n_in + 4:n_in + 4 + n_conv]
    (win16_ref, wgrp16_ref, wout16_ref, uext_ref, sem_ref) = refs[n_in + 4 + n_conv:n_in + 9 + n_conv]
    state_out = refs[n_in + 9 + n_conv] if n_prev else newst_ref
    s = pl.program_id(0)

    @pl.when(s == 0)
    def _():
        if own_bf16:
            _fetch_bf16([(win_ref, win16_ref), (wgrp_ref, wgrp16_ref), (wout_ref, wout16_ref)], sem_ref)
            return
        slots = _half_tiles(out_ref, uext_ref.at[pl.ds(POOL_HIST, tm)])
        rows = tm // 2
        assert POOL_GROUP // rows == len(slots) and N_POOL_GROUPS * POOL_GROUP == D_MODEL
        grp = [(wgrp_ref.at[layer, pl.ds(g * POOL_GROUP + r * rows, rows)],
                slots[r].at[:, pl.ds(g * POOL_GROUP, POOL_GROUP)],
                wgrp16_ref.at[pl.ds(g * POOL_GROUP + r * rows, rows)])
               for g in range(N_POOL_GROUPS) for r in range(len(slots))]
        grp_copies = [pltpu.make_async_copy(src, stage, sem_ref.at[0]) for src, stage, _ in grp]
        for cp in grp_copies:
            cp.start()
        for cp in grp_copies:
            cp.wait()
        for _, stage, dst in grp:
            dst[...] = stage[...].astype(BF16)
        plan = (_weight_blocks(win_ref.at[layer], win16_ref, rows, D_MODEL)
                + _weight_blocks(wout_ref.at[layer], wout16_ref, rows, D_MODEL))
        _load_weights_bf16(plan, slots, sem_ref)

    @pl.when((s >= 1) & (s <= n_tiles))
    def _():
        i = (s - 1) % tiles_per_seq

        @pl.when(i == 0)
        def _():
            uext_ref[0:POOL_HIST, :] = jnp.zeros((POOL_HIST, POOL_WIDTH), F32)

        x = x_ref[...]
        xg = (x * g_ref[...]).astype(BF16)
        inv_rms = lax.rsqrt(jnp.mean(x * x, axis=-1, keepdims=True) + RMS_EPS)
        u = _dot(xg, win16_ref[:, :POOL_WIDTH]) * inv_rms
        z = _dot(xg, win16_ref[:, POOL_WIDTH:]) * inv_rms
        uext_ref[POOL_HIST:POOL_HIST + tm, :] = u

        pos = i * tm + lax.broadcasted_iota(jnp.int32, (tm, 1), 0)
        sums, inv_cnt = [], []
        for g, w in enumerate(POOL_WINDOWS):
            cols = slice(g * POOL_GROUP, (g + 1) * POOL_GROUP)
            acc = u[:, cols]
            for k in range(1, w):
                acc = acc + uext_ref[POOL_HIST - k:POOL_HIST - k + tm, cols]
            sums.append(acc)
            inv_cnt.append(1.0 / jnp.minimum(pos + 1, w).astype(F32))
        out_ref[...] = _pool_tail(x_ref, u, z, sums, inv_cnt, wgrp16_ref, scale_ref, wout16_ref)

        tail = uext_ref[tm:tm + POOL_HIST, :]
        hist_ref[0] = tail
        uext_ref[0:POOL_HIST, :] = tail

        for src, dst in zip(conv_src, conv_dst):
            dst[...] = src[...].astype(BF16)

    @pl.when(s == n_tiles + 1)
    def _():
        h = _rms(xs_ref[...], g_ref[...]).astype(BF16)
        u = _dot(h, win16_ref[:, :POOL_WIDTH])
        z = _dot(h, win16_ref[:, POOL_WIDTH:])
        stack = [pltpu.make_async_copy(prev, newst_ref.at[k], sem_ref.at[k])
                 for k, prev in enumerate(prev_states)]
        for cp in stack:
            cp.start()
        for r in range(POOL_BUF - 1):
            state_out[r] = st_ref[0, r + 1]
        state_out[POOL_BUF - 1] = u
        if n_prev:
            stack.append(pltpu.make_async_copy(state_out, newst_ref.at[n_prev], sem_ref.at[n_prev]))
            stack[-1].start()
        sums, inv_cnt = [], []
        for g, w in enumerate(POOL_WINDOWS):
            cols = slice(g * POOL_GROUP, (g + 1) * POOL_GROUP)
            acc = u[:, cols]
            for k in range(1, w):
                acc = acc + st_ref[0, POOL_BUF - k, :, cols]
            sums.append(acc)
            inv_cnt.append(1.0 / float(min(PAST_LEN + 1, w)))
        outs_ref[...] = _pool_tail(xs_ref, u, z, sums, inv_cnt, wgrp16_ref, scale_ref, wout16_ref)
        for cp in stack:
            cp.wait()


def _convert_specs(convert, n_tiles):
    in_specs, out_specs, out_shape = [], [], []
    for arr, lyr in convert:
        _, rows, cols = arr.shape
        step_rows = rows // n_tiles
        in_specs.append(pl.BlockSpec((None, step_rows, cols),
                                     lambda s, lyr=lyr: (lyr, _tile_of(s, n_tiles), 0)))
        out_specs.append(pl.BlockSpec((step_rows, cols), lambda s: (_tile_of(s, n_tiles), 0)))
        out_shape.append(jax.ShapeDtypeStruct((rows, cols), BF16))
    return in_specs, out_specs, out_shape


def _pool_layer(x2d, xs, state, g, weights, scale, *, layer, own_bf16, convert, prev_states, seq, tm):
    rows = x2d.shape[0]
    n_dec = xs.shape[0]
    n_tiles = rows // tm
    tiles_per_seq = seq // tm
    tile = lambda s: (_tile_of(s, n_tiles), 0)
    one = pl.Buffered(1)
    hbm = pl.BlockSpec(memory_space=pl.ANY)
    conv_in, conv_out, conv_shape = _convert_specs(convert, n_tiles)
    w_in, w_grp, w_out = weights
    n_prev = len(prev_states)
    state_shape = (POOL_BUF, n_dec, POOL_WIDTH)
    return pl.pallas_call(
        functools.partial(_pool_kernel, layer=layer, tm=tm, tiles_per_seq=tiles_per_seq, n_tiles=n_tiles,
                          own_bf16=own_bf16, n_conv=len(convert), n_prev=n_prev),
        grid=(n_tiles + 2,),
        in_specs=[
            pl.BlockSpec((tm, D_MODEL), tile),
            _resident((n_dec, D_MODEL)),
            pl.BlockSpec((1, POOL_BUF, n_dec, POOL_WIDTH), lambda s: (layer, 0, 0, 0), pipeline_mode=one),
            pl.BlockSpec((None, 1, D_MODEL), lambda s: (layer, 0, 0), pipeline_mode=one),
            hbm,
            hbm,
            pl.BlockSpec((None, 1, POOL_WIDTH), lambda s: (layer, 0, 0), pipeline_mode=one),
            hbm,
        ] + conv_in + [hbm] * n_prev,
        out_specs=[
            pl.BlockSpec((tm, D_MODEL), tile),
            pl.BlockSpec((1, POOL_HIST, POOL_WIDTH),
                         lambda s: (_tile_of(s, n_tiles) // tiles_per_seq, 0, 0)),
            _resident((n_dec, D_MODEL)),
            hbm if n_prev else _resident(state_shape),
        ] + conv_out,
        out_shape=[
            jax.ShapeDtypeStruct((rows, D_MODEL), F32),
            jax.ShapeDtypeStruct((rows // seq, POOL_HIST, POOL_WIDTH), F32),
            jax.ShapeDtypeStruct((n_dec, D_MODEL), F32),
            jax.ShapeDtypeStruct(((n_prev + 1,) + state_shape) if n_prev else state_shape, F32),
        ] + conv_shape,
        scratch_shapes=[
            pltpu.VMEM((D_MODEL, 2 * POOL_WIDTH), BF16),
            pltpu.VMEM((N_POOL_GROUPS * POOL_GROUP, POOL_GROUP), BF16),
            pltpu.VMEM((POOL_WIDTH, D_MODEL), BF16),
            pltpu.VMEM((POOL_HIST + tm, POOL_WIDTH), F32),
            pltpu.SemaphoreType.DMA((4,)),
        ] + ([pltpu.VMEM(state_shape, F32)] if n_prev else []),
        compiler_params=pltpu.CompilerParams(
            dimension_semantics=("arbitrary",), vmem_limit_bytes=VMEM_LIMIT),
        name="pool_layer",
    )(x2d, xs, state, g, w_in, w_grp, scale, w_out, *[arr for arr, _ in convert], *prev_states)


def _kv_project(x, g_ref, w16_ref, kg_ref):
    h = _rms(x, g_ref[...]).astype(BF16)
    kv = _dot(h, w16_ref[...])
    pair_lo = lax.broadcasted_iota(jnp.int32, (1, PAIR), 1) < HEAD_DIM
    k_pairs = []
    for p in range(KV_WIDTH // PAIR):
        kp = kv[:, p * PAIR:(p + 1) * PAIR]
        k_pairs.append(kp * _pair_inv_rms(kp, pair_lo, 1.0) * kg_ref[...])
    return k_pairs, kv[:, KV_WIDTH:]


def _dup_heads(col, pair_lo):
    rolled = pltpu.roll(col, HEAD_DIM, 1)
    return jnp.where(pair_lo, col, rolled), jnp.where(pair_lo, rolled, col)


def _slide_window(cache_ref, new_t, win_ref, win16_ref):
    n, _, n_keys = cache_ref.shape
    last = lax.broadcasted_iota(jnp.int32, (1, n_keys), 1) == n_keys - 1
    for b in range(n):
        slid = jnp.where(last, new_t[:, b:b + 1], pltpu.roll(cache_ref[b], n_keys - 1, 1))
        win_ref[b] = slid
        win16_ref[b] = slid.astype(BF16)


def _kv_kernel(x_ref, xs_ref, ck_ref, cv_ref, g_ref, wkv_ref, kg_ref, kgc_ref,
               kt_ref, vb_ref, kwin_ref, vwin_ref, kws_ref, vws_ref, k16_ref, v16_ref,
               w16_ref, wkt16_ref, pk_ref, pv_ref, *, tm, chunk, tiles_per_seq, dec_steps):
    s = pl.program_id(0)
    nblk = tm // BLOCK
    blk_per_chunk = chunk // BLOCK
    pair_lo = lax.broadcasted_iota(jnp.int32, (1, PAIR), 1) < HEAD_DIM

    @pl.when(s == 0)
    def _():
        w16_ref[...] = wkv_ref[...].astype(BF16)
        for c in range(KV_WIDTH // BLOCK):
            cols = slice(c * BLOCK, (c + 1) * BLOCK)
            wkt16_ref[cols, :] = wkv_ref[:, cols].T.astype(BF16)

    @pl.when(s % tiles_per_seq == 0)
    def _():
        pk_ref[...] = jnp.zeros(pk_ref.shape, BF16)
        pv_ref[...] = jnp.zeros(pv_ref.shape, BF16)

    @pl.when(s < dec_steps)
    def _():
        k_pairs, v = _kv_project(xs_ref[...], g_ref, w16_ref, kg_ref)
        n_dec = xs_ref.shape[0]
        pad = jnp.zeros((BLOCK - n_dec, KV_WIDTH), F32)
        kn_t = jnp.concatenate([jnp.concatenate(k_pairs, axis=1), pad], axis=0)
        vn_t = jnp.concatenate([v, pad], axis=0)
        kn_t = jnp.concatenate([kn_t[:, c * BLOCK:(c + 1) * BLOCK].T for c in range(KV_WIDTH // BLOCK)], axis=0)
        vn_t = jnp.concatenate([vn_t[:, c * BLOCK:(c + 1) * BLOCK].T for c in range(KV_WIDTH // BLOCK)], axis=0)
        _slide_window(ck_ref, kn_t, kws_ref, k16_ref)
        _slide_window(cv_ref, vn_t, vws_ref, v16_ref)

    for ch in range(tm // chunk):
        h = _rms(x_ref[ch * chunk:(ch + 1) * chunk, :], g_ref[...]).astype(BF16)
        v = _dot(h, w16_ref[:, KV_WIDTH:])
        k_t = lax.dot_general(wkt16_ref[...], h, (((1,), (1,)), ((), ())),
                              preferred_element_type=F32)
        k_heads = []
        for kh in range(N_KV_HEADS):
            kc = k_t[kh * HEAD_DIM:(kh + 1) * HEAD_DIM, :]
            ms = jnp.mean(kc * kc, axis=0, keepdims=True)
            k_heads.append(kc * lax.rsqrt(ms + RMS_EPS) * kgc_ref[...])
        for jc in range(blk_per_chunk):
            jb = ch * blk_per_chunk + jc
            rows = slice(jc * BLOCK, (jc + 1) * BLOCK)
            for p in range(KV_WIDTH // PAIR):
                v_dup = _dup_heads(v[rows, p * PAIR:(p + 1) * PAIR], pair_lo)
                for half in range(2):
                    kh = 2 * p + half
                    cols = slice(kh * PAIR, (kh + 1) * PAIR)
                    kt_f32 = k_heads[kh][:, rows]
                    kt_cur = jnp.concatenate([kt_f32, kt_f32], axis=0).astype(BF16)
                    v_cur = v_dup[half].astype(BF16)
                    if jb == nblk - 1:
                        hd = slice(kh * HEAD_DIM, (kh + 1) * HEAD_DIM)
                        kwin_ref[0, hd, :] = kt_f32
                        vwin_ref[0, hd, :] = v_dup[half].T[:HEAD_DIM]
                    kt_ref[0, jb, kh, :, 0:BLOCK] = pk_ref[kh]
                    kt_ref[0, jb, kh, :, BLOCK:2 * BLOCK] = kt_cur
                    vb_ref[0, jb, 0:BLOCK, cols] = pv_ref[:, cols]
                    vb_ref[0, jb, BLOCK:2 * BLOCK, cols] = v_cur
                    pk_ref[kh] = kt_cur
                    pv_ref[:, cols] = v_cur


def _shared_kv(x2d, xs, cache_kt, cache_vt, g, w_kv, k_gain_pair, k_gain_col, *, seq, tm, chunk, dec_rows):
    rows = x2d.shape[0]
    bsz = rows // seq
    n_dec = xs.shape[0]
    n_tiles = rows // tm
    tiles_per_seq = seq // tm
    nblk = tm // BLOCK
    dec_steps = n_dec // dec_rows
    assert dec_steps <= n_tiles and dec_steps * dec_rows == n_dec
    dec = lambda s: (jnp.minimum(s, dec_steps - 1), 0, 0)
    win = (dec_rows,) + cache_kt.shape[1:]
    return pl.pallas_call(
        functools.partial(_kv_kernel, tm=tm, chunk=chunk, tiles_per_seq=tiles_per_seq, dec_steps=dec_steps),
        grid=(n_tiles,),
        in_specs=[
            pl.BlockSpec((tm, D_MODEL), lambda s: (s, 0)),
            pl.BlockSpec((dec_rows, D_MODEL), lambda s: (jnp.minimum(s, dec_steps - 1), 0)),
            pl.BlockSpec(win, dec),
            pl.BlockSpec(win, dec),
            _resident((1, D_MODEL)),
            _resident((D_MODEL, 2 * KV_WIDTH)),
            _resident((1, PAIR)),
            _resident((HEAD_DIM, 1)),
        ],
        out_specs=[
            pl.BlockSpec((1, nblk, N_KV_HEADS, PAIR, 2 * BLOCK),
                         lambda s: (s // tiles_per_seq, s % tiles_per_seq, 0, 0, 0)),
            pl.BlockSpec((1, nblk, 2 * BLOCK, N_KV_HEADS * PAIR),
                         lambda s: (s // tiles_per_seq, s % tiles_per_seq, 0, 0)),
            pl.BlockSpec((1, KV_WIDTH, BLOCK), lambda s: (s // tiles_per_seq, 0, 0)),
            pl.BlockSpec((1, KV_WIDTH, BLOCK), lambda s: (s // tiles_per_seq, 0, 0)),
            pl.BlockSpec(win, dec),
            pl.BlockSpec(win, dec),
            pl.BlockSpec(win, dec),
            pl.BlockSpec(win, dec),
        ],
        out_shape=[
            jax.ShapeDtypeStruct((bsz, seq // BLOCK, N_KV_HEADS, PAIR, 2 * BLOCK), BF16),
            jax.ShapeDtypeStruct((bsz, seq // BLOCK, 2 * BLOCK, N_KV_HEADS * PAIR), BF16),
            jax.ShapeDtypeStruct((bsz, KV_WIDTH, BLOCK), F32),
            jax.ShapeDtypeStruct((bsz, KV_WIDTH, BLOCK), F32),
            jax.ShapeDtypeStruct(cache_kt.shape, F32),
            jax.ShapeDtypeStruct(cache_vt.shape, F32),
            jax.ShapeDtypeStruct(cache_kt.shape, BF16),
            jax.ShapeDtypeStruct(cache_vt.shape, BF16),
        ],
        scratch_shapes=[
            pltpu.VMEM((D_MODEL, 2 * KV_WIDTH), BF16),
            pltpu.VMEM((KV_WIDTH, D_MODEL), BF16),
            pltpu.VMEM((N_KV_HEADS, PAIR, BLOCK), BF16),
            pltpu.VMEM((BLOCK, N_KV_HEADS * PAIR), BF16),
        ],
        compiler_params=pltpu.CompilerParams(
            dimension_semantics=("arbitrary",), vmem_limit_bytes=VMEM_LIMIT),
        name="shared_kv",
    )(x2d, xs, cache_kt, cache_vt, g, w_kv, k_gain_pair, k_gain_col)


def _normed_q_pair(q, pr, qg_ref, pair_lo, scale):
    cols = slice(pr * PAIR, (pr + 1) * PAIR)
    qp = q[:, cols]
    return qp * _pair_inv_rms(qp, pair_lo, scale) * qg_ref[:, cols]


def _decode_attention(q, k3, v3, sink_col):
    n, n_keys = k3.shape[0], k3.shape[2]
    gw = GQA_GROUP * HEAD_DIM
    rows = n * GQA_GROUP
    ri = lax.broadcasted_iota(jnp.int32, (rows, n), 0)
    ci = lax.broadcasted_iota(jnp.int32, (rows, n), 1)
    rep = jnp.where(ri // GQA_GROUP == ci, 1.0, 0.0).astype(BF16)
    row_g = lax.broadcasted_iota(jnp.int32, (rows, gw), 0) % GQA_GROUP
    col_g = lax.broadcasted_iota(jnp.int32, (rows, gw), 1) // HEAD_DIM
    own = row_g == col_g
    fj = lax.broadcasted_iota(jnp.int32, (gw, KV_WIDTH), 0)
    fc = lax.broadcasted_iota(jnp.int32, (gw, KV_WIDTH), 1)
    tc = lax.broadcasted_iota(jnp.int32, (KV_WIDTH, gw), 0)
    tj = lax.broadcasted_iota(jnp.int32, (KV_WIDTH, gw), 1)
    qb = q.astype(BF16)

    q_groups = []
    for kh in range(N_KV_HEADS):
        to_kv = jnp.where((fj % HEAD_DIM == fc % HEAD_DIM) & (fc // HEAD_DIM == kh), 1.0, 0.0).astype(BF16)
        mine = jnp.where(own, _dot(rep, qb[:, kh * gw:(kh + 1) * gw]), 0.0).astype(BF16)
        q_groups.append(_dot(mine, to_kv).astype(BF16).reshape(n, GQA_GROUP, KV_WIDTH))
    qm3 = jnp.concatenate(q_groups, axis=1)
    s = jnp.einsum("bhc,bck->bhk", qm3, k3, preferred_element_type=F32)
    h_col = lax.broadcasted_iota(jnp.int32, (N_HEADS, 1), 0).astype(F32)
    slope = jnp.exp2(-8.0 * (h_col + 1.0) / N_HEADS)[None]
    dist = (n_keys - 1 - lax.broadcasted_iota(jnp.int32, (1, n_keys), 1)).astype(F32)[None]
    s = s - slope * dist
    sink = sink_col[None]
    m = jnp.maximum(jnp.max(s, axis=-1, keepdims=True), sink)
    p = jnp.exp(s - m)
    den = jnp.sum(p, axis=-1, keepdims=True) + jnp.exp(sink - m)
    o3 = jnp.einsum("bhk,bck->bhc", p.astype(BF16), v3, preferred_element_type=F32) / den
    outs = []
    for kh in range(N_KV_HEADS):
        from_kv = jnp.where((tj % HEAD_DIM == tc % HEAD_DIM) & (tc // HEAD_DIM == kh), 1.0, 0.0).astype(BF16)
        o2 = o3[:, kh * GQA_GROUP:(kh + 1) * GQA_GROUP, :].reshape(rows, KV_WIDTH)
        hi = o2.astype(BF16)
        lo = (o2 - hi.astype(F32)).astype(BF16)
        spread = jnp.where(own, _dot(hi, from_kv) + _dot(lo, from_kv), 0.0)
        outs.append(jnp.sum(spread.reshape(n, GQA_GROUP, gw), axis=1))
    return jnp.concatenate(outs, axis=-1)


def _attn_kernel(sink_ref, x_ref, xs_ref, k3_ref, v3_ref, sinkc_ref, g_ref, win_ref, qg_ref,
                 kt_ref, vb_ref, wout_ref,
                 out_ref, outs_ref,
                 win16_ref, wout16_ref, h_ref, qlo_ref, qhi_ref, sem_ref,
                 *, layer, tm, tiles_per_seq, n_tiles):
    s = pl.program_id(0)
    nblk = tm // BLOCK
    pair_lo = lax.broadcasted_iota(jnp.int32, (1, PAIR), 1) < HEAD_DIM

    @pl.when(s == 0)
    def _():
        _fetch_bf16([(win_ref, win16_ref), (wout_ref, wout16_ref)], sem_ref)

    @pl.when((s >= 1) & (s <= n_tiles))
    def _():
        i = (s - 1) % tiles_per_seq
        h_ref[...] = _rms(x_ref[...], g_ref[...]).astype(BF16)
        for kh in range(N_KV_HEADS):
            gcols = slice(kh * GQA_GROUP * HEAD_DIM, (kh + 1) * GQA_GROUP * HEAD_DIM)
            q = _dot(h_ref[...], win16_ref[:, gcols])
            for pr in range(GQA_GROUP // 2):
                cols = slice(pr * PAIR, (pr + 1) * PAIR)
                ocols = slice((kh * (GQA_GROUP // 2) + pr) * PAIR, (kh * (GQA_GROUP // 2) + pr + 1) * PAIR)
                qp = q[:, cols]
                qn = qp * _pair_inv_rms(qp, pair_lo, HEAD_DIM ** -0.5 * LOG2E) * qg_ref[:, ocols]
                qlo_ref[:, ocols] = jnp.where(pair_lo, qn, 0.0).astype(BF16)
                qhi_ref[:, ocols] = jnp.where(pair_lo, 0.0, qn).astype(BF16)

        r_idx = lax.broadcasted_iota(jnp.int32, (BLOCK, 2 * BLOCK), 0)
        c_idx = lax.broadcasted_iota(jnp.int32, (BLOCK, 2 * BLOCK), 1)
        dist = r_idx + BLOCK - c_idx
        in_window = (dist >= 0) & (dist < WINDOW)
        neg_dist = -dist.astype(F32)

        for blk in range(nblk):
            rows = slice(blk * BLOCK, (blk + 1) * BLOCK)
            first_key = (i * nblk + blk - 1) * BLOCK
            valid = in_window & (first_key + c_idx >= 0)
            nd = jnp.where(valid, neg_dist, NEG_INF / MIN_SLOPE)
            for kh in range(N_KV_HEADS):
                kt = kt_ref[0, blk, kh]
                vdup = vb_ref[0, blk, :, kh * PAIR:(kh + 1) * PAIR]
                pieces = []
                for pr in range(GQA_GROUP // 2):
                    cols = slice((kh * (GQA_GROUP // 2) + pr) * PAIR,
                                 (kh * (GQA_GROUP // 2) + pr + 1) * PAIR)
                    pieces.append(qlo_ref[rows, cols])
                    pieces.append(qhi_ref[rows, cols])
                s_all = _dot(jnp.concatenate(pieces, axis=0), kt)
                probs, inv_den = [], []
                for g_idx in range(GQA_GROUP):
                    head = kh * GQA_GROUP + g_idx
                    slope2 = 2.0 ** (-8.0 * (head + 1) / N_HEADS) * LOG2E
                    sc = s_all[g_idx * BLOCK:(g_idx + 1) * BLOCK] + slope2 * nd
                    sink2 = sink_ref[layer, head] * LOG2E
                    m = jnp.maximum(jnp.max(sc, axis=-1, keepdims=True), sink2)
                    p = jnp.exp2(sc - m)
                    den = jnp.sum(p, axis=-1, keepdims=True) + jnp.exp2(sink2 - m)
                    probs.append(p.astype(BF16))
                    inv_den.append(1.0 / den)
                o_all = _dot(jnp.concatenate(probs, axis=0), vdup)
                for pr in range(GQA_GROUP // 2):
                    cols = slice((kh * (GQA_GROUP // 2) + pr) * PAIR,
                                 (kh * (GQA_GROUP // 2) + pr + 1) * PAIR)
                    o_even = o_all[(2 * pr) * BLOCK:(2 * pr + 1) * BLOCK] * inv_den[2 * pr]
                    o_odd = o_all[(2 * pr + 1) * BLOCK:(2 * pr + 2) * BLOCK] * inv_den[2 * pr + 1]
                    out_ref[rows, cols] = jnp.where(pair_lo, o_even, o_odd)

        z = _dot(h_ref[...], win16_ref[:, ATTN_WIDTH:])
        y = out_ref[...] * _silu(z)
        out_ref[...] = x_ref[...] + _dot(y.astype(BF16), wout16_ref[...])

    @pl.when(s == n_tiles + 1)
    def _():
        x = xs_ref[...]
        h = _rms(x, g_ref[...]).astype(BF16)
        q = _dot(h, win16_ref[:, :ATTN_WIDTH])
        z = _dot(h, win16_ref[:, ATTN_WIDTH:])
        qn = jnp.concatenate(
            [_normed_q_pair(q, pr, qg_ref, pair_lo, HEAD_DIM ** -0.5) for pr in range(N_PAIRS)], axis=-1)
        o = _decode_attention(qn, k3_ref[...], v3_ref[...], sinkc_ref[:, layer:layer + 1])
        y = o * _silu(z)
        outs_ref[...] = x + _dot(y.astype(BF16), wout16_ref[...])


def _attn_layer(x2d, xs, k3, v3, sinks, sinks_t, g, w_in, q_gain, kt_band, v_band, w_out,
                *, layer, seq, tm):
    rows = x2d.shape[0]
    n_dec = xs.shape[0]
    n_tiles = rows // tm
    tiles_per_seq = seq // tm
    nblk = tm // BLOCK
    tile = lambda s: (_tile_of(s, n_tiles), 0)
    band = lambda s: (_tile_of(s, n_tiles) // tiles_per_seq, _tile_of(s, n_tiles) % tiles_per_seq)
    one = pl.Buffered(1)
    hbm = pl.BlockSpec(memory_space=pl.ANY)
    return pl.pallas_call(
        functools.partial(_attn_kernel, layer=layer, tm=tm, tiles_per_seq=tiles_per_seq, n_tiles=n_tiles),
        grid=(n_tiles + 2,),
        in_specs=[
            pl.BlockSpec(memory_space=pltpu.SMEM),
            pl.BlockSpec((tm, D_MODEL), tile),
            _resident((n_dec, D_MODEL)),
            _resident(k3.shape),
            _resident(v3.shape),
            _resident(sinks_t.shape),
            pl.BlockSpec((None, 1, D_MODEL), lambda s: (layer, 0, 0), pipeline_mode=one),
            hbm,
            pl.BlockSpec((None, 1, ATTN_WIDTH), lambda s: (layer, 0, 0), pipeline_mode=one),
            pl.BlockSpec((1, nblk, N_KV_HEADS, PAIR, 2 * BLOCK), lambda s: band(s) + (0, 0, 0)),
            pl.BlockSpec((1, nblk, 2 * BLOCK, N_KV_HEADS * PAIR), lambda s: band(s) + (0, 0)),
            hbm,
        ],
        out_specs=[
            pl.BlockSpec((tm, D_MODEL), tile),
            _resident((n_dec, D_MODEL)),
        ],
        out_shape=[
            jax.ShapeDtypeStruct((rows, D_MODEL), F32),
            jax.ShapeDtypeStruct((n_dec, D_MODEL), F32),
        ],
        scratch_shapes=[
            pltpu.VMEM((D_MODEL, 2 * ATTN_WIDTH), BF16),
            pltpu.VMEM((ATTN_WIDTH, D_MODEL), BF16),
            pltpu.VMEM((tm, D_MODEL), BF16),
            pltpu.VMEM((tm, ATTN_WIDTH), BF16),
            pltpu.VMEM((tm, ATTN_WIDTH), BF16),
            pltpu.SemaphoreType.DMA((2,)),
        ],
        compiler_params=pltpu.CompilerParams(
            dimension_semantics=("arbitrary",), vmem_limit_bytes=VMEM_LIMIT),
        name="attn_layer",
    )(sinks, x2d, xs, k3, v3, sinks_t, g, w_in, q_gain, kt_band, v_band, w_out)


def kernel(x_prompt, x_sample, state_pool, cache_k_win, cache_v_win, norm_a, w_in_a, w_grp_a, scale_a, w_out_a, norm_kv, w_kv, k_norm, norm_b, w_in_b, q_norm, sinks, w_out_b):
    bsz, seq, _ = x_prompt.shape
    n_dec = x_sample.shape[0]
    assert x_sample.shape[1] == 1 and cache_k_win.shape[1] == WINDOW
    tm_a, tm_b, tm_kv = 256, 256, 1024
    kv_chunk, kv_dec_rows = 256, 8

    hp = x_prompt.reshape(bsz * seq, D_MODEL)
    hs = x_sample[:, 0]
    state_t = state_pool.transpose(0, 2, 1, 3)
    w_grp_rows = w_grp_a.reshape(N_A_LAYERS, N_POOL_GROUPS * POOL_GROUP, POOL_GROUP)

    assert N_A_LAYERS == 2 and N_B_LAYERS == 2
    pool_args = (hp, hs, state_t, norm_a[:, None])
    hp, hist0, hs, state0, a1_in, a1_grp, a1_out, b1_in, b1_out = _pool_layer(
        *pool_args, (w_in_a, w_grp_rows, w_out_a), scale_a[:, None], layer=0, own_bf16=False,
        convert=[(w_in_a, 1), (w_grp_rows, 1), (w_out_a, 1), (w_in_b, 1), (w_out_b, 1)],
        prev_states=[], seq=seq, tm=tm_a)
    hp, hist1, hs, states, b0_in, b0_out = _pool_layer(
        hp, hs, state_t, norm_a[:, None], (a1_in, a1_grp, a1_out), scale_a[:, None], layer=1, own_bf16=True,
        convert=[(w_in_b, 0), (w_out_b, 0)], prev_states=[state0], seq=seq, tm=tm_a)
    pool_p = [h[:, POOL_HIST - POOL_BUF:] for h in (hist0, hist1)]
    attn_weights = [(b0_in, b0_out), (b1_in, b1_out)]

    def keys_minor(t):
        return t.transpose(0, 2, 3, 1).reshape(t.shape[0], KV_WIDTH, t.shape[1])

    def keys_major(t):
        return t.reshape(t.shape[0], N_KV_HEADS, HEAD_DIM, t.shape[2]).transpose(0, 3, 1, 2)

    kt_band, v_band, k_win_p, v_win_p, k_win_s, v_win_s, k16, v16 = _shared_kv(
        hp, hs, keys_minor(cache_k_win), keys_minor(cache_v_win),
        norm_kv[None], w_kv, jnp.tile(k_norm, 2)[None], k_norm[:, None],
        seq=seq, tm=tm_kv, chunk=kv_chunk, dec_rows=kv_dec_rows)

    q_gain = jnp.tile(q_norm, (1, N_HEADS))[:, None]
    for j in range(N_B_LAYERS):
        hp, hs = _attn_layer(hp, hs, k16, v16, sinks, sinks.T, norm_b[:, None], attn_weights[j][0], q_gain,
                             kt_band, v_band, attn_weights[j][1], layer=j, seq=seq, tm=tm_b)

    return (hp.reshape(bsz, seq, D_MODEL), hs[:, None],
            jnp.stack(pool_p, axis=0), states.transpose(0, 2, 1, 3),
            keys_major(k_win_p), keys_major(v_win_p), keys_major(k_win_s), keys_major(v_win_s))
```

```python
import functools
import math

import jax
import jax.numpy as jnp
from jax import lax
from jax.experimental import pallas as pl
from jax.experimental.pallas import tpu as pltpu

D_MODEL = 2048
N_A_LAYERS = 2
N_B_LAYERS = 2
POOL_WINDOWS = (2, 4, 8, 16)
N_POOL_GROUPS = len(POOL_WINDOWS)
POOL_WIDTH = D_MODEL
POOL_GROUP = POOL_WIDTH // N_POOL_GROUPS
POOL_BUF = max(POOL_WINDOWS) - 1
POOL_HIST = POOL_BUF + 1
HEAD_DIM = 64
N_HEADS = D_MODEL // HEAD_DIM
N_KV_HEADS = N_HEADS // 8
GQA_GROUP = N_HEADS // N_KV_HEADS
ATTN_WIDTH = N_HEADS * HEAD_DIM
KV_WIDTH = N_KV_HEADS * HEAD_DIM
WINDOW = 128
BLOCK = WINDOW
PAST_LEN = 16384
RMS_EPS = 1e-6
NEG_INF = -1e30
LOG2E = math.log2(math.e)
MIN_SLOPE = 2.0 ** -8
PAIR = 2 * HEAD_DIM
N_PAIRS = N_HEADS // 2
VMEM_LIMIT = 62 * 1024 * 1024

BF16 = jnp.bfloat16
F32 = jnp.float32


def _rms(x, g):
    return x * lax.rsqrt(jnp.mean(x * x, axis=-1, keepdims=True) + RMS_EPS) * g


def _silu(z):
    return z * jax.nn.sigmoid(z)


def _dot(a, b):
    return jnp.dot(a, b, preferred_element_type=F32)


def _resident(shape):
    return pl.BlockSpec(shape, lambda *_: (0,) * len(shape), pipeline_mode=pl.Buffered(1))


def _tile_of(s, n_tiles):
    return jnp.clip(s - 1, 0, n_tiles - 1)


def _weight_blocks(src, dst16, block_rows, block_cols):
    rows, cols = dst16.shape
    plan = []
    for r in range(0, rows, block_rows):
        for c in range(0, cols, block_cols):
            w = min(block_cols, cols - c)
            plan.append((src.at[pl.ds(r, block_rows), pl.ds(c, w)],
                         dst16.at[pl.ds(r, block_rows), pl.ds(c, w)], (block_rows, w)))
    return plan


def _load_weights_bf16(plan, slots, sem):
    n_slots = len(slots)
    in_flight = n_slots - 1

    def copy(c):
        src, _, (r, w) = plan[c]
        k = c % n_slots
        return pltpu.make_async_copy(src, slots[k].at[pl.ds(0, r), pl.ds(0, w)], sem.at[k])

    for c in range(min(in_flight, len(plan))):
        copy(c).start()
    for c in range(len(plan)):
        if c + in_flight < len(plan):
            copy(c + in_flight).start()
        copy(c).wait()
        _, dst, (r, w) = plan[c]
        dst[...] = slots[c % n_slots][0:r, 0:w].astype(BF16)


def _fetch_bf16(pairs, sem):
    copies = [pltpu.make_async_copy(src, dst, sem.at[i]) for i, (src, dst) in enumerate(pairs)]
    for cp in copies:
        cp.start()
    for cp in copies:
        cp.wait()


def _half_tiles(*refs):
    slots = []
    for ref in refs:
        half = ref.shape[0] // 2
        slots += [ref.at[pl.ds(0, half)], ref.at[pl.ds(half, half)]]
    return slots


def _pair_inv_rms(pair, pair_lo, scale):
    sq = pair * pair
    lo = jnp.sum(jnp.where(pair_lo, sq, 0.0), axis=-1, keepdims=True)
    hi = jnp.sum(jnp.where(pair_lo, 0.0, sq), axis=-1, keepdims=True)
    ms = jnp.where(pair_lo, lo, hi) * (1.0 / HEAD_DIM)
    return lax.rsqrt(ms + RMS_EPS) * scale


def _pool_tail(x_ref, u, z, window_sums, inv_cnt, wgrp_ref, scale_ref, wout_ref):
    mixed = []
    for g in range(N_POOL_GROUPS):
        cols = slice(g * POOL_GROUP, (g + 1) * POOL_GROUP)
        p = window_sums[g] * inv_cnt[g] - u[:, cols]
        mixed.append(_dot(p.astype(BF16), wgrp_ref[cols, :]))
    y = jnp.concatenate(mixed, axis=-1) * scale_ref[...] * _silu(z)
    return x_ref[...] + _dot(y.astype(BF16), wout_ref[...])


def _pool_kernel(*refs, layer, tm, tiles_per_seq, n_tiles, own_bf16, n_conv):
    (x_ref, xs_ref, st_ref, g_ref, win_ref, wgrp_ref, scale_ref, wout_ref) = refs[:8]
    conv_src = refs[8:8 + n_conv]
    (out_ref, hist_ref, outs_ref, newst_ref) = refs[8 + n_conv:12 + n_conv]
    conv_dst = refs[12 + n_conv:12 + 2 * n_conv]
    (win16_ref, wgrp16_ref, wout16_ref, uext_ref, sem_ref) = refs[12 + 2 * n_conv:]
    g_ref, scale_ref = g_ref.at[pl.ds(layer, 1)], scale_ref.at[pl.ds(layer, 1)]
    s = pl.program_id(0)

    @pl.when(s == 0)
    def _():
        if own_bf16:
            _fetch_bf16([(win_ref, win16_ref), (wgrp_ref, wgrp16_ref), (wout_ref, wout16_ref)], sem_ref)
            return
        slots = _half_tiles(out_ref, uext_ref.at[pl.ds(POOL_HIST, tm)])
        rows = tm // 2
        assert POOL_GROUP // rows == len(slots) and N_POOL_GROUPS * POOL_GROUP == D_MODEL
        grp = [(wgrp_ref.at[layer, pl.ds(g * POOL_GROUP + r * rows, rows)],
                slots[r].at[:, pl.ds(g * POOL_GROUP, POOL_GROUP)],
                wgrp16_ref.at[pl.ds(g * POOL_GROUP + r * rows, rows)])
               for g in range(N_POOL_GROUPS) for r in range(len(slots))]
        grp_copies = [pltpu.make_async_copy(src, stage, sem_ref.at[0]) for src, stage, _ in grp]
        for cp in grp_copies:
            cp.start()
        for cp in grp_copies:
            cp.wait()
        for _, stage, dst in grp:
            dst[...] = stage[...].astype(BF16)
        plan = (_weight_blocks(win_ref.at[layer], win16_ref, rows, D_MODEL)
                + _weight_blocks(wout_ref.at[layer], wout16_ref, rows, D_MODEL))
        _load_weights_bf16(plan, slots, sem_ref)

    @pl.when((s >= 1) & (s <= n_tiles))
    def _():
        i = (s - 1) % tiles_per_seq

        @pl.when(i == 0)
        def _():
            uext_ref[0:POOL_HIST, :] = jnp.zeros((POOL_HIST, POOL_WIDTH), F32)

        x = x_ref[...]
        xg = (x * g_ref[...]).astype(BF16)
        inv_rms = lax.rsqrt(jnp.mean(x * x, axis=-1, keepdims=True) + RMS_EPS)
        u = _dot(xg, win16_ref[:, :POOL_WIDTH]) * inv_rms
        z = _dot(xg, win16_ref[:, POOL_WIDTH:]) * inv_rms
        uext_ref[POOL_HIST:POOL_HIST + tm, :] = u

        pos = i * tm + lax.broadcasted_iota(jnp.int32, (tm, 1), 0)
        sums, inv_cnt = [], []
        for g, w in enumerate(POOL_WINDOWS):
            cols = slice(g * POOL_GROUP, (g + 1) * POOL_GROUP)
            acc = u[:, cols]
            for k in range(1, w):
                acc = acc + uext_ref[POOL_HIST - k:POOL_HIST - k + tm, cols]
            sums.append(acc)
            inv_cnt.append(1.0 / jnp.minimum(pos + 1, w).astype(F32))
        out_ref[...] = _pool_tail(x_ref, u, z, sums, inv_cnt, wgrp16_ref, scale_ref, wout16_ref)

        tail = uext_ref[tm:tm + POOL_HIST, :]
        hist_ref[0] = tail
        uext_ref[0:POOL_HIST, :] = tail

        for src, dst in zip(conv_src, conv_dst):
            dst[...] = src[...].astype(BF16)

    @pl.when(s == n_tiles + 1)
    def _():
        h = _rms(xs_ref[...], g_ref[...]).astype(BF16)
        u = _dot(h, win16_ref[:, :POOL_WIDTH])
        z = _dot(h, win16_ref[:, POOL_WIDTH:])
        for r in range(POOL_BUF - 1):
            newst_ref[r] = st_ref[0, r + 1]
        newst_ref[POOL_BUF - 1] = u
        sums, inv_cnt = [], []
        for g, w in enumerate(POOL_WINDOWS):
            cols = slice(g * POOL_GROUP, (g + 1) * POOL_GROUP)
            acc = u[:, cols]
            for k in range(1, w):
                acc = acc + st_ref[0, POOL_BUF - k, :, cols]
            sums.append(acc)
            inv_cnt.append(1.0 / float(min(PAST_LEN + 1, w)))
        outs_ref[...] = _pool_tail(xs_ref, u, z, sums, inv_cnt, wgrp16_ref, scale_ref, wout16_ref)


def _convert_specs(convert, n_tiles):
    in_specs, out_specs, out_shape = [], [], []
    for arr, lyr in convert:
        _, rows, cols = arr.shape
        step_rows = rows // n_tiles
        in_specs.append(pl.BlockSpec((None, step_rows, cols),
                                     lambda s, lyr=lyr: (lyr, _tile_of(s, n_tiles), 0)))
        out_specs.append(pl.BlockSpec((step_rows, cols), lambda s: (_tile_of(s, n_tiles), 0)))
        out_shape.append(jax.ShapeDtypeStruct((rows, cols), BF16))
    return in_specs, out_specs, out_shape


def _pool_layer(x2d, xs, state, g, weights, scale, *, layer, own_bf16, convert, seq, tm):
    rows = x2d.shape[0]
    n_dec = xs.shape[0]
    n_tiles = rows // tm
    tiles_per_seq = seq // tm
    tile = lambda s: (_tile_of(s, n_tiles), 0)
    one = pl.Buffered(1)
    hbm = pl.BlockSpec(memory_space=pl.ANY)
    conv_in, conv_out, conv_shape = _convert_specs(convert, n_tiles)
    w_in, w_grp, w_out = weights
    return pl.pallas_call(
        functools.partial(_pool_kernel, layer=layer, tm=tm, tiles_per_seq=tiles_per_seq, n_tiles=n_tiles,
                          own_bf16=own_bf16, n_conv=len(convert)),
        grid=(n_tiles + 2,),
        in_specs=[
            pl.BlockSpec((tm, D_MODEL), tile),
            _resident((n_dec, D_MODEL)),
            pl.BlockSpec((1, POOL_BUF, n_dec, POOL_WIDTH), lambda s: (layer, 0, 0, 0), pipeline_mode=one),
            _resident(g.shape),
            hbm,
            hbm,
            _resident(scale.shape),
            hbm,
        ] + conv_in,
        out_specs=[
            pl.BlockSpec((tm, D_MODEL), tile),
            pl.BlockSpec((1, POOL_HIST, POOL_WIDTH),
                         lambda s: (_tile_of(s, n_tiles) // tiles_per_seq, 0, 0)),
            _resident((n_dec, D_MODEL)),
            _resident((POOL_BUF, n_dec, POOL_WIDTH)),
        ] + conv_out,
        out_shape=[
            jax.ShapeDtypeStruct((rows, D_MODEL), F32),
            jax.ShapeDtypeStruct((rows // seq, POOL_HIST, POOL_WIDTH), F32),
            jax.ShapeDtypeStruct((n_dec, D_MODEL), F32),
            jax.ShapeDtypeStruct((POOL_BUF, n_dec, POOL_WIDTH), F32),
        ] + conv_shape,
        scratch_shapes=[
            pltpu.VMEM((D_MODEL, 2 * POOL_WIDTH), BF16),
            pltpu.VMEM((N_POOL_GROUPS * POOL_GROUP, POOL_GROUP), BF16),
            pltpu.VMEM((POOL_WIDTH, D_MODEL), BF16),
            pltpu.VMEM((POOL_HIST + tm, POOL_WIDTH), F32),
            pltpu.SemaphoreType.DMA((4,)),
        ],
        compiler_params=pltpu.CompilerParams(
            dimension_semantics=("arbitrary",), vmem_limit_bytes=VMEM_LIMIT),
        name="pool_layer",
    )(x2d, xs, state, g, w_in, w_grp, scale, w_out, *[arr for arr, _ in convert])


def _kv_project(x, g_ref, w16_ref, kg_ref):
    h = _rms(x, g_ref[...]).astype(BF16)
    kv = _dot(h, w16_ref[...])
    pair_lo = lax.broadcasted_iota(jnp.int32, (1, PAIR), 1) < HEAD_DIM
    k_pairs = []
    for p in range(KV_WIDTH // PAIR):
        kp = kv[:, p * PAIR:(p + 1) * PAIR]
        k_pairs.append(kp * _pair_inv_rms(kp, pair_lo, 1.0) * kg_ref[...])
    return k_pairs, kv[:, KV_WIDTH:]


def _dup_heads(col, pair_lo):
    rolled = pltpu.roll(col, HEAD_DIM, 1)
    return jnp.where(pair_lo, col, rolled), jnp.where(pair_lo, rolled, col)


def _slide_window(cache_ref, new_t, win_ref, win16_ref):
    n, _, n_keys = cache_ref.shape
    last = lax.broadcasted_iota(jnp.int32, (1, n_keys), 1) == n_keys - 1
    for b in range(n):
        slid = jnp.where(last, new_t[:, b:b + 1], pltpu.roll(cache_ref[b], n_keys - 1, 1))
        win_ref[b] = slid
        win16_ref[b] = slid.astype(BF16)


def _kv_kernel(x_ref, xs_ref, ck_ref, cv_ref, g_ref, wkv_ref, kg_ref, kgc_ref,
               kt_ref, vb_ref, kwin_ref, vwin_ref, kws_ref, vws_ref, k16_ref, v16_ref,
               w16_ref, wkt16_ref, pk_ref, pv_ref, *, tm, chunk, tiles_per_seq, dec_steps):
    s = pl.program_id(0)
    nblk = tm // BLOCK
    blk_per_chunk = chunk // BLOCK
    pair_lo = lax.broadcasted_iota(jnp.int32, (1, PAIR), 1) < HEAD_DIM

    @pl.when(s == 0)
    def _():
        w16_ref[...] = wkv_ref[...].astype(BF16)
        for c in range(KV_WIDTH // BLOCK):
            cols = slice(c * BLOCK, (c + 1) * BLOCK)
            wkt16_ref[cols, :] = wkv_ref[:, cols].T.astype(BF16)

    @pl.when(s % tiles_per_seq == 0)
    def _():
        pk_ref[...] = jnp.zeros(pk_ref.shape, BF16)
        pv_ref[...] = jnp.zeros(pv_ref.shape, BF16)

    @pl.when(s < dec_steps)
    def _():
        k_pairs, v = _kv_project(xs_ref[...], g_ref, w16_ref, kg_ref)
        n_dec = xs_ref.shape[0]
        pad = jnp.zeros((BLOCK - n_dec, KV_WIDTH), F32)
        kn_t = jnp.concatenate([jnp.concatenate(k_pairs, axis=1), pad], axis=0)
        vn_t = jnp.concatenate([v, pad], axis=0)
        kn_t = jnp.concatenate([kn_t[:, c * BLOCK:(c + 1) * BLOCK].T for c in range(KV_WIDTH // BLOCK)], axis=0)
        vn_t = jnp.concatenate([vn_t[:, c * BLOCK:(c + 1) * BLOCK].T for c in range(KV_WIDTH // BLOCK)], axis=0)
        _slide_window(ck_ref, kn_t, kws_ref, k16_ref)
        _slide_window(cv_ref, vn_t, vws_ref, v16_ref)

    for ch in range(tm // chunk):
        h = _rms(x_ref[ch * chunk:(ch + 1) * chunk, :], g_ref[...]).astype(BF16)
        v = _dot(h, w16_ref[:, KV_WIDTH:])
        k_t = lax.dot_general(wkt16_ref[...], h, (((1,), (1,)), ((), ())),
                              preferred_element_type=F32)
        k_heads = []
        for kh in range(N_KV_HEADS):
            kc = k_t[kh * HEAD_DIM:(kh + 1) * HEAD_DIM, :]
            ms = jnp.mean(kc * kc, axis=0, keepdims=True)
            k_heads.append(kc * lax.rsqrt(ms + RMS_EPS) * kgc_ref[...])
        for jc in range(blk_per_chunk):
            jb = ch * blk_per_chunk + jc
            rows = slice(jc * BLOCK, (jc + 1) * BLOCK)
            for p in range(KV_WIDTH // PAIR):
                v_dup = _dup_heads(v[rows, p * PAIR:(p + 1) * PAIR], pair_lo)
                for half in range(2):
                    kh = 2 * p + half
                    cols = slice(kh * PAIR, (kh + 1) * PAIR)
                    kt_f32 = k_heads[kh][:, rows]
                    kt_cur = jnp.concatenate([kt_f32, kt_f32], axis=0).astype(BF16)
                    v_cur = v_dup[half].astype(BF16)
                    if jb == nblk - 1:
                        hd = slice(kh * HEAD_DIM, (kh + 1) * HEAD_DIM)
                        kwin_ref[0, hd, :] = kt_f32
                        vwin_ref[0, hd, :] = v_dup[half].T[:HEAD_DIM]
                    kt_ref[0, jb, kh, :, 0:BLOCK] = pk_ref[kh]
                    kt_ref[0, jb, kh, :, BLOCK:2 * BLOCK] = kt_cur
                    vb_ref[0, jb, 0:BLOCK, cols] = pv_ref[:, cols]
                    vb_ref[0, jb, BLOCK:2 * BLOCK, cols] = v_cur
                    pk_ref[kh] = kt_cur
                    pv_ref[:, cols] = v_cur


def _shared_kv(x2d, xs, cache_kt, cache_vt, g, w_kv, k_gain_pair, k_gain_col, *, seq, tm, chunk, dec_rows):
    rows = x2d.shape[0]
    bsz = rows // seq
    n_dec = xs.shape[0]
    n_tiles = rows // tm
    tiles_per_seq = seq // tm
    nblk = tm // BLOCK
    dec_steps = n_dec // dec_rows
    assert dec_steps <= n_tiles and dec_steps * dec_rows == n_dec
    dec = lambda s: (jnp.minimum(s, dec_steps - 1), 0, 0)
    win = (dec_rows,) + cache_kt.shape[1:]
    return pl.pallas_call(
        functools.partial(_kv_kernel, tm=tm, chunk=chunk, tiles_per_seq=tiles_per_seq, dec_steps=dec_steps),
        grid=(n_tiles,),
        in_specs=[
            pl.BlockSpec((tm, D_MODEL), lambda s: (s, 0)),
            pl.BlockSpec((dec_rows, D_MODEL), lambda s: (jnp.minimum(s, dec_steps - 1), 0)),
            pl.BlockSpec(win, dec),
            pl.BlockSpec(win, dec),
            _resident((1, D_MODEL)),
            _resident((D_MODEL, 2 * KV_WIDTH)),
            _resident((1, PAIR)),
            _resident((HEAD_DIM, 1)),
        ],
        out_specs=[
            pl.BlockSpec((1, nblk, N_KV_HEADS, PAIR, 2 * BLOCK),
                         lambda s: (s // tiles_per_seq, s % tiles_per_seq, 0, 0, 0)),
            pl.BlockSpec((1, nblk, 2 * BLOCK, N_KV_HEADS * PAIR),
                         lambda s: (s // tiles_per_seq, s % tiles_per_seq, 0, 0)),
            pl.BlockSpec((1, KV_WIDTH, BLOCK), lambda s: (s // tiles_per_seq, 0, 0)),
            pl.BlockSpec((1, KV_WIDTH, BLOCK), lambda s: (s // tiles_per_seq, 0, 0)),
            pl.BlockSpec(win, dec),
            pl.BlockSpec(win, dec),
            pl.BlockSpec(win, dec),
            pl.BlockSpec(win, dec),
        ],
        out_shape=[
            jax.ShapeDtypeStruct((bsz, seq // BLOCK, N_KV_HEADS, PAIR, 2 * BLOCK), BF16),
            jax.ShapeDtypeStruct((bsz, seq // BLOCK, 2 * BLOCK, N_KV_HEADS * PAIR), BF16),
            jax.ShapeDtypeStruct((bsz, KV_WIDTH, BLOCK), F32),
            jax.ShapeDtypeStruct((bsz, KV_WIDTH, BLOCK), F32),
            jax.ShapeDtypeStruct(cache_kt.shape, F32),
            jax.ShapeDtypeStruct(cache_vt.shape, F32),
            jax.ShapeDtypeStruct(cache_kt.shape, BF16),
            jax.ShapeDtypeStruct(cache_vt.shape, BF16),
        ],
        scratch_shapes=[
            pltpu.VMEM((D_MODEL, 2 * KV_WIDTH), BF16),
            pltpu.VMEM((KV_WIDTH, D_MODEL), BF16),
            pltpu.VMEM((N_KV_HEADS, PAIR, BLOCK), BF16),
            pltpu.VMEM((BLOCK, N_KV_HEADS * PAIR), BF16),
        ],
        compiler_params=pltpu.CompilerParams(
            dimension_semantics=("arbitrary",), vmem_limit_bytes=VMEM_LIMIT),
        name="shared_kv",
    )(x2d, xs, cache_kt, cache_vt, g, w_kv, k_gain_pair, k_gain_col)


def _normed_q_pair(q, pr, qg_ref, pair_lo, scale):
    cols = slice(pr * PAIR, (pr + 1) * PAIR)
    qp = q[:, cols]
    return qp * _pair_inv_rms(qp, pair_lo, scale) * qg_ref[:, cols]


def _decode_attention(q, k3, v3, sink_col):
    n, n_keys = k3.shape[0], k3.shape[2]
    gw = GQA_GROUP * HEAD_DIM
    rows = n * GQA_GROUP
    ri = lax.broadcasted_iota(jnp.int32, (rows, n), 0)
    ci = lax.broadcasted_iota(jnp.int32, (rows, n), 1)
    rep = jnp.where(ri // GQA_GROUP == ci, 1.0, 0.0).astype(BF16)
    row_g = lax.broadcasted_iota(jnp.int32, (rows, gw), 0) % GQA_GROUP
    col_g = lax.broadcasted_iota(jnp.int32, (rows, gw), 1) // HEAD_DIM
    own = row_g == col_g
    fj = lax.broadcasted_iota(jnp.int32, (gw, KV_WIDTH), 0)
    fc = lax.broadcasted_iota(jnp.int32, (gw, KV_WIDTH), 1)
    tc = lax.broadcasted_iota(jnp.int32, (KV_WIDTH, gw), 0)
    tj = lax.broadcasted_iota(jnp.int32, (KV_WIDTH, gw), 1)
    qb = q.astype(BF16)

    q_groups = []
    for kh in range(N_KV_HEADS):
        to_kv = jnp.where((fj % HEAD_DIM == fc % HEAD_DIM) & (fc // HEAD_DIM == kh), 1.0, 0.0).astype(BF16)
        mine = jnp.where(own, _dot(rep, qb[:, kh * gw:(kh + 1) * gw]), 0.0).astype(BF16)
        q_groups.append(_dot(mine, to_kv).astype(BF16).reshape(n, GQA_GROUP, KV_WIDTH))
    qm3 = jnp.concatenate(q_groups, axis=1)
    s = jnp.einsum("bhc,bck->bhk", qm3, k3, preferred_element_type=F32)
    h_col = lax.broadcasted_iota(jnp.int32, (N_HEADS, 1), 0).astype(F32)
    slope = jnp.exp2(-8.0 * (h_col + 1.0) / N_HEADS)[None]
    dist = (n_keys - 1 - lax.broadcasted_iota(jnp.int32, (1, n_keys), 1)).astype(F32)[None]
    s = s - slope * dist
    sink = sink_col[None]
    m = jnp.maximum(jnp.max(s, axis=-1, keepdims=True), sink)
    p = jnp.exp(s - m)
    den = jnp.sum(p, axis=-1, keepdims=True) + jnp.exp(sink - m)
    o3 = jnp.einsum("bhk,bck->bhc", p.astype(BF16), v3, preferred_element_type=F32) / den
    outs = []
    for kh in range(N_KV_HEADS):
        from_kv = jnp.where((tj % HEAD_DIM == tc % HEAD_DIM) & (tc // HEAD_DIM == kh), 1.0, 0.0).astype(BF16)
        o2 = o3[:, kh * GQA_GROUP:(kh + 1) * GQA_GROUP, :].reshape(rows, KV_WIDTH)
        hi = o2.astype(BF16)
        lo = (o2 - hi.astype(F32)).astype(BF16)
        spread = jnp.where(own, _dot(hi, from_kv) + _dot(lo, from_kv), 0.0)
        outs.append(jnp.sum(spread.reshape(n, GQA_GROUP, gw), axis=1))
    return jnp.concatenate(outs, axis=-1)


def _attn_kernel(sink_ref, x_ref, xs_ref, k3_ref, v3_ref, sinkc_ref, g_ref, win_ref, qg_ref,
                 kt_ref, vb_ref, wout_ref,
                 out_ref, outs_ref,
                 win16_ref, wout16_ref, h_ref, qlo_ref, qhi_ref, kbuf_ref, vbuf_ref, sem_ref,
                 *, layer, tm, tiles_per_seq, n_tiles):
    s = pl.program_id(0)
    nblk = tm // BLOCK
    pair_lo = lax.broadcasted_iota(jnp.int32, (1, PAIR), 1) < HEAD_DIM
    g_ref, qg_ref = g_ref.at[pl.ds(layer, 1)], qg_ref.at[pl.ds(layer, 1)]

    @pl.when(s == 0)
    def _():
        _fetch_bf16([(win_ref, win16_ref), (wout_ref, wout16_ref)], sem_ref)

    @pl.when((s >= 1) & (s <= n_tiles))
    def _():
        i = (s - 1) % tiles_per_seq
        h_ref[...] = _rms(x_ref[...], g_ref[...]).astype(BF16)
        for kh in range(N_KV_HEADS):
            gcols = slice(kh * GQA_GROUP * HEAD_DIM, (kh + 1) * GQA_GROUP * HEAD_DIM)
            q = _dot(h_ref[...], win16_ref[:, gcols])
            for pr in range(GQA_GROUP // 2):
                cols = slice(pr * PAIR, (pr + 1) * PAIR)
                ocols = slice((kh * (GQA_GROUP // 2) + pr) * PAIR, (kh * (GQA_GROUP // 2) + pr + 1) * PAIR)
                qp = q[:, cols]
                qn = qp * _pair_inv_rms(qp, pair_lo, HEAD_DIM ** -0.5 * LOG2E) * qg_ref[:, ocols]
                qlo_ref[:, ocols] = jnp.where(pair_lo, qn, 0.0).astype(BF16)
                qhi_ref[:, ocols] = jnp.where(pair_lo, 0.0, qn).astype(BF16)

        r_idx = lax.broadcasted_iota(jnp.int32, (BLOCK, 2 * BLOCK), 0)
        c_idx = lax.broadcasted_iota(jnp.int32, (BLOCK, 2 * BLOCK), 1)
        dist = r_idx + BLOCK - c_idx
        in_window = (dist >= 0) & (dist < WINDOW)
        neg_dist = -dist.astype(F32)

        for blk in range(nblk):
            rows = slice(blk * BLOCK, (blk + 1) * BLOCK)
            first_key = (i * nblk + blk - 1) * BLOCK
            valid = in_window & (first_key + c_idx >= 0)
            nd = jnp.where(valid, neg_dist, NEG_INF / MIN_SLOPE)
            for kh in range(N_KV_HEADS):
                kt = kt_ref[0, blk, kh]
                vdup = vb_ref[0, blk, :, kh * PAIR:(kh + 1) * PAIR]
                pieces = []
                for pr in range(GQA_GROUP // 2):
                    cols = slice((kh * (GQA_GROUP // 2) + pr) * PAIR,
                                 (kh * (GQA_GROUP // 2) + pr + 1) * PAIR)
                    pieces.append(qlo_ref[rows, cols])
                    pieces.append(qhi_ref[rows, cols])
                s_all = _dot(jnp.concatenate(pieces, axis=0), kt)
                probs, inv_den = [], []
                for g_idx in range(GQA_GROUP):
                    head = kh * GQA_GROUP + g_idx
                    slope2 = 2.0 ** (-8.0 * (head + 1) / N_HEADS) * LOG2E
                    sc = s_all[g_idx * BLOCK:(g_idx + 1) * BLOCK] + slope2 * nd
                    sink2 = sink_ref[layer, head] * LOG2E
                    m = jnp.maximum(jnp.max(sc, axis=-1, keepdims=True), sink2)
                    p = jnp.exp2(sc - m)
                    den = jnp.sum(p, axis=-1, keepdims=True) + jnp.exp2(sink2 - m)
                    probs.append(p.astype(BF16))
                    inv_den.append(1.0 / den)
                o_all = _dot(jnp.concatenate(probs, axis=0), vdup)
                for pr in range(GQA_GROUP // 2):
                    cols = slice((kh * (GQA_GROUP // 2) + pr) * PAIR,
                                 (kh * (GQA_GROUP // 2) + pr + 1) * PAIR)
                    o_even = o_all[(2 * pr) * BLOCK:(2 * pr + 1) * BLOCK] * inv_den[2 * pr]
                    o_odd = o_all[(2 * pr + 1) * BLOCK:(2 * pr + 2) * BLOCK] * inv_den[2 * pr + 1]
                    out_ref[rows, cols] = jnp.where(pair_lo, o_even, o_odd)

        z = _dot(h_ref[...], win16_ref[:, ATTN_WIDTH:])
        y = out_ref[...] * _silu(z)
        out_ref[...] = x_ref[...] + _dot(y.astype(BF16), wout16_ref[...])

    @pl.when(s == n_tiles + 1)
    def _():
        kv_in = [pltpu.make_async_copy(k3_ref, kbuf_ref, sem_ref.at[0]),
                 pltpu.make_async_copy(v3_ref, vbuf_ref, sem_ref.at[1])]
        for cp in kv_in:
            cp.start()
        x = xs_ref[...]
        h = _rms(x, g_ref[...]).astype(BF16)
        q = _dot(h, win16_ref[:, :ATTN_WIDTH])
        z = _dot(h, win16_ref[:, ATTN_WIDTH:])
        qn = jnp.concatenate(
            [_normed_q_pair(q, pr, qg_ref, pair_lo, HEAD_DIM ** -0.5) for pr in range(N_PAIRS)], axis=-1)
        for cp in kv_in:
            cp.wait()
        o = _decode_attention(qn, kbuf_ref[...], vbuf_ref[...], sinkc_ref[:, layer:layer + 1])
        y = o * _silu(z)
        outs_ref[...] = x + _dot(y.astype(BF16), wout16_ref[...])


def _attn_layer(x2d, xs, k3, v3, sinks, sinks_t, g, w_in, q_gain, kt_band, v_band, w_out,
                *, layer, seq, tm):
    rows = x2d.shape[0]
    n_dec = xs.shape[0]
    n_tiles = rows // tm
    tiles_per_seq = seq // tm
    nblk = tm // BLOCK
    tile = lambda s: (_tile_of(s, n_tiles), 0)
    band = lambda s: (_tile_of(s, n_tiles) // tiles_per_seq, _tile_of(s, n_tiles) % tiles_per_seq)
    one = pl.Buffered(1)
    hbm = pl.BlockSpec(memory_space=pl.ANY)
    return pl.pallas_call(
        functools.partial(_attn_kernel, layer=layer, tm=tm, tiles_per_seq=tiles_per_seq, n_tiles=n_tiles),
        grid=(n_tiles + 2,),
        in_specs=[
            pl.BlockSpec(memory_space=pltpu.SMEM),
            pl.BlockSpec((tm, D_MODEL), tile),
            _resident((n_dec, D_MODEL)),
            hbm,
            hbm,
            _resident(sinks_t.shape),
            _resident(g.shape),
            hbm,
            _resident(q_gain.shape),
            pl.BlockSpec((1, nblk, N_KV_HEADS, PAIR, 2 * BLOCK), lambda s: band(s) + (0, 0, 0)),
            pl.BlockSpec((1, nblk, 2 * BLOCK, N_KV_HEADS * PAIR), lambda s: band(s) + (0, 0)),
            hbm,
        ],
        out_specs=[
            pl.BlockSpec((tm, D_MODEL), tile),
            _resident((n_dec, D_MODEL)),
        ],
        out_shape=[
            jax.ShapeDtypeStruct((rows, D_MODEL), F32),
            jax.ShapeDtypeStruct((n_dec, D_MODEL), F32),
        ],
        scratch_shapes=[
            pltpu.VMEM((D_MODEL, 2 * ATTN_WIDTH), BF16),
            pltpu.VMEM((ATTN_WIDTH, D_MODEL), BF16),
            pltpu.VMEM((tm, D_MODEL), BF16),
            pltpu.VMEM((tm, ATTN_WIDTH), BF16),
            pltpu.VMEM((tm, ATTN_WIDTH), BF16),
            pltpu.VMEM(k3.shape, BF16),
            pltpu.VMEM(v3.shape, BF16),
            pltpu.SemaphoreType.DMA((2,)),
        ],
        compiler_params=pltpu.CompilerParams(
            dimension_semantics=("arbitrary",), vmem_limit_bytes=VMEM_LIMIT),
        name="attn_layer",
    )(sinks, x2d, xs, k3, v3, sinks_t, g, w_in, q_gain, kt_band, v_band, w_out)


def kernel(x_prompt, x_sample, state_pool, cache_k_win, cache_v_win, norm_a, w_in_a, w_grp_a, scale_a, w_out_a, norm_kv, w_kv, k_norm, norm_b, w_in_b, q_norm, sinks, w_out_b):
    bsz, seq, _ = x_prompt.shape
    n_dec = x_sample.shape[0]
    assert x_sample.shape[1] == 1 and cache_k_win.shape[1] == WINDOW
    tm_a, tm_b, tm_kv = 256, 256, 1024
    kv_chunk, kv_dec_rows = 256, 8

    hp = x_prompt.reshape(bsz * seq, D_MODEL)
    hs = x_sample[:, 0]
    state_t = state_pool.transpose(0, 2, 1, 3)
    w_grp_rows = w_grp_a.reshape(N_A_LAYERS, N_POOL_GROUPS * POOL_GROUP, POOL_GROUP)

    assert N_A_LAYERS == 2 and N_B_LAYERS == 2
    hp, hist0, hs, state0, a1_in, a1_grp, a1_out, b1_in, b1_out = _pool_layer(
        hp, hs, state_t, norm_a, (w_in_a, w_grp_rows, w_out_a), scale_a, layer=0, own_bf16=False,
        convert=[(w_in_a, 1), (w_grp_rows, 1), (w_out_a, 1), (w_in_b, 1), (w_out_b, 1)], seq=seq, tm=tm_a)
    hp, hist1, hs, state1, b0_in, b0_out = _pool_layer(
        hp, hs, state_t, norm_a, (a1_in, a1_grp, a1_out), scale_a, layer=1, own_bf16=True,
        convert=[(w_in_b, 0), (w_out_b, 0)], seq=seq, tm=tm_a)
    pool_p = [h[:, POOL_HIST - POOL_BUF:] for h in (hist0, hist1)]
    pool_s = [state0, state1]
    attn_weights = [(b0_in, b0_out), (b1_in, b1_out)]

    def keys_minor(t):
        return t.transpose(0, 2, 3, 1).reshape(t.shape[0], KV_WIDTH, t.shape[1])

    def keys_major(t):
        return t.reshape(t.shape[0], N_KV_HEADS, HEAD_DIM, t.shape[2]).transpose(0, 3, 1, 2)

    kt_band, v_band, k_win_p, v_win_p, k_win_s, v_win_s, k16, v16 = _shared_kv(
        hp, hs, keys_minor(cache_k_win), keys_minor(cache_v_win),
        norm_kv[None], w_kv, jnp.tile(k_norm, 2)[None], k_norm[:, None],
        seq=seq, tm=tm_kv, chunk=kv_chunk, dec_rows=kv_dec_rows)

    q_gain = jnp.tile(q_norm, (1, N_HEADS))
    for j in range(N_B_LAYERS):
        hp, hs = _attn_layer(hp, hs, k16, v16, sinks, sinks.T, norm_b, attn_weights[j][0], q_gain,
                             kt_band, v_band, attn_weights[j][1], layer=j, seq=seq, tm=tm_b)

    return (hp.reshape(bsz, seq, D_MODEL), hs[:, None],
            jnp.stack(pool_p, axis=0), jnp.stack(pool_s, axis=0).transpose(0, 2, 1, 3),
            keys_major(k_win_p), keys_major(v_win_p), keys_major(k_win_s), keys_major(v_win_s))
```

```python
import functools
import math

import jax
import jax.numpy as jnp
from jax import lax
from jax.experimental import pallas as pl
from jax.experimental.pallas import tpu as pltpu

D_MODEL = 2048
N_A_LAYERS = 2
N_B_LAYERS = 2
POOL_WINDOWS = (2, 4, 8, 16)
N_POOL_GROUPS = len(POOL_WINDOWS)
POOL_WIDTH = D_MODEL
POOL_GROUP = POOL_WIDTH // N_POOL_GROUPS
POOL_BUF = max(POOL_WINDOWS) - 1
POOL_HIST = POOL_BUF + 1
HEAD_DIM = 64
N_HEADS = D_MODEL // HEAD_DIM
N_KV_HEADS = N_HEADS // 8
GQA_GROUP = N_HEADS // N_KV_HEADS
ATTN_WIDTH = N_HEADS * HEAD_DIM
KV_WIDTH = N_KV_HEADS * HEAD_DIM
WINDOW = 128
BLOCK = WINDOW
PAST_LEN = 16384
RMS_EPS = 1e-6
NEG_INF = -1e30
LOG2E = math.log2(math.e)
MIN_SLOPE = 2.0 ** -8
PAIR = 2 * HEAD_DIM
N_PAIRS = N_HEADS // 2
VMEM_LIMIT = 62 * 1024 * 1024
KV_RING_SLOTS = 3

BF16 = jnp.bfloat16
F32 = jnp.float32


def _rms(x, g):
    return x * lax.rsqrt(jnp.mean(x * x, axis=-1, keepdims=True) + RMS_EPS) * g


def _silu(z):
    return z * jax.nn.sigmoid(z)


def _dot(a, b):
    return jnp.dot(a, b, preferred_element_type=F32)


def _resident(shape):
    return pl.BlockSpec(shape, lambda *_: (0,) * len(shape), pipeline_mode=pl.Buffered(1))


def _tile_of(s, n_tiles):
    return jnp.clip(s - 1, 0, n_tiles - 1)


def _weight_blocks(src, dst16, block_rows, block_cols):
    rows, cols = dst16.shape
    plan = []
    for r in range(0, rows, block_rows):
        for c in range(0, cols, block_cols):
            w = min(block_cols, cols - c)
            plan.append((src.at[pl.ds(r, block_rows), pl.ds(c, w)],
                         dst16.at[pl.ds(r, block_rows), pl.ds(c, w)], (block_rows, w)))
    return plan


def _load_weights_bf16(plan, slots, sem):
    n_slots = len(slots)
    in_flight = n_slots - 1

    def copy(c):
        src, _, (r, w) = plan[c]
        k = c % n_slots
        return pltpu.make_async_copy(src, slots[k].at[pl.ds(0, r), pl.ds(0, w)], sem.at[k])

    for c in range(min(in_flight, len(plan))):
        copy(c).start()
    for c in range(len(plan)):
        if c + in_flight < len(plan):
            copy(c + in_flight).start()
        copy(c).wait()
        _, dst, (r, w) = plan[c]
        dst[...] = slots[c % n_slots][0:r, 0:w].astype(BF16)


def _fetch_bf16(pairs, sem):
    copies = [pltpu.make_async_copy(src, dst, sem.at[i]) for i, (src, dst) in enumerate(pairs)]
    for cp in copies:
        cp.start()
    for cp in copies:
        cp.wait()


def _half_tiles(*refs):
    slots = []
    for ref in refs:
        half = ref.shape[0] // 2
        slots += [ref.at[pl.ds(0, half)], ref.at[pl.ds(half, half)]]
    return slots


def _pair_inv_rms(pair, pair_lo, scale):
    sq = pair * pair
    lo = jnp.sum(jnp.where(pair_lo, sq, 0.0), axis=-1, keepdims=True)
    hi = jnp.sum(jnp.where(pair_lo, 0.0, sq), axis=-1, keepdims=True)
    ms = jnp.where(pair_lo, lo, hi) * (1.0 / HEAD_DIM)
    return lax.rsqrt(ms + RMS_EPS) * scale


def _pool_tail(x_ref, u, z, window_sums, inv_cnt, wgrp_ref, scale_ref, wout_ref):
    mixed = []
    for g in range(N_POOL_GROUPS):
        cols = slice(g * POOL_GROUP, (g + 1) * POOL_GROUP)
        p = window_sums[g] * inv_cnt[g] - u[:, cols]
        mixed.append(_dot(p.astype(BF16), wgrp_ref[cols, :]))
    y = jnp.concatenate(mixed, axis=-1) * scale_ref[...] * _silu(z)
    return x_ref[...] + _dot(y.astype(BF16), wout_ref[...])


def _pool_kernel(*refs, layer, tm, tiles_per_seq, n_tiles, own_bf16, n_conv):
    (x_ref, xs_ref, st_ref, g_ref, win_ref, wgrp_ref, scale_ref, wout_ref) = refs[:8]
    conv_src = refs[8:8 + n_conv]
    (out_ref, hist_ref, outs_ref, newst_ref) = refs[8 + n_conv:12 + n_conv]
    conv_dst = refs[12 + n_conv:12 + 2 * n_conv]
    (win16_ref, wgrp16_ref, wout16_ref, uext_ref, sem_ref) = refs[12 + 2 * n_conv:]
    g_ref, scale_ref = g_ref.at[pl.ds(layer, 1)], scale_ref.at[pl.ds(layer, 1)]
    s = pl.program_id(0)

    @pl.when(s == 0)
    def _():
        if own_bf16:
            _fetch_bf16([(win_ref, win16_ref), (wgrp_ref, wgrp16_ref), (wout_ref, wout16_ref)], sem_ref)
            return
        slots = _half_tiles(out_ref, uext_ref.at[pl.ds(POOL_HIST, tm)])
        rows = tm // 2
        assert POOL_GROUP // rows == len(slots) and N_POOL_GROUPS * POOL_GROUP == D_MODEL
        grp = [(wgrp_ref.at[layer, pl.ds(g * POOL_GROUP + r * rows, rows)],
                slots[r].at[:, pl.ds(g * POOL_GROUP, POOL_GROUP)],
                wgrp16_ref.at[pl.ds(g * POOL_GROUP + r * rows, rows)])
               for g in range(N_POOL_GROUPS) for r in range(len(slots))]
        grp_copies = [pltpu.make_async_copy(src, stage, sem_ref.at[0]) for src, stage, _ in grp]
        for cp in grp_copies:
            cp.start()
        for cp in grp_copies:
            cp.wait()
        for _, stage, dst in grp:
            dst[...] = stage[...].astype(BF16)
        plan = (_weight_blocks(win_ref.at[layer], win16_ref, rows, D_MODEL)
                + _weight_blocks(wout_ref.at[layer], wout16_ref, rows, D_MODEL))
        _load_weights_bf16(plan, slots, sem_ref)

    @pl.when((s >= 1) & (s <= n_tiles))
    def _():
        i = (s - 1) % tiles_per_seq

        @pl.when(i == 0)
        def _():
            uext_ref[0:POOL_HIST, :] = jnp.zeros((POOL_HIST, POOL_WIDTH), F32)

        x = x_ref[...]
        xg = (x * g_ref[...]).astype(BF16)
        inv_rms = lax.rsqrt(jnp.mean(x * x, axis=-1, keepdims=True) + RMS_EPS)
        u = _dot(xg, win16_ref[:, :POOL_WIDTH]) * inv_rms
        z = _dot(xg, win16_ref[:, POOL_WIDTH:]) * inv_rms
        uext_ref[POOL_HIST:POOL_HIST + tm, :] = u

        pos = i * tm + lax.broadcasted_iota(jnp.int32, (tm, 1), 0)
        sums, inv_cnt = [], []
        for g, w in enumerate(POOL_WINDOWS):
            cols = slice(g * POOL_GROUP, (g + 1) * POOL_GROUP)
            acc = u[:, cols]
            for k in range(1, w):
                acc = acc + uext_ref[POOL_HIST - k:POOL_HIST - k + tm, cols]
            sums.append(acc)
            inv_cnt.append(1.0 / jnp.minimum(pos + 1, w).astype(F32))
        out_ref[...] = _pool_tail(x_ref, u, z, sums, inv_cnt, wgrp16_ref, scale_ref, wout16_ref)

        tail = uext_ref[tm:tm + POOL_HIST, :]
        hist_ref[0] = tail
        uext_ref[0:POOL_HIST, :] = tail

        for src, dst in zip(conv_src, conv_dst):
            dst[...] = src[...].astype(BF16)

    @pl.when(s == n_tiles + 1)
    def _():
        h = _rms(xs_ref[...], g_ref[...]).astype(BF16)
        u = _dot(h, win16_ref[:, :POOL_WIDTH])
        z = _dot(h, win16_ref[:, POOL_WIDTH:])
        for r in range(POOL_BUF - 1):
            newst_ref[r] = st_ref[0, r + 1]
        newst_ref[POOL_BUF - 1] = u
        sums, inv_cnt = [], []
        for g, w in enumerate(POOL_WINDOWS):
            cols = slice(g * POOL_GROUP, (g + 1) * POOL_GROUP)
            acc = u[:, cols]
            for k in range(1, w):
                acc = acc + st_ref[0, POOL_BUF - k, :, cols]
            sums.append(acc)
            inv_cnt.append(1.0 / float(min(PAST_LEN + 1, w)))
        outs_ref[...] = _pool_tail(xs_ref, u, z, sums, inv_cnt, wgrp16_ref, scale_ref, wout16_ref)


def _convert_specs(convert, n_tiles):
    in_specs, out_specs, out_shape = [], [], []
    for arr, lyr in convert:
        _, rows, cols = arr.shape
        step_rows = rows // n_tiles
        in_specs.append(pl.BlockSpec((None, step_rows, cols),
                                     lambda s, lyr=lyr: (lyr, _tile_of(s, n_tiles), 0)))
        out_specs.append(pl.BlockSpec((step_rows, cols), lambda s: (_tile_of(s, n_tiles), 0)))
        out_shape.append(jax.ShapeDtypeStruct((rows, cols), BF16))
    return in_specs, out_specs, out_shape


def _pool_layer(x2d, xs, state, g, weights, scale, *, layer, own_bf16, convert, seq, tm):
    rows = x2d.shape[0]
    n_dec = xs.shape[0]
    n_tiles = rows // tm
    tiles_per_seq = seq // tm
    tile = lambda s: (_tile_of(s, n_tiles), 0)
    one = pl.Buffered(1)
    hbm = pl.BlockSpec(memory_space=pl.ANY)
    conv_in, conv_out, conv_shape = _convert_specs(convert, n_tiles)
    w_in, w_grp, w_out = weights
    return pl.pallas_call(
        functools.partial(_pool_kernel, layer=layer, tm=tm, tiles_per_seq=tiles_per_seq, n_tiles=n_tiles,
                          own_bf16=own_bf16, n_conv=len(convert)),
        grid=(n_tiles + 2,),
        in_specs=[
            pl.BlockSpec((tm, D_MODEL), tile),
            _resident((n_dec, D_MODEL)),
            pl.BlockSpec((1, POOL_BUF, n_dec, POOL_WIDTH), lambda s: (layer, 0, 0, 0), pipeline_mode=one),
            _resident(g.shape),
            hbm,
            hbm,
            _resident(scale.shape),
            hbm,
        ] + conv_in,
        out_specs=[
            pl.BlockSpec((tm, D_MODEL), tile),
            pl.BlockSpec((1, POOL_HIST, POOL_WIDTH),
                         lambda s: (_tile_of(s, n_tiles) // tiles_per_seq, 0, 0)),
            _resident((n_dec, D_MODEL)),
            _resident((POOL_BUF, n_dec, POOL_WIDTH)),
        ] + conv_out,
        out_shape=[
            jax.ShapeDtypeStruct((rows, D_MODEL), F32),
            jax.ShapeDtypeStruct((rows // seq, POOL_HIST, POOL_WIDTH), F32),
            jax.ShapeDtypeStruct((n_dec, D_MODEL), F32),
            jax.ShapeDtypeStruct((POOL_BUF, n_dec, POOL_WIDTH), F32),
        ] + conv_shape,
        scratch_shapes=[
            pltpu.VMEM((D_MODEL, 2 * POOL_WIDTH), BF16),
            pltpu.VMEM((N_POOL_GROUPS * POOL_GROUP, POOL_GROUP), BF16),
            pltpu.VMEM((POOL_WIDTH, D_MODEL), BF16),
            pltpu.VMEM((POOL_HIST + tm, POOL_WIDTH), F32),
            pltpu.SemaphoreType.DMA((4,)),
        ],
        compiler_params=pltpu.CompilerParams(
            dimension_semantics=("arbitrary",), vmem_limit_bytes=VMEM_LIMIT),
        name="pool_layer",
    )(x2d, xs, state, g, w_in, w_grp, scale, w_out, *[arr for arr, _ in convert])


def _kv_project(x, g_ref, w16_ref, kg_ref):
    h = _rms(x, g_ref[...]).astype(BF16)
    kv = _dot(h, w16_ref[...])
    pair_lo = lax.broadcasted_iota(jnp.int32, (1, PAIR), 1) < HEAD_DIM
    k_pairs = []
    for p in range(KV_WIDTH // PAIR):
        kp = kv[:, p * PAIR:(p + 1) * PAIR]
        k_pairs.append(kp * _pair_inv_rms(kp, pair_lo, 1.0) * kg_ref[...])
    return k_pairs, kv[:, KV_WIDTH:]


def _dup_heads(col, pair_lo):
    rolled = pltpu.roll(col, HEAD_DIM, 1)
    return jnp.where(pair_lo, col, rolled), jnp.where(pair_lo, rolled, col)


def _slide_window(cache_ref, new_t, win_ref, win16_ref):
    n, _, n_keys = cache_ref.shape
    last = lax.broadcasted_iota(jnp.int32, (1, n_keys), 1) == n_keys - 1
    for b in range(n):
        slid = jnp.where(last, new_t[:, b:b + 1], pltpu.roll(cache_ref[b], n_keys - 1, 1))
        win_ref[b] = slid
        win16_ref[b] = slid.astype(BF16)


def _kv_kernel(x_ref, xs_ref, ck_ref, cv_ref, g_ref, wkv_ref, kg_ref, kgc_ref,
               kt_ref, vb_ref, kwin_ref, vwin_ref, kws_ref, vws_ref, k16_ref, v16_ref,
               w16_ref, wkt16_ref, ring_ref, sem_ref,
               *, tm, chunk, n_tiles, dec_steps):
    s = pl.program_id(0)
    nblk = tm // BLOCK
    blk_per_chunk = chunk // BLOCK
    pair_lo = lax.broadcasted_iota(jnp.int32, (1, PAIR), 1) < HEAD_DIM

    n_slots = ring_ref.shape[0]

    def fetch(t):
        return pltpu.make_async_copy(x_ref.at[pl.ds(t * tm, tm)], ring_ref.at[t % n_slots],
                                     sem_ref.at[t % n_slots])

    @pl.when(s == 0)
    def _():
        for t in range(min(n_slots - 1, n_tiles)):
            fetch(t).start()

    @pl.when(s + n_slots - 1 < n_tiles)
    def _():
        fetch(s + n_slots - 1).start()

    fetch(s).wait()
    x_tile = ring_ref.at[s % n_slots]

    @pl.when(s == 0)
    def _():
        w16_ref[...] = wkv_ref[...].astype(BF16)
        for c in range(KV_WIDTH // BLOCK):
            cols = slice(c * BLOCK, (c + 1) * BLOCK)
            wkt16_ref[cols, :] = wkv_ref[:, cols].T.astype(BF16)

    @pl.when(s < dec_steps)
    def _():
        k_pairs, v = _kv_project(xs_ref[...], g_ref, w16_ref, kg_ref)
        n_dec = xs_ref.shape[0]
        pad = jnp.zeros((BLOCK - n_dec, KV_WIDTH), F32)
        kn_t = jnp.concatenate([jnp.concatenate(k_pairs, axis=1), pad], axis=0)
        vn_t = jnp.concatenate([v, pad], axis=0)
        kn_t = jnp.concatenate([kn_t[:, c * BLOCK:(c + 1) * BLOCK].T for c in range(KV_WIDTH // BLOCK)], axis=0)
        vn_t = jnp.concatenate([vn_t[:, c * BLOCK:(c + 1) * BLOCK].T for c in range(KV_WIDTH // BLOCK)], axis=0)
        _slide_window(ck_ref, kn_t, kws_ref, k16_ref)
        _slide_window(cv_ref, vn_t, vws_ref, v16_ref)

    for ch in range(tm // chunk):
        h = _rms(x_tile[ch * chunk:(ch + 1) * chunk, :], g_ref[...]).astype(BF16)
        v = _dot(h, w16_ref[:, KV_WIDTH:])
        k_t = lax.dot_general(wkt16_ref[...], h, (((1,), (1,)), ((), ())),
                              preferred_element_type=F32)
        k_heads = []
        for kh in range(N_KV_HEADS):
            kc = k_t[kh * HEAD_DIM:(kh + 1) * HEAD_DIM, :]
            ms = jnp.mean(kc * kc, axis=0, keepdims=True)
            k_heads.append(kc * lax.rsqrt(ms + RMS_EPS) * kgc_ref[...])
        for jc in range(blk_per_chunk):
            jb = ch * blk_per_chunk + jc
            rows = slice(jc * BLOCK, (jc + 1) * BLOCK)
            for p in range(KV_WIDTH // PAIR):
                v_dup = _dup_heads(v[rows, p * PAIR:(p + 1) * PAIR], pair_lo)
                for half in range(2):
                    kh = 2 * p + half
                    cols = slice(kh * PAIR, (kh + 1) * PAIR)
                    kt_f32 = k_heads[kh][:, rows]
                    kt_cur = jnp.concatenate([kt_f32, kt_f32], axis=0).astype(BF16)
                    v_cur = v_dup[half].astype(BF16)
                    if jb == nblk - 1:
                        hd = slice(kh * HEAD_DIM, (kh + 1) * HEAD_DIM)
                        kwin_ref[0, hd, :] = kt_f32
                        vwin_ref[0, hd, :] = v_dup[half].T[:HEAD_DIM]
                    kt_ref[0, jb, kh] = kt_cur
                    vb_ref[0, jb, :, cols] = v_cur


def _shared_kv(x2d, xs, cache_kt, cache_vt, g, w_kv, k_gain_pair, k_gain_col, *, seq, tm, chunk, dec_rows):
    rows = x2d.shape[0]
    bsz = rows // seq
    n_dec = xs.shape[0]
    n_tiles = rows // tm
    tiles_per_seq = seq // tm
    nblk = tm // BLOCK
    dec_steps = n_dec // dec_rows
    assert dec_steps <= n_tiles and dec_steps * dec_rows == n_dec
    dec = lambda s: (jnp.minimum(s, dec_steps - 1), 0, 0)
    win = (dec_rows,) + cache_kt.shape[1:]
    return pl.pallas_call(
        functools.partial(_kv_kernel, tm=tm, chunk=chunk, n_tiles=n_tiles, dec_steps=dec_steps),
        grid=(n_tiles,),
        in_specs=[
            pl.BlockSpec(memory_space=pl.ANY),
            pl.BlockSpec((dec_rows, D_MODEL), lambda s: (jnp.minimum(s, dec_steps - 1), 0)),
            pl.BlockSpec(win, dec),
            pl.BlockSpec(win, dec),
            _resident((1, D_MODEL)),
            _resident((D_MODEL, 2 * KV_WIDTH)),
            _resident((1, PAIR)),
            _resident((HEAD_DIM, 1)),
        ],
        out_specs=[
            pl.BlockSpec((1, nblk, N_KV_HEADS, PAIR, BLOCK),
                         lambda s: (s // tiles_per_seq, s % tiles_per_seq, 0, 0, 0)),
            pl.BlockSpec((1, nblk, BLOCK, N_KV_HEADS * PAIR),
                         lambda s: (s // tiles_per_seq, s % tiles_per_seq, 0, 0)),
            pl.BlockSpec((1, KV_WIDTH, BLOCK), lambda s: (s // tiles_per_seq, 0, 0)),
            pl.BlockSpec((1, KV_WIDTH, BLOCK), lambda s: (s // tiles_per_seq, 0, 0)),
            pl.BlockSpec(win, dec),
            pl.BlockSpec(win, dec),
            pl.BlockSpec(win, dec),
            pl.BlockSpec(win, dec),
        ],
        out_shape=[
            jax.ShapeDtypeStruct((bsz, seq // BLOCK, N_KV_HEADS, PAIR, BLOCK), BF16),
            jax.ShapeDtypeStruct((bsz, seq // BLOCK, BLOCK, N_KV_HEADS * PAIR), BF16),
            jax.ShapeDtypeStruct((bsz, KV_WIDTH, BLOCK), F32),
            jax.ShapeDtypeStruct((bsz, KV_WIDTH, BLOCK), F32),
            jax.ShapeDtypeStruct(cache_kt.shape, F32),
            jax.ShapeDtypeStruct(cache_vt.shape, F32),
            jax.ShapeDtypeStruct(cache_kt.shape, BF16),
            jax.ShapeDtypeStruct(cache_vt.shape, BF16),
        ],
        scratch_shapes=[
            pltpu.VMEM((D_MODEL, 2 * KV_WIDTH), BF16),
            pltpu.VMEM((KV_WIDTH, D_MODEL), BF16),
            pltpu.VMEM((KV_RING_SLOTS, tm, D_MODEL), F32),
            pltpu.SemaphoreType.DMA((KV_RING_SLOTS,)),
        ],
        compiler_params=pltpu.CompilerParams(
            dimension_semantics=("arbitrary",), vmem_limit_bytes=VMEM_LIMIT),
        name="shared_kv",
    )(x2d, xs, cache_kt, cache_vt, g, w_kv, k_gain_pair, k_gain_col)


def _normed_q_pair(q, pr, qg_ref, pair_lo, scale):
    cols = slice(pr * PAIR, (pr + 1) * PAIR)
    qp = q[:, cols]
    return qp * _pair_inv_rms(qp, pair_lo, scale) * qg_ref[:, cols]


def _decode_attention(q, k3, v3, sink_col):
    n, n_keys = k3.shape[0], k3.shape[2]
    gw = GQA_GROUP * HEAD_DIM
    rows = n * GQA_GROUP
    ri = lax.broadcasted_iota(jnp.int32, (rows, n), 0)
    ci = lax.broadcasted_iota(jnp.int32, (rows, n), 1)
    rep = jnp.where(ri // GQA_GROUP == ci, 1.0, 0.0).astype(BF16)
    row_g = lax.broadcasted_iota(jnp.int32, (rows, gw), 0) % GQA_GROUP
    col_g = lax.broadcasted_iota(jnp.int32, (rows, gw), 1) // HEAD_DIM
    own = row_g == col_g
    fj = lax.broadcasted_iota(jnp.int32, (gw, KV_WIDTH), 0)
    fc = lax.broadcasted_iota(jnp.int32, (gw, KV_WIDTH), 1)
    tc = lax.broadcasted_iota(jnp.int32, (KV_WIDTH, gw), 0)
    tj = lax.broadcasted_iota(jnp.int32, (KV_WIDTH, gw), 1)
    qb = q.astype(BF16)

    q_groups = []
    for kh in range(N_KV_HEADS):
        to_kv = jnp.where((fj % HEAD_DIM == fc % HEAD_DIM) & (fc // HEAD_DIM == kh), 1.0, 0.0).astype(BF16)
        mine = jnp.where(own, _dot(rep, qb[:, kh * gw:(kh + 1) * gw]), 0.0).astype(BF16)
        q_groups.append(_dot(mine, to_kv).astype(BF16).reshape(n, GQA_GROUP, KV_WIDTH))
    qm3 = jnp.concatenate(q_groups, axis=1)
    s = jnp.einsum("bhc,bck->bhk", qm3, k3, preferred_element_type=F32)
    h_col = lax.broadcasted_iota(jnp.int32, (N_HEADS, 1), 0).astype(F32)
    slope = jnp.exp2(-8.0 * (h_col + 1.0) / N_HEADS)[None]
    dist = (n_keys - 1 - lax.broadcasted_iota(jnp.int32, (1, n_keys), 1)).astype(F32)[None]
    s = s - slope * dist
    sink = sink_col[None]
    m = jnp.maximum(jnp.max(s, axis=-1, keepdims=True), sink)
    p = jnp.exp(s - m)
    den = jnp.sum(p, axis=-1, keepdims=True) + jnp.exp(sink - m)
    o3 = jnp.einsum("bhk,bck->bhc", p.astype(BF16), v3, preferred_element_type=F32) / den
    outs = []
    for kh in range(N_KV_HEADS):
        from_kv = jnp.where((tj % HEAD_DIM == tc % HEAD_DIM) & (tc // HEAD_DIM == kh), 1.0, 0.0).astype(BF16)
        o2 = o3[:, kh * GQA_GROUP:(kh + 1) * GQA_GROUP, :].reshape(rows, KV_WIDTH)
        hi = o2.astype(BF16)
        lo = (o2 - hi.astype(F32)).astype(BF16)
        spread = jnp.where(own, _dot(hi, from_kv) + _dot(lo, from_kv), 0.0)
        outs.append(jnp.sum(spread.reshape(n, GQA_GROUP, gw), axis=1))
    return jnp.concatenate(outs, axis=-1)


def _attn_kernel(sink_ref, x_ref, xs_ref, k3_ref, v3_ref, sinkc_ref, g_ref, win_ref, qg_ref,
                 kt_ref, vb_ref, ktp_ref, vbp_ref, wout_ref,
                 out_ref, outs_ref,
                 win16_ref, wout16_ref, h_ref, qlo_ref, qhi_ref, kbuf_ref, vbuf_ref, sem_ref,
                 *, layer, tm, tiles_per_seq, n_tiles):
    s = pl.program_id(0)
    nblk = tm // BLOCK
    pair_lo = lax.broadcasted_iota(jnp.int32, (1, PAIR), 1) < HEAD_DIM
    g_ref, qg_ref = g_ref.at[pl.ds(layer, 1)], qg_ref.at[pl.ds(layer, 1)]

    @pl.when(s == 0)
    def _():
        _fetch_bf16([(win_ref, win16_ref), (wout_ref, wout16_ref)], sem_ref)

    @pl.when((s >= 1) & (s <= n_tiles))
    def _():
        i = (s - 1) % tiles_per_seq
        h_ref[...] = _rms(x_ref[...], g_ref[...]).astype(BF16)
        for kh in range(N_KV_HEADS):
            gcols = slice(kh * GQA_GROUP * HEAD_DIM, (kh + 1) * GQA_GROUP * HEAD_DIM)
            q = _dot(h_ref[...], win16_ref[:, gcols])
            for pr in range(GQA_GROUP // 2):
                cols = slice(pr * PAIR, (pr + 1) * PAIR)
                ocols = slice((kh * (GQA_GROUP // 2) + pr) * PAIR, (kh * (GQA_GROUP // 2) + pr + 1) * PAIR)
                qp = q[:, cols]
                qn = qp * _pair_inv_rms(qp, pair_lo, HEAD_DIM ** -0.5 * LOG2E) * qg_ref[:, ocols]
                qlo_ref[:, ocols] = jnp.where(pair_lo, qn, 0.0).astype(BF16)
                qhi_ref[:, ocols] = jnp.where(pair_lo, 0.0, qn).astype(BF16)

        r_idx = lax.broadcasted_iota(jnp.int32, (BLOCK, 2 * BLOCK), 0)
        c_idx = lax.broadcasted_iota(jnp.int32, (BLOCK, 2 * BLOCK), 1)
        dist = r_idx + BLOCK - c_idx
        in_window = (dist >= 0) & (dist < WINDOW)
        neg_dist = -dist.astype(F32)

        for blk in range(nblk):
            rows = slice(blk * BLOCK, (blk + 1) * BLOCK)
            first_key = (i * nblk + blk - 1) * BLOCK
            valid = in_window & (first_key + c_idx >= 0)
            nd = jnp.where(valid, neg_dist, NEG_INF / MIN_SLOPE)
            for kh in range(N_KV_HEADS):
                vcols = slice(kh * PAIR, (kh + 1) * PAIR)
                if blk == 0:
                    kt_prev, v_prev = ktp_ref[0, 0, kh], vbp_ref[0, 0, :, vcols]
                else:
                    kt_prev, v_prev = kt_ref[0, blk - 1, kh], vb_ref[0, blk - 1, :, vcols]
                kt = jnp.concatenate([kt_prev, kt_ref[0, blk, kh]], axis=1)
                vdup = jnp.concatenate([v_prev, vb_ref[0, blk, :, vcols]], axis=0)
                pieces = []
                for pr in range(GQA_GROUP // 2):
                    cols = slice((kh * (GQA_GROUP // 2) + pr) * PAIR,
                                 (kh * (GQA_GROUP // 2) + pr + 1) * PAIR)
                    pieces.append(qlo_ref[rows, cols])
                    pieces.append(qhi_ref[rows, cols])
                s_all = _dot(jnp.concatenate(pieces, axis=0), kt)
                probs, inv_den = [], []
                for g_idx in range(GQA_GROUP):
                    head = kh * GQA_GROUP + g_idx
                    slope2 = 2.0 ** (-8.0 * (head + 1) / N_HEADS) * LOG2E
                    sc = s_all[g_idx * BLOCK:(g_idx + 1) * BLOCK] + slope2 * nd
                    sink2 = sink_ref[layer, head] * LOG2E
                    m = jnp.maximum(jnp.max(sc, axis=-1, keepdims=True), sink2)
                    p = jnp.exp2(sc - m)
                    den = jnp.sum(p, axis=-1, keepdims=True) + jnp.exp2(sink2 - m)
                    probs.append(p.astype(BF16))
                    inv_den.append(1.0 / den)
                o_all = _dot(jnp.concatenate(probs, axis=0), vdup)
                for pr in range(GQA_GROUP // 2):
                    cols = slice((kh * (GQA_GROUP // 2) + pr) * PAIR,
                                 (kh * (GQA_GROUP // 2) + pr + 1) * PAIR)
                    o_even = o_all[(2 * pr) * BLOCK:(2 * pr + 1) * BLOCK] * inv_den[2 * pr]
                    o_odd = o_all[(2 * pr + 1) * BLOCK:(2 * pr + 2) * BLOCK] * inv_den[2 * pr + 1]
                    out_ref[rows, cols] = jnp.where(pair_lo, o_even, o_odd)

        z = _dot(h_ref[...], win16_ref[:, ATTN_WIDTH:])
        y = out_ref[...] * _silu(z)
        out_ref[...] = x_ref[...] + _dot(y.astype(BF16), wout16_ref[...])

    @pl.when(s == n_tiles + 1)
    def _():
        kv_in = [pltpu.make_async_copy(k3_ref, kbuf_ref, sem_ref.at[0]),
                 pltpu.make_async_copy(v3_ref, vbuf_ref, sem_ref.at[1])]
        for cp in kv_in:
            cp.start()
        x = xs_ref[...]
        h = _rms(x, g_ref[...]).astype(BF16)
        q = _dot(h, win16_ref[:, :ATTN_WIDTH])
        z = _dot(h, win16_ref[:, ATTN_WIDTH:])
        qn = jnp.concatenate(
            [_normed_q_pair(q, pr, qg_ref, pair_lo, HEAD_DIM ** -0.5) for pr in range(N_PAIRS)], axis=-1)
        for cp in kv_in:
            cp.wait()
        o = _decode_attention(qn, kbuf_ref[...], vbuf_ref[...], sinkc_ref[:, layer:layer + 1])
        y = o * _silu(z)
        outs_ref[...] = x + _dot(y.astype(BF16), wout16_ref[...])


def _attn_layer(x2d, xs, k3, v3, sinks, sinks_t, g, w_in, q_gain, kt_band, v_band, w_out,
                *, layer, seq, tm):
    rows = x2d.shape[0]
    n_dec = xs.shape[0]
    n_tiles = rows // tm
    tiles_per_seq = seq // tm
    nblk = tm // BLOCK
    tile = lambda s: (_tile_of(s, n_tiles), 0)
    band = lambda s: (_tile_of(s, n_tiles) // tiles_per_seq, _tile_of(s, n_tiles) % tiles_per_seq)
    prev_block = lambda s: (band(s)[0], jnp.maximum(band(s)[1] * nblk - 1, 0))
    one = pl.Buffered(1)
    hbm = pl.BlockSpec(memory_space=pl.ANY)
    return pl.pallas_call(
        functools.partial(_attn_kernel, layer=layer, tm=tm, tiles_per_seq=tiles_per_seq, n_tiles=n_tiles),
        grid=(n_tiles + 2,),
        in_specs=[
            pl.BlockSpec(memory_space=pltpu.SMEM),
            pl.BlockSpec((tm, D_MODEL), tile),
            _resident((n_dec, D_MODEL)),
            hbm,
            hbm,
            _resident(sinks_t.shape),
            _resident(g.shape),
            hbm,
            _resident(q_gain.shape),
            pl.BlockSpec((1, nblk, N_KV_HEADS, PAIR, BLOCK), lambda s: band(s) + (0, 0, 0)),
            pl.BlockSpec((1, nblk, BLOCK, N_KV_HEADS * PAIR), lambda s: band(s) + (0, 0)),
            pl.BlockSpec((1, 1, N_KV_HEADS, PAIR, BLOCK), lambda s: prev_block(s) + (0, 0, 0)),
            pl.BlockSpec((1, 1, BLOCK, N_KV_HEADS * PAIR), lambda s: prev_block(s) + (0, 0)),
            hbm,
        ],
        out_specs=[
            pl.BlockSpec((tm, D_MODEL), tile),
            _resident((n_dec, D_MODEL)),
        ],
        out_shape=[
            jax.ShapeDtypeStruct((rows, D_MODEL), F32),
            jax.ShapeDtypeStruct((n_dec, D_MODEL), F32),
        ],
        scratch_shapes=[
            pltpu.VMEM((D_MODEL, 2 * ATTN_WIDTH), BF16),
            pltpu.VMEM((ATTN_WIDTH, D_MODEL), BF16),
            pltpu.VMEM((tm, D_MODEL), BF16),
            pltpu.VMEM((tm, ATTN_WIDTH), BF16),
            pltpu.VMEM((tm, ATTN_WIDTH), BF16),
            pltpu.VMEM(k3.shape, BF16),
            pltpu.VMEM(v3.shape, BF16),
            pltpu.SemaphoreType.DMA((2,)),
        ],
        compiler_params=pltpu.CompilerParams(
            dimension_semantics=("arbitrary",), vmem_limit_bytes=VMEM_LIMIT),
        name="attn_layer",
    )(sinks, x2d, xs, k3, v3, sinks_t, g, w_in, q_gain, kt_band, v_band, kt_band, v_band, w_out)


def kernel(x_prompt, x_sample, state_pool, cache_k_win, cache_v_win, norm_a, w_in_a, w_grp_a, scale_a, w_out_a, norm_kv, w_kv, k_norm, norm_b, w_in_b, q_norm, sinks, w_out_b):
    bsz, seq, _ = x_prompt.shape
    n_dec = x_sample.shape[0]
    assert x_sample.shape[1] == 1 and cache_k_win.shape[1] == WINDOW
    tm_a, tm_b, tm_kv = 256, 256, 1024
    kv_chunk, kv_dec_rows = 256, 8

    hp = x_prompt.reshape(bsz * seq, D_MODEL)
    hs = x_sample[:, 0]
    state_t = state_pool.transpose(0, 2, 1, 3)
    w_grp_rows = w_grp_a.reshape(N_A_LAYERS, N_POOL_GROUPS * POOL_GROUP, POOL_GROUP)

    assert N_A_LAYERS == 2 and N_B_LAYERS == 2
    hp, hist0, hs, state0, a1_in, a1_grp, a1_out, b1_in, b1_out = _pool_layer(
        hp, hs, state_t, norm_a, (w_in_a, w_grp_rows, w_out_a), scale_a, layer=0, own_bf16=False,
        convert=[(w_in_a, 1), (w_grp_rows, 1), (w_out_a, 1), (w_in_b, 1), (w_out_b, 1)], seq=seq, tm=tm_a)
    hp, hist1, hs, state1, b0_in, b0_out = _pool_layer(
        hp, hs, state_t, norm_a, (a1_in, a1_grp, a1_out), scale_a, layer=1, own_bf16=True,
        convert=[(w_in_b, 0), (w_out_b, 0)], seq=seq, tm=tm_a)
    pool_p = [h[:, POOL_HIST - POOL_BUF:] for h in (hist0, hist1)]
    pool_s = [state0, state1]
    attn_weights = [(b0_in, b0_out), (b1_in, b1_out)]

    def keys_minor(t):
        return t.transpose(0, 2, 3, 1).reshape(t.shape[0], KV_WIDTH, t.shape[1])

    def keys_major(t):
        return t.reshape(t.shape[0], N_KV_HEADS, HEAD_DIM, t.shape[2]).transpose(0, 3, 1, 2)

    kt_band, v_band, k_win_p, v_win_p, k_win_s, v_win_s, k16, v16 = _shared_kv(
        hp, hs, keys_minor(cache_k_win), keys_minor(cache_v_win),
        norm_kv[None], w_kv, jnp.tile(k_norm, 2)[None], k_norm[:, None],
        seq=seq, tm=tm_kv, chunk=kv_chunk, dec_rows=kv_dec_rows)

    q_gain = jnp.tile(q_norm, (1, N_HEADS))
    for j in range(N_B_LAYERS):
        hp, hs = _attn_layer(hp, hs, k16, v16, sinks, sinks.T, norm_b, attn_weights[j][0], q_gain,
                             kt_band, v_band, attn_weights[j][1], layer=j, seq=seq, tm=tm_b)

    return (hp.reshape(bsz, seq, D_MODEL), hs[:, None],
            jnp.stack(pool_p, axis=0), jnp.stack(pool_s, axis=0).transpose(0, 2, 1, 3),
            keys_major(k_win_p), keys_major(v_win_p), keys_major(k_win_s), keys_major(v_win_s))
```
